```python
import jax, jax.numpy as jnp
from jax import lax
import numpy as np

D_MODEL = 1024
BATCH = 4
SEQ = 8192
DEPTH = 2

N_MEM = 256
N_EVEN = (DEPTH + 1) // 2
N_ODD = DEPTH // 2
EPS = 1e-6
ROPE_THETA = 10000.0

GLA_HEADS = 4
GLA_DK = D_MODEL // 16
GLA_DV = D_MODEL // 8
GLA_GATE_RANK = 16
GLA_TAU = 16.0
GLA_CHUNK = 64

SB_HEADS = 8
SB_DH = D_MODEL // 16
SB_Q_BLOCK = 128

MOBA_HEADS = 8
MOBA_DH = D_MODEL // 16
MOBA_BLOCK = 256
MOBA_TOPK = 3
MOBA_Q_BLOCK = 64

RG_WIDTH = D_MODEL // 2
RG_BLOCKS = 8
RG_BDIM = RG_WIDTH // RG_BLOCKS
RG_CONV = 4
RG_C = 8.0

XA_HEADS = 4
XA_DH = D_MODEL // 8

PEER_HEADS = 8
PEER_NKEYS = 128
PEER_N = PEER_NKEYS * PEER_NKEYS
PEER_DQ = D_MODEL // 4
PEER_TOPK = 16
PEER_T_BLOCK = 128

EVEN_WIDTHS = (GLA_HEADS * GLA_DK, GLA_HEADS * GLA_DK, GLA_HEADS * GLA_DV, GLA_HEADS * GLA_DV,
               GLA_GATE_RANK, SB_HEADS * SB_DH, SB_HEADS * SB_DH, SB_HEADS * SB_DH)
EVEN_IN = sum(EVEN_WIDTHS)
EVEN_OUT = GLA_HEADS * GLA_DV + SB_HEADS * SB_DH
ODD_WIDTHS = (MOBA_HEADS * MOBA_DH, MOBA_HEADS * MOBA_DH, MOBA_HEADS * MOBA_DH, RG_WIDTH, RG_WIDTH)
ODD_IN = sum(ODD_WIDTHS)
ODD_OUT = MOBA_HEADS * MOBA_DH + RG_WIDTH

kernel_name = "hybrid_gla_sb_moba_rglru_peer"


def _split(z, widths):
    offs, acc = [], 0
    for w in widths[:-1]:
        acc += w
        offs.append(acc)
    return jnp.split(z, offs, axis=-1)


def rmsnorm(x, g):
    xf = x.astype(jnp.float32)
    y = xf * lax.rsqrt(jnp.mean(xf * xf, axis=-1, keepdims=True) + EPS)
    return (y * g.astype(jnp.float32)).astype(x.dtype)


def rope_tables(positions, dh):
    inv_freq = ROPE_THETA ** (-jnp.arange(0, dh, 2, dtype=jnp.float32) / dh)
    ang = positions.astype(jnp.float32)[..., None] * inv_freq
    return jnp.cos(ang)[:, :, None, :], jnp.sin(ang)[:, :, None, :]


def apply_rope(t, cos, sin):
    tf = t.astype(jnp.float32)
    t1, t2 = jnp.split(tf, 2, axis=-1)
    return jnp.concatenate([t1 * cos - t2 * sin, t2 * cos + t1 * sin], axis=-1).astype(t.dtype)


def gla_attention(q, k, v, log_a):
    B, S, H, DK = q.shape
    DV = v.shape[-1]
    C = GLA_CHUNK
    NC = S // C

    def to_chunks(t):
        return t.reshape(B, NC, C, H, t.shape[-1]).transpose(1, 0, 3, 2, 4)

    qc, kc, vc, gc = to_chunks(q * (DK ** -0.5)), to_chunks(k), to_chunks(v), to_chunks(log_a)
    causal = jnp.tril(jnp.ones((C, C), dtype=bool))[:, :, None]

    def step(state, inp):
        qi, ki, vi, gi = inp
        b = jnp.cumsum(gi.astype(jnp.float32), axis=-2)
        o_inter = jnp.einsum('bhtk,bhkv->bhtv', qi * jnp.exp(b), state)
        diff = b[:, :, :, None, :] - b[:, :, None, :, :]
        decay = jnp.exp(jnp.where(causal, diff, -jnp.inf))
        scores = jnp.einsum('bhtk,bhsk,bhtsk->bhts', qi, ki, decay)
        o_intra = jnp.einsum('bhts,bhsv->bhtv', scores, vi)
        b_last = b[:, :, -1:, :]
        k_dec = ki * jnp.exp(b_last - b)
        state = state * jnp.exp(b_last[:, :, 0, :, None]) + jnp.einsum('bhsk,bhsv->bhkv', k_dec, vi)
        return state, o_inter + o_intra

    state0 = jnp.zeros((B, H, DK, DV), jnp.float32)
    _, out = lax.scan(step, state0, (qc, kc, vc, gc))
    return out.transpose(1, 0, 3, 2, 4).reshape(B, S, H, DV).astype(v.dtype)


def stick_breaking_attention(q, k, v):
    B, S, H, D = q.shape
    NQ = S // SB_Q_BLOCK
    qb = q.reshape(B, NQ, SB_Q_BLOCK, H, D).transpose(1, 0, 3, 2, 4)
    kt = k.transpose(0, 2, 1, 3)
    vt = v.transpose(0, 2, 1, 3)
    kpos = jnp.arange(S)

    def block(args):
        qi, i = args
        qpos = i * SB_Q_BLOCK + jnp.arange(SB_Q_BLOCK)
        past = kpos[None, :] < qpos[:, None]
        z = jnp.einsum('bhqd,bhsd->bhqs', qi, kt).astype(jnp.float32) * (D ** -0.5)
        log_beta = jax.nn.log_sigmoid(z)
        log_rem = jnp.where(past, -jax.nn.softplus(z), 0.0)
        after = lax.cumsum(log_rem, axis=3, reverse=True) - log_rem
        w = jnp.where(past, jnp.exp(log_beta + after), 0.0)
        return jnp.einsum('bhqs,bhsd->bhqd', w.astype(vt.dtype), vt)

    out = lax.map(block, (qb, jnp.arange(NQ)))
    return out.transpose(1, 0, 3, 2, 4).reshape(B, S, H * D)


def moba_attention(q, k, v):
    B, S, H, D = q.shape
    NB = -(-S // MOBA_BLOCK)
    S_pad = NB * MOBA_BLOCK
    K_EFF = min(MOBA_TOPK, NB)
    pad = ((0, 0), (0, S_pad - S), (0, 0), (0, 0))
    kp = jnp.pad(k, pad).transpose(0, 2, 1, 3).reshape(B, H, NB, MOBA_BLOCK, D)
    vp = jnp.pad(v, pad).transpose(0, 2, 1, 3).reshape(B, H, NB, MOBA_BLOCK, D)
    k_mean = jnp.mean(kp.astype(jnp.float32), axis=3)
    NQ = S // MOBA_Q_BLOCK
    qb = q.reshape(B, NQ, MOBA_Q_BLOCK, H, D).transpose(1, 0, 3, 2, 4)
    scale = D ** -0.5
    bi = jnp.arange(B)[:, None, None, None]
    hi = jnp.arange(H)[None, :, None, None]

    def block(args):
        qi, i = args
        qpos = i * MOBA_Q_BLOCK + jnp.arange(MOBA_Q_BLOCK)
        own = (i * MOBA_Q_BLOCK) // MOBA_BLOCK
        gate = jnp.einsum('bhqd,bhnd->bhqn', qi.astype(jnp.float32), k_mean)
        gate = jnp.where(jnp.arange(NB)[None, :] < own, gate, -jnp.inf)
        _, sel = lax.top_k(gate, K_EFF)
        valid = jnp.arange(K_EFF) < own
        kg = kp[bi, hi, sel]
        vg = vp[bi, hi, sel]
        s_sel = jnp.einsum('bhqd,bhqnsd->bhqns', qi, kg).astype(jnp.float32) * scale
        s_sel = jnp.where(valid[:, None], s_sel, -jnp.inf)
        k_own = lax.dynamic_index_in_dim(kp, own, axis=2, keepdims=False)
        v_own = lax.dynamic_index_in_dim(vp, own, axis=2, keepdims=False)
        s_own = jnp.einsum('bhqd,bhsd->bhqs', qi, k_own).astype(jnp.float32) * scale
        own_pos = own * MOBA_BLOCK + jnp.arange(MOBA_BLOCK)
        s_own = jnp.where(own_pos[None, :] <= qpos[:, None], s_own, -jnp.inf)
        Qb = qi.shape[2]
        logits = jnp.concatenate([s_sel.reshape(B, H, Qb, K_EFF * MOBA_BLOCK), s_own], axis=-1)
        p = jax.nn.softmax(logits, axis=-1).astype(v.dtype)
        p_sel = p[..., :K_EFF * MOBA_BLOCK].reshape(B, H, Qb, K_EFF, MOBA_BLOCK)
        p_own = p[..., K_EFF * MOBA_BLOCK:]
        return (jnp.einsum('bhqns,bhqnsd->bhqd', p_sel, vg)
                + jnp.einsum('bhqs,bhsd->bhqd', p_own, v_own))

    out = lax.map(block, (qb, jnp.arange(NQ)))
    return out.transpose(1, 0, 3, 2, 4).reshape(B, S, H * D)


def rg_lru(xb, conv_w, conv_b, wa, ba, wx, bx, lam):
    B, S, W = xb.shape
    xc = lax.conv_general_dilated(xb, conv_w[:, None, :].astype(xb.dtype), window_strides=(1,),
                                  padding=[(RG_CONV - 1, 0)],
                                  dimension_numbers=('NWC', 'WIO', 'NWC'),
                                  feature_group_count=W) + conv_b
    xf = xc.astype(jnp.float32)
    xg = xf.reshape(B, S, RG_BLOCKS, RG_BDIM)
    r = jax.nn.sigmoid(jnp.einsum('bsgi,gij->bsgj', xg, wa.astype(jnp.float32)).reshape(B, S, W) + ba)
    ig = jax.nn.sigmoid(jnp.einsum('bsgi,gij->bsgj', xg, wx.astype(jnp.float32)).reshape(B, S, W) + bx)
    log_a = -RG_C * r * jax.nn.softplus(-lam.astype(jnp.float32))
    a = jnp.exp(log_a)
    u = jnp.sqrt(-jnp.expm1(2.0 * log_a)) * (ig * xf)

    def combine(left, right):
        a1, b1 = left
        a2, b2 = right
        return a1 * a2, a2 * b1 + b2

    _, h = lax.associative_scan(combine, (a, u), axis=1)
    return h.astype(xb.dtype)


def even_mixer(h, w_in, gate_up, gate_b, gla_out_norm, w_out):
    B, S, _ = h.shape
    gq, gk, gv, gr, gg, sq, sk, sv = _split(h @ w_in, EVEN_WIDTHS)
    log_a = jax.nn.log_sigmoid((gg @ gate_up + gate_b).astype(jnp.float32)) / GLA_TAU
    a_out = gla_attention(gq.reshape(B, S, GLA_HEADS, GLA_DK), gk.reshape(B, S, GLA_HEADS, GLA_DK),
                          gv.reshape(B, S, GLA_HEADS, GLA_DV), log_a.reshape(B, S, GLA_HEADS, GLA_DK))
    a_out = rmsnorm(a_out, gla_out_norm).reshape(B, S, GLA_HEADS * GLA_DV) * jax.nn.silu(gr)
    shp = (B, S, SB_HEADS, SB_DH)
    b_out = stick_breaking_attention(sq.reshape(shp), sk.reshape(shp), sv.reshape(shp))
    return jnp.concatenate([a_out, b_out], axis=-1) @ w_out


def odd_mixer(h, cos, sin, w_in, q_norm, k_norm, conv_w, conv_b, wa, ba, wx, bx, lam, w_out):
    B, S, _ = h.shape
    mq, mk, mv, rx, rg = _split(h @ w_in, ODD_WIDTHS)
    shp = (B, S, MOBA_HEADS, MOBA_DH)
    q = apply_rope(rmsnorm(mq.reshape(shp), q_norm), cos, sin)
    k = apply_rope(rmsnorm(mk.reshape(shp), k_norm), cos, sin)
    c_out = moba_attention(q, k, mv.reshape(shp))
    d_out = rg_lru(rx, conv_w, conv_b, wa, ba, wx, bx, lam) * jax.nn.gelu(rg)
    return jnp.concatenate([c_out, d_out], axis=-1) @ w_out


def memory_cross_attention(xn, memn, wq, wkv, q_norm, k_norm, wo):
    B, S, _ = xn.shape
    M = memn.shape[1]
    q = rmsnorm((xn @ wq).reshape(B, S, XA_HEADS, XA_DH), q_norm)
    k, v = jnp.split(memn @ wkv, 2, axis=-1)
    k = rmsnorm(k.reshape(B, M, XA_HEADS, XA_DH), k_norm)
    v = v.reshape(B, M, XA_HEADS, XA_DH)
    s = jnp.einsum('bshd,bmhd->bhsm', q, k).astype(jnp.float32) * (XA_DH ** -0.5)
    p = jax.nn.softmax(s, axis=-1).astype(v.dtype)
    o = jnp.einsum('bhsm,bmhd->bshd', p, v).reshape(B, S, XA_HEADS * XA_DH)
    return o @ wo


def peer_ffn(xn, wq, subkeys, u, v):
    B, S, D = xn.shape
    T = B * S
    NT = T // PEER_T_BLOCK
    K = PEER_TOPK

    def block(xi):
        Tb = xi.shape[0]
        q = (xi @ wq).reshape(Tb, PEER_HEADS, 2, PEER_DQ // 2)
        s = jnp.einsum('thcd,hcnd->thcn', q, subkeys).astype(jnp.float32)
        s1, i1 = lax.top_k(s[:, :, 0], K)
        s2, i2 = lax.top_k(s[:, :, 1], K)
        cand = (s1[..., :, None] + s2[..., None, :]).reshape(Tb, PEER_HEADS, K * K)
        cidx = (i1[..., :, None] * PEER_NKEYS + i2[..., None, :]).reshape(Tb, PEER_HEADS, K * K)
        top_s, pos = lax.top_k(cand, K)
        eidx = jnp.take_along_axis(cidx, pos, axis=-1)
        g = jax.nn.softmax(top_s, axis=-1)
        ue = u[eidx]
        ve = v[eidx]
        act = jax.nn.gelu(jnp.einsum('td,thkd->thk', xi, ue).astype(jnp.float32), approximate=False)
        return jnp.einsum('thk,thkd->td', (g * act).astype(ve.dtype), ve)

    out = lax.map(block, xn.reshape(NT, PEER_T_BLOCK, D))
    return out.reshape(B, S, D)


def setup_inputs(seed: int = 0) -> dict:
    key = jax.random.key(seed)
    ks = iter(jax.random.split(key, 48))
    f32 = jnp.float32

    def nrm(shape, scale):
        return jax.random.normal(next(ks), shape, f32) * scale

    def gain(shape):
        return 1.0 + 0.02 * jax.random.normal(next(ks), shape, f32)

    x = jax.random.normal(next(ks), (BATCH, SEQ, D_MODEL), f32)
    mem = jax.random.normal(next(ks), (BATCH, N_MEM, D_MODEL), f32)
    offs = jax.random.randint(next(ks), (BATCH, 1), 0, 4096, dtype=jnp.int32)
    positions = offs + jnp.arange(SEQ, dtype=jnp.int32)[None, :]
    a_target = jax.random.uniform(next(ks), (N_ODD, RG_WIDTH), f32, minval=0.9, maxval=0.999)
    s_lam = a_target ** (1.0 / RG_C)
    lam = jnp.log(s_lam) - jnp.log1p(-s_lam)
    return {
        "x": x,
        "mem": mem,
        "positions": positions,
        "ev_norm": gain((N_EVEN, D_MODEL)),
        "ev_w_in": nrm((N_EVEN, D_MODEL, EVEN_IN), D_MODEL ** -0.5),
        "ev_gla_gate_up": nrm((N_EVEN, GLA_GATE_RANK, GLA_HEADS * GLA_DK), GLA_GATE_RANK ** -0.5),
        "ev_gla_gate_b": nrm((N_EVEN, GLA_HEADS * GLA_DK), 0.1),
        "ev_gla_out_norm": gain((N_EVEN, GLA_DV)),
        "ev_w_out": nrm((N_EVEN, EVEN_OUT, D_MODEL), EVEN_OUT ** -0.5),
        "od_norm": gain((N_ODD, D_MODEL)),
        "od_w_in": nrm((N_ODD, D_MODEL, ODD_IN), D_MODEL ** -0.5),
        "od_q_norm": gain((N_ODD, MOBA_DH)),
        "od_k_norm": gain((N_ODD, MOBA_DH)),
        "od_conv_w": nrm((N_ODD, RG_CONV, RG_WIDTH), RG_CONV ** -0.5),
        "od_conv_b": nrm((N_ODD, RG_WIDTH), 0.02),
        "od_gate_a_w": nrm((N_ODD, RG_BLOCKS, RG_BDIM, RG_BDIM), RG_BDIM ** -0.5),
        "od_gate_a_b": nrm((N_ODD, RG_WIDTH), 0.02),
        "od_gate_x_w": nrm((N_ODD, RG_BLOCKS, RG_BDIM, RG_BDIM), RG_BDIM ** -0.5),
        "od_gate_x_b": nrm((N_ODD, RG_WIDTH), 0.02),
        "od_lambda": lam,
        "od_w_out": nrm((N_ODD, ODD_OUT, D_MODEL), ODD_OUT ** -0.5),
        "xa_norm": gain((DEPTH, D_MODEL)),
        "xa_mem_norm": gain((DEPTH, D_MODEL)),
        "xa_wq": nrm((DEPTH, D_MODEL, XA_HEADS * XA_DH), D_MODEL ** -0.5),
        "xa_wkv": nrm((DEPTH, D_MODEL, 2 * XA_HEADS * XA_DH), D_MODEL ** -0.5),
        "xa_q_norm": gain((DEPTH, XA_DH)),
        "xa_k_norm": gain((DEPTH, XA_DH)),
        "xa_wo": nrm((DEPTH, XA_HEADS * XA_DH, D_MODEL), (XA_HEADS * XA_DH) ** -0.5),
        "ffn_norm": gain((DEPTH, D_MODEL)),
        "peer_wq": nrm((DEPTH, D_MODEL, PEER_HEADS * PEER_DQ), D_MODEL ** -0.5),
        "peer_subkeys": nrm((DEPTH, PEER_HEADS, 2, PEER_NKEYS, PEER_DQ // 2), (PEER_DQ // 2) ** -0.5),
        "peer_u": nrm((DEPTH, PEER_N, D_MODEL), D_MODEL ** -0.5),
        "peer_v": nrm((DEPTH, PEER_N, D_MODEL), PEER_HEADS ** -0.5),
    }


def reference(x, mem, positions,
              ev_norm, ev_w_in, ev_gla_gate_up, ev_gla_gate_b, ev_gla_out_norm, ev_w_out,
              od_norm, od_w_in, od_q_norm, od_k_norm, od_conv_w, od_conv_b,
              od_gate_a_w, od_gate_a_b, od_gate_x_w, od_gate_x_b, od_lambda, od_w_out,
              xa_norm, xa_mem_norm, xa_wq, xa_wkv, xa_q_norm, xa_k_norm, xa_wo,
              ffn_norm, peer_wq, peer_subkeys, peer_u, peer_v):
    cos, sin = rope_tables(positions, MOBA_DH)
    for l in range(DEPTH):
        if l % 2 == 0:
            e = l // 2
            x = x + even_mixer(rmsnorm(x, ev_norm[e]), ev_w_in[e], ev_gla_gate_up[e], ev_gla_gate_b[e],
                               ev_gla_out_norm[e], ev_w_out[e])
        else:
            o = l // 2
            x = x + odd_mixer(rmsnorm(x, od_norm[o]), cos, sin, od_w_in[o], od_q_norm[o], od_k_norm[o],
                              od_conv_w[o], od_conv_b[o], od_gate_a_w[o], od_gate_a_b[o],
                              od_gate_x_w[o], od_gate_x_b[o], od_lambda[o], od_w_out[o])
        x = x + memory_cross_attention(rmsnorm(x, xa_norm[l]), rmsnorm(mem, xa_mem_norm[l]),
                                       xa_wq[l], xa_wkv[l], xa_q_norm[l], xa_k_norm[l], xa_wo[l])
        x = x + peer_ffn(rmsnorm(x, ffn_norm[l]), peer_wq[l], peer_subkeys[l], peer_u[l], peer_v[l])
    return x
```

```python
import functools

import jax
import jax.numpy as jnp
import numpy as np
from jax import lax
from jax.experimental import pallas as pl
from jax.experimental.pallas import tpu as pltpu

F32 = jnp.float32
BF16 = jnp.bfloat16

EPS = 1e-6
ROPE_THETA = 10000.0
LANES = 128
HEAD_DIM = 64
GLA_HEADS = 4
GLA_DV = 128
GLA_CHUNK = 64
GLA_TAU = 16.0
GLA_RANK = 16
MOBA_BLOCK = 256
MOBA_TOPK = 3
RG_C = 8.0
RG_CONV = 4
XA_HEADS = 4
XA_DH = 128
PEER_HEADS = 8
PEER_NKEYS = 128
PEER_TOPK = 16
NEG = -1e30
VMEM_LIMIT = 56 * 1024 * 1024


def _cparams(sem):
    return pltpu.CompilerParams(dimension_semantics=sem, vmem_limit_bytes=VMEM_LIMIT)


def _dot(a, b):
    return jnp.dot(a.astype(BF16), b.astype(BF16), preferred_element_type=F32)


def _dot_nt(a, b):
    return lax.dot_general(a.astype(BF16), b.astype(BF16), (((1,), (1,)), ((), ())),
                           preferred_element_type=F32)


def _dot_tn(a, b):
    return lax.dot_general(a.astype(BF16), b.astype(BF16), (((0,), (0,)), ((), ())),
                           preferred_element_type=F32)


def _split2(a):
    hi = a.astype(BF16)
    lo = (a - hi.astype(F32)).astype(BF16)
    return hi, lo


def _dot_exact_rhs(a, m_bf16):
    hi, lo = _split2(a)
    return (jnp.dot(hi, m_bf16, preferred_element_type=F32)
            + jnp.dot(lo, m_bf16, preferred_element_type=F32))


def _dot_exact_lhs(m_bf16, a):
    hi, lo = _split2(a)
    return (jnp.dot(m_bf16, hi, preferred_element_type=F32)
            + jnp.dot(m_bf16, lo, preferred_element_type=F32))


def _rms(x, g):
    return x * lax.rsqrt(jnp.mean(x * x, axis=-1, keepdims=True) + EPS) * g


def _softplus(z):
    return jnp.maximum(z, 0.0) + jnp.log1p(jnp.exp(-jnp.abs(z)))


def _head_mask(hh):
    lane = lax.broadcasted_iota(jnp.int32, (1, LANES), 1)
    return ((lane // HEAD_DIM) == hh).astype(F32)


def _norm_proj_kernel(x_ref, g_ref, w_ref, o_ref):
    xn = _rms(x_ref[...], g_ref[...])
    o_ref[...] = jnp.dot(xn.astype(BF16), w_ref[...], preferred_element_type=F32)


def norm_proj(x2d, g, w_bf16, tm=256):
    T, D = x2d.shape
    N = w_bf16.shape[1]
    return pl.pallas_call(
        _norm_proj_kernel,
        grid=(T // tm,),
        in_specs=[pl.BlockSpec((tm, D), lambda i: (i, 0)),
                  pl.BlockSpec((1, D), lambda i: (0, 0)),
                  pl.BlockSpec((D, N), lambda i: (0, 0))],
        out_specs=pl.BlockSpec((tm, N), lambda i: (i, 0)),
        out_shape=jax.ShapeDtypeStruct((T, N), F32),
        compiler_params=_cparams(("parallel",)),
        name="norm_proj",
    )(x2d, g.reshape(1, D), w_bf16)


def _out_proj_kernel(x_ref, a_ref, b_ref, w_ref, o_ref):
    ka = a_ref.shape[1]
    o_ref[...] = (x_ref[...] + _dot(a_ref[...], w_ref[0:ka, :])
                  + _dot(b_ref[...], w_ref[ka:, :]))


def out_proj(x2d, a2d, b2d, w_bf16, tm=512):
    T, D = x2d.shape
    ka, kb = a2d.shape[1], b2d.shape[1]
    return pl.pallas_call(
        _out_proj_kernel,
        grid=(T // tm,),
        in_specs=[pl.BlockSpec((tm, D), lambda i: (i, 0)),
                  pl.BlockSpec((tm, ka), lambda i: (i, 0)),
                  pl.BlockSpec((tm, kb), lambda i: (i, 0)),
                  pl.BlockSpec((ka + kb, D), lambda i: (0, 0))],
        out_specs=pl.BlockSpec((tm, D), lambda i: (i, 0)),
        out_shape=jax.ShapeDtypeStruct((T, D), F32),
        compiler_params=_cparams(("parallel",)),
        name="out_proj",
    )(x2d, a2d, b2d, w_bf16)


EV_Q, EV_K, EV_V, EV_R, EV_SQ, EV_SK, EV_SV, EV_GG = 0, 2, 4, 8, 12, 16, 20, 24
EV_COLS = 25 * LANES


def _gla_kernel(q_ref, k_ref, v_ref, r_ref, gg_ref, gup_ref, gb_ref, onorm_ref, o_ref, state_ref,
                *, ts):
    C = GLA_CHUNK

    @pl.when(pl.program_id(1) == 0)
    def _():
        state_ref[...] = jnp.zeros_like(state_ref)

    row = lax.broadcasted_iota(jnp.int32, (C, C), 0)
    col = lax.broadcasted_iota(jnp.int32, (C, C), 1)
    causal = row >= col
    tri = causal.astype(BF16)
    masks = [_head_mask(0), _head_mask(1)]
    scale = HEAD_DIM ** -0.5
    onorm = onorm_ref[...]

    for ci in range(ts // C):
        sl = slice(ci * C, (ci + 1) * C)
        pre = _dot(gg_ref[0, sl, :], gup_ref[...]) + gb_ref[...]
        g = (jnp.minimum(pre, 0.0) - jnp.log1p(jnp.exp(-jnp.abs(pre)))) * (1.0 / GLA_TAU)
        b = _dot_exact_lhs(tri, g)
        bmid = b[C // 2 - 1:C // 2, :]
        blast = b[C - 1:C, :]
        q = q_ref[0, sl, :] * scale
        k = k_ref[0, sl, :]
        qd = q * jnp.exp(b - bmid)
        kd = k * jnp.exp(bmid - b)
        qe = q * jnp.exp(b)
        kdec = k * jnp.exp(blast - b)
        eb_last = jnp.exp(blast)
        for p in range(GLA_HEADS // 2):
            lanes = slice(p * LANES, (p + 1) * LANES)
            st = state_ref[p]
            new_st = st * eb_last[:, lanes]
            for hh in range(2):
                h = 2 * p + hh
                m = masks[hh]
                hl = slice(h * GLA_DV, (h + 1) * GLA_DV)
                a = _dot_nt(qd[:, lanes] * m, kd[:, lanes])
                a = jnp.where(causal, a, 0.0)
                v_h = v_ref[0, sl, hl]
                o = _dot(a, v_h) + _dot_nt(qe[:, lanes] * m, st)
                o = _rms(o, onorm)
                r_h = r_ref[0, sl, hl]
                o_ref[0, sl, hl] = o * (r_h * jax.nn.sigmoid(r_h))
                new_st = new_st + _dot_tn(v_h, kdec[:, lanes] * m)
            state_ref[p] = new_st


def gla_mixer(proj3, gate_up_pad, gate_b, out_norm, ts=256):
    B, S, _ = proj3.shape
    nh = GLA_HEADS
    w = nh * GLA_DV

    def col(blk_w, tile):
        idx = tile * LANES // blk_w
        return pl.BlockSpec((1, ts, blk_w), lambda b, c: (b, c, idx))

    return pl.pallas_call(
        functools.partial(_gla_kernel, ts=ts),
        grid=(B, S // ts),
        in_specs=[col(nh * HEAD_DIM, EV_Q), col(nh * HEAD_DIM, EV_K), col(w, EV_V), col(w, EV_R),
                  col(LANES, EV_GG),
                  pl.BlockSpec((LANES, nh * HEAD_DIM), lambda b, c: (0, 0)),
                  pl.BlockSpec((1, nh * HEAD_DIM), lambda b, c: (0, 0)),
                  pl.BlockSpec((1, GLA_DV), lambda b, c: (0, 0))],
        out_specs=pl.BlockSpec((1, ts, w), lambda b, c: (b, c, 0)),
        out_shape=jax.ShapeDtypeStruct((B, S, w), F32),
        scratch_shapes=[pltpu.VMEM((nh // 2, GLA_DV, LANES), F32)],
        compiler_params=_cparams(("parallel", "arbitrary")),
        name="gla",
    )(proj3, proj3, proj3, proj3, proj3, gate_up_pad, gate_b.reshape(1, -1), out_norm.reshape(1, -1))


def _sb_kernel(q_ref, k_ref, v_ref, o_ref, acc_ref, carry_ref, *, tq):
    i = pl.program_id(2)
    q = q_ref[0] * (HEAD_DIM ** -0.5)
    masks = [_head_mask(0), _head_mask(1)]
    qh = [(q * m).astype(BF16) for m in masks]
    row = lax.broadcasted_iota(jnp.int32, (tq, tq), 0)
    col = lax.broadcasted_iota(jnp.int32, (tq, tq), 1)
    upper = (row > col).astype(BF16)
    past = col < row
    acc_ref[...] = jnp.zeros_like(acc_ref)
    carry_ref[...] = jnp.zeros_like(carry_ref)

    def tile(j, diag):
        off = pl.multiple_of(j * tq, tq)
        k = k_ref[0, pl.ds(off, tq), :].astype(BF16)
        v = v_ref[0, pl.ds(off, tq), :].astype(BF16)
        for hh in range(2):
            z = lax.dot_general(qh[hh], k, (((1,), (1,)), ((), ())), preferred_element_type=F32)
            sp = _softplus(z)
            log_rem = jnp.where(past, -sp, 0.0) if diag else -sp
            carry = carry_ref[hh]
            after = _dot_exact_rhs(log_rem, upper) + carry
            w = jnp.exp((z - sp) + after)
            if diag:
                w = jnp.where(past, w, 0.0)
            acc_ref[hh] += jnp.dot(w.astype(BF16), v, preferred_element_type=F32)
            carry_ref[hh] = carry + jnp.sum(log_rem, axis=1, keepdims=True)

    tile(i, True)

    def body(t, c):
        tile(i - 1 - t, False)
        return c

    lax.fori_loop(0, i, body, 0)
    o_ref[0] = acc_ref[0] * masks[0] + acc_ref[1] * masks[1]


def sb_attention(proj3, tq=256):
    B, S, _ = proj3.shape
    npairs = 4
    return pl.pallas_call(
        functools.partial(_sb_kernel, tq=tq),
        grid=(B, npairs, S // tq),
        in_specs=[pl.BlockSpec((1, tq, LANES), lambda b, p, i: (b, i, EV_SQ + p)),
                  pl.BlockSpec((1, S, LANES), lambda b, p, i: (b, 0, EV_SK + p)),
                  pl.BlockSpec((1, S, LANES), lambda b, p, i: (b, 0, EV_SV + p))],
        out_specs=pl.BlockSpec((1, tq, LANES), lambda b, p, i: (b, i, p)),
        out_shape=jax.ShapeDtypeStruct((B, S, npairs * LANES), F32),
        scratch_shapes=[pltpu.VMEM((2, tq, LANES), F32), pltpu.VMEM((2, tq, 1), F32)],
        compiler_params=_cparams(("parallel", "parallel", "arbitrary")),
        name="sb_attention",
    )(proj3, proj3, proj3)


OD_Q, OD_K, OD_V, OD_RX, OD_RG = 0, 4, 8, 12, 16
OD_W = 4 * LANES


def _moba_prep_kernel(q_ref, k_ref, pos_ref, invf_ref, qn_ref, kn_ref, bd_ref, qo_ref, ko_ref, km_ref):
    ang = pos_ref[0].astype(F32) * invf_ref[...]
    cos1, sin1 = jnp.cos(ang), jnp.sin(ang)
    lane = lax.broadcasted_iota(jnp.int32, (1, LANES), 1)
    first_half = (lane % HEAD_DIM) < (HEAD_DIM // 2)
    bd = bd_ref[...]

    def norm_rope(x, gain):
        ms = _dot_exact_rhs(x * x, bd) * (1.0 / HEAD_DIM)
        xn = x * lax.rsqrt(ms + EPS) * gain
        outs = []
        for t in range(OD_W // LANES):
            xb = xn[:, t * LANES:(t + 1) * LANES]
            up = pltpu.roll(xb, LANES - HEAD_DIM // 2, axis=1)
            dn = pltpu.roll(xb, HEAD_DIM // 2, axis=1)
            outs.append(xb * cos1 + jnp.where(first_half, -up, dn) * sin1)
        return jnp.concatenate(outs, axis=1)

    qo_ref[0] = norm_rope(q_ref[0], qn_ref[...])
    kr = norm_rope(k_ref[0], kn_ref[...])
    ko_ref[0] = kr
    km_ref[0, 0] = jnp.mean(kr, axis=0, keepdims=True)


def moba_prep(proj3, pos3, inv_freq_tile, qn_tile, kn_tile, blockdiag):
    B, S, _ = proj3.shape
    tb = MOBA_BLOCK
    nb = S // tb
    full = lambda shape: pl.BlockSpec(shape, lambda b, i: (0,) * len(shape))
    return pl.pallas_call(
        _moba_prep_kernel,
        grid=(B, nb),
        in_specs=[pl.BlockSpec((1, tb, OD_W), lambda b, i: (b, i, OD_Q // 4)),
                  pl.BlockSpec((1, tb, OD_W), lambda b, i: (b, i, OD_K // 4)),
                  pl.BlockSpec((1, tb, 1), lambda b, i: (b, i, 0)),
                  full((1, LANES)), full((1, OD_W)), full((1, OD_W)), full((OD_W, OD_W))],
        out_specs=[pl.BlockSpec((1, tb, OD_W), lambda b, i: (b, i, 0)),
                   pl.BlockSpec((1, tb, OD_W), lambda b, i: (b, i, 0)),
                   pl.BlockSpec((1, 1, 1, OD_W), lambda b, i: (b, i, 0, 0))],
        out_shape=[jax.ShapeDtypeStruct((B, S, OD_W), F32),
                   jax.ShapeDtypeStruct((B, S, OD_W), F32),
                   jax.ShapeDtypeStruct((B, nb, 1, OD_W), F32)],
        compiler_params=_cparams(("parallel", "parallel")),
        name="moba_prep",
    )(proj3, proj3, pos3, inv_freq_tile, qn_tile, kn_tile, blockdiag)


def _moba_kernel(q_ref, k_ref, v_ref, km_ref, o_ref, acc_ref, m_ref, l_ref, sel_ref):
    tb = MOBA_BLOCK
    nbp = km_ref.shape[1]
    i = pl.program_id(2)
    q = q_ref[0]
    km = km_ref[0]
    masks = [_head_mask(0), _head_mask(1)]
    scale = HEAD_DIM ** -0.5
    blk = lax.broadcasted_iota(jnp.int32, (tb, nbp), 1).astype(F32)
    own = i.astype(F32)
    row = lax.broadcasted_iota(jnp.int32, (tb, tb), 0)
    col = lax.broadcasted_iota(jnp.int32, (tb, tb), 1)
    visible = col <= row
    off_own = pl.multiple_of(i * tb, tb)
    k_own = k_ref[0, pl.ds(off_own, tb), :].astype(BF16)
    v_own = v_ref[0, pl.ds(off_own, tb), :].astype(BF16)
    qs = []
    for hh in range(2):
        qm = q * masks[hh]
        gate = _dot_nt(qm, km)
        g = jnp.where(blk < own, gate, -jnp.inf)
        sel = jnp.zeros((tb, nbp), F32)
        for r in range(MOBA_TOPK):
            mx = jnp.max(g, axis=1, keepdims=True)
            idx = jnp.min(jnp.where(g == mx, blk, float(nbp)), axis=1, keepdims=True)
            hit = blk == idx
            keep = jnp.where(i > r, 1.0, 0.0)
            sel = sel + jnp.where(hit, keep, 0.0)
            g = jnp.where(hit, -jnp.inf, g)
        sel_ref[hh] = sel
        qb = (qm * scale).astype(BF16)
        qs.append(qb)
        s = lax.dot_general(qb, k_own, (((1,), (1,)), ((), ())), preferred_element_type=F32)
        s = jnp.where(visible, s, NEG)
        m0 = jnp.max(s, axis=1, keepdims=True)
        p = jnp.exp(s - m0)
        m_ref[hh] = m0
        l_ref[hh] = jnp.sum(p, axis=1, keepdims=True)
        acc_ref[hh] = jnp.dot(p.astype(BF16), v_own, preferred_element_type=F32)

    def body(n, c):
        off = pl.multiple_of(n * tb, tb)
        k = k_ref[0, pl.ds(off, tb), :].astype(BF16)
        v = v_ref[0, pl.ds(off, tb), :].astype(BF16)
        for hh in range(2):
            chosen = jnp.sum(jnp.where(blk == n.astype(F32), sel_ref[hh], 0.0), axis=1, keepdims=True)
            s = lax.dot_general(qs[hh], k, (((1,), (1,)), ((), ())), preferred_element_type=F32)
            s = jnp.where(chosen > 0.0, s, NEG)
            m_old = m_ref[hh]
            m_new = jnp.maximum(m_old, jnp.max(s, axis=1, keepdims=True))
            alpha = jnp.exp(m_old - m_new)
            p = jnp.exp(s - m_new)
            l_ref[hh] = l_ref[hh] * alpha + jnp.sum(p, axis=1, keepdims=True)
            acc_ref[hh] = acc_ref[hh] * alpha + jnp.dot(p.astype(BF16), v, preferred_element_type=F32)
            m_ref[hh] = m_new
        return c

    lax.fori_loop(0, i, body, 0)
    o_ref[0] = (acc_ref[0] / l_ref[0]) * masks[0] + (acc_ref[1] / l_ref[1]) * masks[1]


def moba_attention(q_rot, k_rot, proj3, kmean):
    B, S, _ = q_rot.shape
    tb = MOBA_BLOCK
    nb = S // tb
    npairs = OD_W // LANES
    nbp = -(-nb // LANES) * LANES
    kmean = jnp.pad(kmean, ((0, 0), (0, nbp - nb), (0, 0)))
    return pl.pallas_call(
        _moba_kernel,
        grid=(B, npairs, nb),
        in_specs=[pl.BlockSpec((1, tb, LANES), lambda b, p, i: (b, i, p)),
                  pl.BlockSpec((1, S, LANES), lambda b, p, i: (b, 0, p)),
                  pl.BlockSpec((1, S, LANES), lambda b, p, i: (b, 0, OD_V + p)),
                  pl.BlockSpec((1, nbp, LANES), lambda b, p, i: (b, 0, p))],
        out_specs=pl.BlockSpec((1, tb, LANES), lambda b, p, i: (b, i, p)),
        out_shape=jax.ShapeDtypeStruct((B, S, OD_W), F32),
        scratch_shapes=[pltpu.VMEM((2, tb, LANES), F32), pltpu.VMEM((2, tb, 1), F32),
                        pltpu.VMEM((2, tb, 1), F32), pltpu.VMEM((2, tb, nbp), F32)],
        compiler_params=_cparams(("parallel", "parallel", "arbitrary")),
        name="moba_attention",
    )(q_rot, k_rot, proj3, kmean)


def _rglru_kernel(x_ref, gate_ref, cw_ref, cb_ref, wa_ref, ba_ref, wx_ref, bx_ref, lam_ref, o_ref,
                  buf_ref, xprev_ref, hprev_ref, *, ts):
    @pl.when(pl.program_id(1) == 0)
    def _():
        xprev_ref[...] = jnp.zeros_like(xprev_ref)
        hprev_ref[...] = jnp.zeros_like(hprev_ref)

    x = x_ref[0]
    buf_ref[0:8, :] = xprev_ref[...]
    buf_ref[8:8 + ts, :] = x
    xprev_ref[...] = x[ts - 8:ts, :]
    xc = cb_ref[...]
    for kk in range(RG_CONV):
        start = 8 - (RG_CONV - 1) + kk
        xc = xc + cw_ref[kk:kk + 1, :] * buf_ref[start:start + ts, :]
    r = jax.nn.sigmoid(_dot(xc, wa_ref[...]) + ba_ref[...])
    ig = jax.nn.sigmoid(_dot(xc, wx_ref[...]) + bx_ref[...])
    log_a = (-RG_C) * r * _softplus(-lam_ref[...])
    a = jnp.exp(log_a)
    u = jnp.sqrt(-jnp.tanh(log_a) * (a * a + 1.0)) * (ig * xc)
    t_idx = lax.broadcasted_iota(jnp.int32, (ts, 1), 0)
    d = 1
    while d < ts:
        valid = t_idx >= d
        a_sh = pltpu.roll(a, d, axis=0)
        u_sh = pltpu.roll(u, d, axis=0)
        u = jnp.where(valid, a * u_sh, 0.0) + u
        a = jnp.where(valid, a * a_sh, a)
        d *= 2
    h = a * hprev_ref[...] + u
    hprev_ref[...] = h[ts - 1:ts, :]
    o_ref[0] = h * jax.nn.gelu(gate_ref[0], approximate=True)


def rg_lru_mixer(proj3, conv_w, conv_b, wa_bd, ba, wx_bd, bx, lam, ts=256):
    B, S, _ = proj3.shape
    W = conv_w.shape[1]
    full = lambda shape: pl.BlockSpec(shape, lambda b, t: (0,) * len(shape))
    row = lambda v: v.reshape(1, W)
    return pl.pallas_call(
        functools.partial(_rglru_kernel, ts=ts),
        grid=(B, S // ts),
        in_specs=[pl.BlockSpec((1, ts, W), lambda b, t: (b, t, OD_RX // 4)),
                  pl.BlockSpec((1, ts, W), lambda b, t: (b, t, OD_RG // 4)),
                  full((RG_CONV, W)), full((1, W)), full((W, W)), full((1, W)), full((W, W)),
                  full((1, W)), full((1, W))],
        out_specs=pl.BlockSpec((1, ts, W), lambda b, t: (b, t, 0)),
        out_shape=jax.ShapeDtypeStruct((B, S, W), F32),
        scratch_shapes=[pltpu.VMEM((ts + 8, W), F32), pltpu.VMEM((8, W), F32), pltpu.VMEM((1, W), F32)],
        compiler_params=_cparams(("parallel", "arbitrary")),
        name="rg_lru",
    )(proj3, proj3, conv_w, row(conv_b), wa_bd, row(ba), wx_bd, row(bx), row(lam))


def _xattn_kernel(x_ref, kv_ref, g_ref, wq_ref, qn_ref, kn_ref, wo_ref, o_ref):
    x = x_ref[0]
    q = _dot(_rms(x, g_ref[...]), wq_ref[...])
    kv = kv_ref[0]
    hw = XA_HEADS * XA_DH
    outs = []
    for h in range(XA_HEADS):
        hl = slice(h * XA_DH, (h + 1) * XA_DH)
        qh = _rms(q[:, hl], qn_ref[...])
        kh = _rms(kv[:, hl], kn_ref[...])
        vh = kv[:, hw + h * XA_DH: hw + (h + 1) * XA_DH]
        s = _dot_nt(qh, kh) * (XA_DH ** -0.5)
        s = s - jnp.max(s, axis=1, keepdims=True)
        p = jnp.exp(s)
        p = p / jnp.sum(p, axis=1, keepdims=True)
        outs.append(_dot(p, vh))
    o_ref[0] = x + _dot(jnp.concatenate(outs, axis=1), wo_ref[...])


def cross_attention(x3, kv3, g, wq_bf16, qn, kn, wo_bf16, tq=256):
    B, S, D = x3.shape
    M = kv3.shape[1]
    hw = XA_HEADS * XA_DH
    full = lambda shape: pl.BlockSpec(shape, lambda b, i: (0,) * len(shape))
    return pl.pallas_call(
        _xattn_kernel,
        grid=(B, S // tq),
        in_specs=[pl.BlockSpec((1, tq, D), lambda b, i: (b, i, 0)),
                  pl.BlockSpec((1, M, 2 * hw), lambda b, i: (b, 0, 0)),
                  full((1, D)), full((D, hw)), full((1, XA_DH)), full((1, XA_DH)), full((hw, D))],
        out_specs=pl.BlockSpec((1, tq, D), lambda b, i: (b, i, 0)),
        out_shape=jax.ShapeDtypeStruct((B, S, D), F32),
        compiler_params=_cparams(("parallel", "parallel")),
        name="cross_attention",
    )(x3, kv3, g.reshape(1, D), wq_bf16, qn.reshape(1, -1), kn.reshape(1, -1), wo_bf16)


def _top_values(s, k):
    rows = []
    for _ in range(k):
        m = jnp.max(s, axis=0, keepdims=True)
        rows.append(m)
        s = jnp.where(s == m, -jnp.inf, s)
    return rows


def _peer_kernel(x_ref, g_ref, wq_ref, sk_ref, u_ref, vt_ref, o_ref,
                 xn_ref, thr_ref, a_ref, e2_ref, s2_ref, p_ref, acc_ref, *, ib):
    e = pl.program_id(1)
    nk = PEER_NKEYS
    K = PEER_TOPK

    @pl.when(e == 0)
    def _():
        xn = _rms(x_ref[...], g_ref[...]).astype(BF16)
        xn_ref[...] = xn
        acc_ref[...] = jnp.zeros_like(acc_ref)

        def route(h, c):
            q = jnp.dot(xn, wq_ref[h], preferred_element_type=F32)
            s1 = _dot_nt(sk_ref[2 * h], q[:, :nk])
            s2 = _dot_nt(sk_ref[2 * h + 1], q[:, nk:])
            v1 = _top_values(s1, K)
            v2 = _top_values(s2, K)
            v2m = jnp.concatenate(v2, axis=0)
            cand = jnp.concatenate([v1[a] + v2m for a in range(K)], axis=0)
            top = _top_values(cand, K)
            tau = top[K - 1]
            smax = top[0]
            z = top[0] * 0.0
            for t in top:
                z = z + jnp.exp(t - smax)
            thr = jnp.full(s1.shape, jnp.inf, F32)
            for bb in range(K):
                thr = jnp.minimum(thr, jnp.where(s1 + v2[bb] >= tau, v2[bb], jnp.inf))
            thr_ref[h] = thr
            a_ref[h] = jnp.exp(s1 - v1[0]) / z
            e2_ref[h] = jnp.exp(s2 - v2[0])
            s2_ref[h] = s2
            return c

        lax.fori_loop(0, PEER_HEADS, route, 0)

    xn = xn_ref[...]
    for ii in range(ib):
        i = e * ib + ii
        hid = _dot_nt(u_ref[ii * nk:(ii + 1) * nk, :], xn)
        act = 0.5 * hid * (1.0 + lax.erf(hid * np.float32(np.sqrt(0.5))))
        w = jnp.zeros_like(hid)
        for h in range(PEER_HEADS):
            thr_row = thr_ref[h, pl.ds(i, 1), :]
            a_row = a_ref[h, pl.ds(i, 1), :]
            w = w + jnp.where(s2_ref[h] >= thr_row, e2_ref[h] * a_row, 0.0)
        p_ref[ii * nk:(ii + 1) * nk, :] = (w * act).astype(BF16)
    acc_ref[...] += jnp.dot(vt_ref[...], p_ref[...], preferred_element_type=F32)

    @pl.when(e == pl.num_programs(1) - 1)
    def _():
        o_ref[...] = x_ref[...] + acc_ref[...].T


def peer_ffn(x2d, g, wq3_bf16, sk_bf16, u_bf16, vt_bf16, tq=256, ib=4):
    T, D = x2d.shape
    ne = PEER_NKEYS // ib
    ec = ib * PEER_NKEYS
    H = PEER_HEADS
    full = lambda shape: pl.BlockSpec(shape, lambda i, e: (0,) * len(shape))
    route_scratch = pltpu.VMEM((H, PEER_NKEYS, tq), F32)
    return pl.pallas_call(
        functools.partial(_peer_kernel, ib=ib),
        grid=(T // tq, ne),
        in_specs=[pl.BlockSpec((tq, D), lambda i, e: (i, 0)),
                  full((1, D)), full((H, D, 2 * PEER_NKEYS)), full((2 * H, PEER_NKEYS, PEER_NKEYS)),
                  pl.BlockSpec((ec, D), lambda i, e: (e, 0)),
                  pl.BlockSpec((D, ec), lambda i, e: (0, e))],
        out_specs=pl.BlockSpec((tq, D), lambda i, e: (i, 0)),
        out_shape=jax.ShapeDtypeStruct((T, D), F32),
        scratch_shapes=[pltpu.VMEM((tq, D), BF16), route_scratch, route_scratch, route_scratch,
                        route_scratch, pltpu.VMEM((ec, tq), BF16), pltpu.VMEM((D, tq), F32)],
        compiler_params=_cparams(("parallel", "arbitrary")),
        name="peer_ffn",
    )(x2d, g.reshape(1, D), wq3_bf16, sk_bf16, u_bf16, vt_bf16)


def _block_diag(w):
    G, n, _ = w.shape
    eye = jnp.eye(G, dtype=w.dtype)
    return (eye[:, None, :, None] * w[:, :, None, :]).reshape(G * n, G * n)


def even_layer(x2d, B, S, norm, w_in, gate_up, gate_b, out_norm, w_out):
    D = x2d.shape[1]
    gg0 = EV_GG * LANES
    wp = jnp.concatenate([w_in[:, :1536], w_in[:, 1536 + GLA_RANK:], w_in[:, 1536:1536 + GLA_RANK],
                          jnp.zeros((D, EV_COLS - gg0 - GLA_RANK), w_in.dtype)], axis=1).astype(BF16)
    proj = norm_proj(x2d, norm, wp).reshape(B, S, EV_COLS)
    gup = jnp.concatenate([gate_up, jnp.zeros((LANES - GLA_RANK, gate_up.shape[1]), gate_up.dtype)], axis=0)
    a_out = gla_mixer(proj, gup.astype(BF16), gate_b, out_norm)
    b_out = sb_attention(proj)
    return out_proj(x2d, a_out.reshape(B * S, -1), b_out.reshape(B * S, -1), w_out.astype(BF16))


def odd_layer(x2d, B, S, positions, norm, w_in, q_norm, k_norm, conv_w, conv_b, wa, ba, wx, bx, lam, w_out):
    proj = norm_proj(x2d, norm, w_in.astype(BF16)).reshape(B, S, -1)
    half = HEAD_DIM // 2
    inv_freq = ROPE_THETA ** (-jnp.arange(0, HEAD_DIM, 2, dtype=F32) / HEAD_DIM)
    inv_tile = jnp.tile(inv_freq, LANES // half).reshape(1, LANES)
    seg = np.arange(OD_W) // HEAD_DIM
    blockdiag = jnp.asarray(seg[:, None] == seg[None, :], dtype=BF16)
    q_rot, k_rot, kmean = moba_prep(proj, positions.reshape(B, S, 1), inv_tile,
                                    jnp.tile(q_norm, OD_W // HEAD_DIM).reshape(1, OD_W),
                                    jnp.tile(k_norm, OD_W // HEAD_DIM).reshape(1, OD_W), blockdiag)
    c_out = moba_attention(q_rot, k_rot, proj, kmean.reshape(B, S // MOBA_BLOCK, OD_W))
    d_out = rg_lru_mixer(proj, conv_w, conv_b, _block_diag(wa).astype(BF16), ba,
                         _block_diag(wx).astype(BF16), bx, lam)
    return out_proj(x2d, c_out.reshape(B * S, -1), d_out.reshape(B * S, -1), w_out.astype(BF16))


def kernel(x, mem, positions, ev_norm, ev_w_in, ev_gla_gate_up, ev_gla_gate_b, ev_gla_out_norm, ev_w_out, od_norm, od_w_in, od_q_norm, od_k_norm, od_conv_w, od_conv_b, od_gate_a_w, od_gate_a_b, od_gate_x_w, od_gate_x_b, od_lambda, od_w_out, xa_norm, xa_mem_norm, xa_wq, xa_wkv, xa_q_norm, xa_k_norm, xa_wo, ffn_norm, peer_wq, peer_subkeys, peer_u, peer_v):
    B, S, D = x.shape
    M = mem.shape[1]
    depth = xa_norm.shape[0]
    x2d = x.reshape(B * S, D)
    mem2d = mem.reshape(B * M, D)
    for l in range(depth):
        if l % 2 == 0:
            e = l // 2
            x2d = even_layer(x2d, B, S, ev_norm[e], ev_w_in[e], ev_gla_gate_up[e], ev_gla_gate_b[e],
                             ev_gla_out_norm[e], ev_w_out[e])
        else:
            o = l // 2
            x2d = odd_layer(x2d, B, S, positions, od_norm[o], od_w_in[o], od_q_norm[o], od_k_norm[o],
                            od_conv_w[o], od_conv_b[o], od_gate_a_w[o], od_gate_a_b[o],
                            od_gate_x_w[o], od_gate_x_b[o], od_lambda[o], od_w_out[o])
        kv = norm_proj(mem2d, xa_mem_norm[l], xa_wkv[l].astype(BF16))
        x2d = cross_attention(x2d.reshape(B, S, D), kv.reshape(B, M, -1), xa_norm[l],
                              xa_wq[l].astype(BF16), xa_q_norm[l], xa_k_norm[l],
                              xa_wo[l].astype(BF16)).reshape(B * S, D)
        H = PEER_HEADS
        wq3 = peer_wq[l].reshape(D, H, 2 * PEER_NKEYS).transpose(1, 0, 2).astype(BF16)
        sk = peer_subkeys[l].reshape(2 * H, PEER_NKEYS, -1).astype(BF16)
        x2d = peer_ffn(x2d, ffn_norm[l], wq3, sk, peer_u[l].astype(BF16), peer_v[l].T.astype(BF16))
    return x2d.reshape(B, S, D)
```

```python
import functools

import jax
import jax.numpy as jnp
import numpy as np
from jax import lax
from jax.experimental import pallas as pl
from jax.experimental.pallas import tpu as pltpu

F32 = jnp.float32
BF16 = jnp.bfloat16

EPS = 1e-6
ROPE_THETA = 10000.0
LANES = 128
HEAD_DIM = 64
GLA_HEADS = 4
GLA_DV = 128
GLA_CHUNK = 64
GLA_TAU = 16.0
GLA_RANK = 16
MOBA_BLOCK = 256
MOBA_TOPK = 3
RG_C = 8.0
RG_CONV = 4
XA_HEADS = 4
XA_DH = 128
PEER_HEADS = 8
PEER_NKEYS = 128
PEER_TOPK = 16
NEG = -1e30
F32_LOG_TINY = -104.0
VMEM_LIMIT = 56 * 1024 * 1024


def _cparams(sem):
    return pltpu.CompilerParams(dimension_semantics=sem, vmem_limit_bytes=VMEM_LIMIT)


def _dot(a, b):
    return jnp.dot(a.astype(BF16), b.astype(BF16), preferred_element_type=F32)


def _dot_nt(a, b):
    return lax.dot_general(a.astype(BF16), b.astype(BF16), (((1,), (1,)), ((), ())),
                           preferred_element_type=F32)


def _dot_tn(a, b):
    return lax.dot_general(a.astype(BF16), b.astype(BF16), (((0,), (0,)), ((), ())),
                           preferred_element_type=F32)


def _split2(a):
    hi = a.astype(BF16)
    lo = (a - hi.astype(F32)).astype(BF16)
    return hi, lo


def _dot_exact_rhs(a, m_bf16):
    hi, lo = _split2(a)
    return (jnp.dot(hi, m_bf16, preferred_element_type=F32)
            + jnp.dot(lo, m_bf16, preferred_element_type=F32))


def _dot_exact_lhs(m_bf16, a):
    hi, lo = _split2(a)
    return (jnp.dot(m_bf16, hi, preferred_element_type=F32)
            + jnp.dot(m_bf16, lo, preferred_element_type=F32))


def _rms(x, g):
    return x * lax.rsqrt(jnp.mean(x * x, axis=-1, keepdims=True) + EPS) * g


def _softplus(z):
    return jnp.maximum(z, 0.0) + jnp.log1p(jnp.exp(-jnp.abs(z)))


def _head_mask(hh):
    lane = lax.broadcasted_iota(jnp.int32, (1, LANES), 1)
    return ((lane // HEAD_DIM) == hh).astype(F32)


def _norm_proj_kernel(x_ref, g_ref, w_ref, o_ref):
    xn = _rms(x_ref[...], g_ref[...])
    o_ref[...] = jnp.dot(xn.astype(BF16), w_ref[...], preferred_element_type=F32)


def norm_proj(x2d, g, w_bf16, tm=256):
    T, D = x2d.shape
    N = w_bf16.shape[1]
    return pl.pallas_call(
        _norm_proj_kernel,
        grid=(T // tm,),
        in_specs=[pl.BlockSpec((tm, D), lambda i: (i, 0)),
                  pl.BlockSpec((1, D), lambda i: (0, 0)),
                  pl.BlockSpec((D, N), lambda i: (0, 0))],
        out_specs=pl.BlockSpec((tm, N), lambda i: (i, 0)),
        out_shape=jax.ShapeDtypeStruct((T, N), F32),
        compiler_params=_cparams(("parallel",)),
        name="norm_proj",
    )(x2d, g.reshape(1, D), w_bf16)


def _out_proj_kernel(x_ref, a_ref, b_ref, w_ref, o_ref):
    ka = a_ref.shape[1]
    o_ref[...] = (x_ref[...] + _dot(a_ref[...], w_ref[0:ka, :])
                  + _dot(b_ref[...], w_ref[ka:, :]))


def out_proj(x2d, a2d, b2d, w_bf16, tm=512):
    T, D = x2d.shape
    ka, kb = a2d.shape[1], b2d.shape[1]
    return pl.pallas_call(
        _out_proj_kernel,
        grid=(T // tm,),
        in_specs=[pl.BlockSpec((tm, D), lambda i: (i, 0)),
                  pl.BlockSpec((tm, ka), lambda i: (i, 0)),
                  pl.BlockSpec((tm, kb), lambda i: (i, 0)),
                  pl.BlockSpec((ka + kb, D), lambda i: (0, 0))],
        out_specs=pl.BlockSpec((tm, D), lambda i: (i, 0)),
        out_shape=jax.ShapeDtypeStruct((T, D), F32),
        compiler_params=_cparams(("parallel",)),
        name="out_proj",
    )(x2d, a2d, b2d, w_bf16)


EV_Q, EV_K, EV_V, EV_R, EV_SQ, EV_SK, EV_SV, EV_GG = 0, 2, 4, 8, 12, 16, 20, 24
EV_COLS = 25 * LANES


def _gla_kernel(q_ref, k_ref, v_ref, r_ref, gg_ref, gup_ref, gb_ref, onorm_ref, o_ref, state_ref,
                *, ts):
    C = GLA_CHUNK

    @pl.when(pl.program_id(1) == 0)
    def _():
        state_ref[...] = jnp.zeros_like(state_ref)

    row = lax.broadcasted_iota(jnp.int32, (C, C), 0)
    col = lax.broadcasted_iota(jnp.int32, (C, C), 1)
    causal = row >= col
    tri = causal.astype(BF16)
    masks = [_head_mask(0), _head_mask(1)]
    scale = HEAD_DIM ** -0.5
    onorm = onorm_ref[...]

    for ci in range(ts // C):
        sl = slice(ci * C, (ci + 1) * C)
        pre = _dot(gg_ref[0, sl, :], gup_ref[...]) + gb_ref[...]
        g = (jnp.minimum(pre, 0.0) - jnp.log1p(jnp.exp(-jnp.abs(pre)))) * (1.0 / GLA_TAU)
        b = _dot_exact_lhs(tri, g)
        bmid = b[C // 2 - 1:C // 2, :]
        blast = b[C - 1:C, :]
        q = q_ref[0, sl, :] * scale
        k = k_ref[0, sl, :]
        qd = q * jnp.exp(b - bmid)
        kd = k * jnp.exp(bmid - b)
        qe = q * jnp.exp(b)
        kdec = k * jnp.exp(blast - b)
        eb_last = jnp.exp(blast)
        for p in range(GLA_HEADS // 2):
            lanes = slice(p * LANES, (p + 1) * LANES)
            st = state_ref[p]
            new_st = st * eb_last[:, lanes]
            for hh in range(2):
                h = 2 * p + hh
                m = masks[hh]
                hl = slice(h * GLA_DV, (h + 1) * GLA_DV)
                a = _dot_nt(qd[:, lanes] * m, kd[:, lanes])
                a = jnp.where(causal, a, 0.0)
                v_h = v_ref[0, sl, hl]
                o = _dot(a, v_h) + _dot_nt(qe[:, lanes] * m, st)
                o = _rms(o, onorm)
                r_h = r_ref[0, sl, hl]
                o_ref[0, sl, hl] = o * (r_h * jax.nn.sigmoid(r_h))
                new_st = new_st + _dot_tn(v_h, kdec[:, lanes] * m)
            state_ref[p] = new_st


def gla_mixer(proj3, gate_up_pad, gate_b, out_norm, ts=256):
    B, S, _ = proj3.shape
    nh = GLA_HEADS
    w = nh * GLA_DV

    def col(blk_w, tile):
        idx = tile * LANES // blk_w
        return pl.BlockSpec((1, ts, blk_w), lambda b, c: (b, c, idx))

    return pl.pallas_call(
        functools.partial(_gla_kernel, ts=ts),
        grid=(B, S // ts),
        in_specs=[col(nh * HEAD_DIM, EV_Q), col(nh * HEAD_DIM, EV_K), col(w, EV_V), col(w, EV_R),
                  col(LANES, EV_GG),
                  pl.BlockSpec((LANES, nh * HEAD_DIM), lambda b, c: (0, 0)),
                  pl.BlockSpec((1, nh * HEAD_DIM), lambda b, c: (0, 0)),
                  pl.BlockSpec((1, GLA_DV), lambda b, c: (0, 0))],
        out_specs=pl.BlockSpec((1, ts, w), lambda b, c: (b, c, 0)),
        out_shape=jax.ShapeDtypeStruct((B, S, w), F32),
        scratch_shapes=[pltpu.VMEM((nh // 2, GLA_DV, LANES), F32)],
        compiler_params=_cparams(("parallel", "arbitrary")),
        name="gla",
    )(proj3, proj3, proj3, proj3, proj3, gate_up_pad, gate_b.reshape(1, -1), out_norm.reshape(1, -1))


def _sb_kernel(q_ref, k_ref, v_ref, o_ref, acc_ref, carry_ref, *, tq):
    i = pl.program_id(2)
    q = q_ref[0] * (HEAD_DIM ** -0.5)
    masks = [_head_mask(0), _head_mask(1)]
    qh = [(q * m).astype(BF16) for m in masks]
    row = lax.broadcasted_iota(jnp.int32, (tq, tq), 0)
    col = lax.broadcasted_iota(jnp.int32, (tq, tq), 1)
    upper = (row > col).astype(BF16)
    past = col < row
    acc_ref[...] = jnp.zeros_like(acc_ref)
    carry_ref[...] = jnp.zeros_like(carry_ref)

    def tile(j, diag):
        off = pl.multiple_of(j * tq, tq)
        k = k_ref[0, pl.ds(off, tq), :].astype(BF16)
        v = v_ref[0, pl.ds(off, tq), :].astype(BF16)
        for hh in range(2):
            z = lax.dot_general(qh[hh], k, (((1,), (1,)), ((), ())), preferred_element_type=F32)
            sp = _softplus(z)
            log_rem = jnp.where(past, -sp, 0.0) if diag else -sp
            carry = carry_ref[hh]
            after = _dot_exact_rhs(log_rem, upper) + carry
            w = jnp.exp((z - sp) + after)
            if diag:
                w = jnp.where(past, w, 0.0)
            acc_ref[hh] += jnp.dot(w.astype(BF16), v, preferred_element_type=F32)
            carry_ref[hh] = carry + jnp.sum(log_rem, axis=1, keepdims=True)

    tile(i, True)

    def live(c):
        j, worst = c
        return jnp.logical_and(j >= 0, worst > F32_LOG_TINY)

    def body(c):
        j, _ = c
        tile(j, False)
        return j - 1, jnp.max(carry_ref[...])

    lax.while_loop(live, body, (i - 1, jnp.max(carry_ref[...])))
    o_ref[0] = acc_ref[0] * masks[0] + acc_ref[1] * masks[1]


def sb_attention(proj3, tq=256):
    B, S, _ = proj3.shape
    npairs = 4
    return pl.pallas_call(
        functools.partial(_sb_kernel, tq=tq),
        grid=(B, npairs, S // tq),
        in_specs=[pl.BlockSpec((1, tq, LANES), lambda b, p, i: (b, i, EV_SQ + p)),
                  pl.BlockSpec((1, S, LANES), lambda b, p, i: (b, 0, EV_SK + p)),
                  pl.BlockSpec((1, S, LANES), lambda b, p, i: (b, 0, EV_SV + p))],
        out_specs=pl.BlockSpec((1, tq, LANES), lambda b, p, i: (b, i, p)),
        out_shape=jax.ShapeDtypeStruct((B, S, npairs * LANES), F32),
        scratch_shapes=[pltpu.VMEM((2, tq, LANES), F32), pltpu.VMEM((2, tq, 1), F32)],
        compiler_params=_cparams(("parallel", "parallel", "arbitrary")),
        name="sb_attention",
    )(proj3, proj3, proj3)


OD_Q, OD_K, OD_V, OD_RX, OD_RG = 0, 4, 8, 12, 16
OD_W = 4 * LANES


def _moba_prep_kernel(q_ref, k_ref, v_ref, pos_ref, invf_ref, qn_ref, kn_ref, bd_ref,
                      qt_ref, ko_ref, vt_ref, km_ref):
    ang = pos_ref[0].astype(F32) * invf_ref[...]
    cos1, sin1 = jnp.cos(ang), jnp.sin(ang)
    lane = lax.broadcasted_iota(jnp.int32, (1, LANES), 1)
    first_half = (lane % HEAD_DIM) < (HEAD_DIM // 2)
    bd = bd_ref[...]

    def norm_rope(x, gain):
        ms = _dot_exact_rhs(x * x, bd) * (1.0 / HEAD_DIM)
        xn = x * lax.rsqrt(ms + EPS) * gain
        outs = []
        for t in range(OD_W // LANES):
            xb = xn[:, t * LANES:(t + 1) * LANES]
            up = pltpu.roll(xb, LANES - HEAD_DIM // 2, axis=1)
            dn = pltpu.roll(xb, HEAD_DIM // 2, axis=1)
            outs.append(xb * cos1 + jnp.where(first_half, -up, dn) * sin1)
        return jnp.concatenate(outs, axis=1)

    qt_ref[0] = norm_rope(q_ref[0], qn_ref[...]).T.astype(BF16)
    kr = norm_rope(k_ref[0], kn_ref[...])
    ko_ref[0] = kr.astype(BF16)
    km_ref[0, 0] = jnp.mean(kr, axis=0, keepdims=True)
    vt_ref[0] = v_ref[0].T.astype(BF16)


def moba_prep(proj3, pos3, inv_freq_tile, qn_tile, kn_tile, blockdiag):
    B, S, _ = proj3.shape
    tb = MOBA_BLOCK
    nb = S // tb
    full = lambda shape: pl.BlockSpec(shape, lambda b, i: (0,) * len(shape))
    return pl.pallas_call(
        _moba_prep_kernel,
        grid=(B, nb),
        in_specs=[pl.BlockSpec((1, tb, OD_W), lambda b, i: (b, i, OD_Q // 4)),
                  pl.BlockSpec((1, tb, OD_W), lambda b, i: (b, i, OD_K // 4)),
                  pl.BlockSpec((1, tb, OD_W), lambda b, i: (b, i, OD_V // 4)),
                  pl.BlockSpec((1, tb, 1), lambda b, i: (b, i, 0)),
                  full((1, LANES)), full((1, OD_W)), full((1, OD_W)), full((OD_W, OD_W))],
        out_specs=[pl.BlockSpec((1, OD_W, tb), lambda b, i: (b, 0, i)),
                   pl.BlockSpec((1, tb, OD_W), lambda b, i: (b, i, 0)),
                   pl.BlockSpec((1, OD_W, tb), lambda b, i: (b, 0, i)),
                   pl.BlockSpec((1, 1, 1, OD_W), lambda b, i: (b, i, 0, 0))],
        out_shape=[jax.ShapeDtypeStruct((B, OD_W, S), BF16),
                   jax.ShapeDtypeStruct((B, S, OD_W), BF16),
                   jax.ShapeDtypeStruct((B, OD_W, S), BF16),
                   jax.ShapeDtypeStruct((B, nb, 1, OD_W), F32)],
        compiler_params=_cparams(("parallel", "parallel")),
        name="moba_prep",
    )(proj3, proj3, proj3, pos3, inv_freq_tile, qn_tile, kn_tile, blockdiag)


def _moba_kernel(qt_ref, k_ref, vt_ref, km_ref, o_ref, acc_ref, m_ref, sel_ref):
    tb = MOBA_BLOCK
    nbp = km_ref.shape[1]
    i = pl.program_id(2)
    qt = qt_ref[0]
    km = km_ref[0].astype(BF16)
    chan = lax.broadcasted_iota(jnp.int32, (LANES, 1), 0)
    in_head = [(chan // HEAD_DIM) == hh for hh in range(2)]
    blk = lax.broadcasted_iota(jnp.int32, (nbp, tb), 0).astype(F32)
    own = i.astype(F32)
    key = lax.broadcasted_iota(jnp.int32, (tb, tb), 0)
    qry = lax.broadcasted_iota(jnp.int32, (tb, tb), 1)
    visible = key <= qry
    off_own = pl.multiple_of(i * tb, tb)
    k_own = k_ref[0, pl.ds(off_own, tb), :]
    vt_own = vt_ref[0, :, pl.ds(off_own, tb)]
    zero = jnp.zeros_like(qt)
    one = jnp.ones_like(qt)
    qs = []
    for hh in range(2):
        qm = jnp.where(in_head[hh], qt, zero)
        gate = jnp.dot(km, qm, preferred_element_type=F32)
        g = jnp.where(blk < own, gate, -jnp.inf)
        sel = jnp.zeros((nbp, tb), F32)
        for r in range(MOBA_TOPK):
            mx = jnp.max(g, axis=0, keepdims=True)
            idx = jnp.min(jnp.where(g == mx, blk, float(nbp)), axis=0, keepdims=True)
            hit = blk == idx
            keep = jnp.where(i > r, 1.0, 0.0)
            sel = sel + jnp.where(hit, keep, 0.0)
            g = jnp.where(hit, -jnp.inf, g)
        sel_ref[hh] = sel
        qb = qm * (HEAD_DIM ** -0.5)
        qs.append(qb)
        s = jnp.dot(k_own, qb, preferred_element_type=F32)
        s = jnp.where(visible, s, NEG)
        m0 = jnp.max(s, axis=0, keepdims=True)
        p = jnp.exp(s - m0)
        m_ref[hh] = m0
        acc_ref[hh] = jnp.dot(jnp.where(in_head[hh], vt_own, one), p.astype(BF16),
                              preferred_element_type=F32)

    def past_blocks(n, nblk):
        off = pl.multiple_of(n * tb, tb)
        k = k_ref[0, pl.ds(off, nblk * tb), :]
        vt = vt_ref[0, :, pl.ds(off, nblk * tb)]
        scores = [jnp.dot(k, qs[hh], preferred_element_type=F32) for hh in range(2)]
        probs, alphas = [], []
        for hh in range(2):
            parts = []
            for j in range(nblk):
                chosen = sel_ref[hh, pl.ds(n + j, 1), :]
                parts.append(jnp.where(chosen > 0.0, scores[hh][j * tb:(j + 1) * tb, :], NEG))
            m_old = m_ref[hh]
            m_new = m_old
            for part in parts:
                m_new = jnp.maximum(m_new, jnp.max(part, axis=0, keepdims=True))
            alphas.append(jnp.exp(m_old - m_new))
            probs.append(jnp.concatenate([jnp.exp(part - m_new).astype(BF16) for part in parts], axis=0))
            m_ref[hh] = m_new
        ones_v = jnp.ones_like(vt)
        pv = [jnp.dot(jnp.where(in_head[hh], vt, ones_v), probs[hh], preferred_element_type=F32)
              for hh in range(2)]
        for hh in range(2):
            acc_ref[hh] = acc_ref[hh] * alphas[hh] + pv[hh]

    def two_blocks(t, c):
        past_blocks(2 * t, 2)
        return c

    lax.fori_loop(0, i // 2, two_blocks, 0)

    @pl.when(i % 2 == 1)
    def _():
        past_blocks(i - 1, 1)

    a0, a1 = acc_ref[0], acc_ref[1]
    den0 = a0[HEAD_DIM:HEAD_DIM + 1, :]
    den1 = a1[0:1, :]
    o_ref[0] = jnp.where(in_head[0], a0 / den0, a1 / den1).T


def moba_attention(q_t, k_rot, v_t, kmean):
    B, S, _ = k_rot.shape
    tb = MOBA_BLOCK
    nb = S // tb
    npairs = OD_W // LANES
    nbp = -(-nb // LANES) * LANES
    kmean = jnp.pad(kmean, ((0, 0), (0, nbp - nb), (0, 0)))
    return pl.pallas_call(
        _moba_kernel,
        grid=(B, npairs, nb),
        in_specs=[pl.BlockSpec((1, LANES, tb), lambda b, p, i: (b, p, i)),
                  pl.BlockSpec((1, S, LANES), lambda b, p, i: (b, 0, p)),
                  pl.BlockSpec((1, LANES, S), lambda b, p, i: (b, p, 0)),
                  pl.BlockSpec((1, nbp, LANES), lambda b, p, i: (b, 0, p))],
        out_specs=pl.BlockSpec((1, tb, LANES), lambda b, p, i: (b, i, p)),
        out_shape=jax.ShapeDtypeStruct((B, S, OD_W), F32),
        scratch_shapes=[pltpu.VMEM((2, LANES, tb), F32), pltpu.VMEM((2, 1, tb), F32),
                        pltpu.VMEM((2, nbp, tb), F32)],
        compiler_params=_cparams(("parallel", "parallel", "arbitrary")),
        name="moba_attention",
    )(q_t, k_rot, v_t, kmean)


def _rglru_kernel(x_ref, gate_ref, cw_ref, cb_ref, wa_ref, ba_ref, wx_ref, bx_ref, lam_ref, o_ref,
                  buf_ref, xprev_ref, hprev_ref, *, ts):
    @pl.when(pl.program_id(1) == 0)
    def _():
        xprev_ref[...] = jnp.zeros_like(xprev_ref)
        hprev_ref[...] = jnp.zeros_like(hprev_ref)

    x = x_ref[0]
    buf_ref[0:8, :] = xprev_ref[...]
    buf_ref[8:8 + ts, :] = x
    xprev_ref[...] = x[ts - 8:ts, :]
    xc = cb_ref[...]
    for kk in range(RG_CONV):
        start = 8 - (RG_CONV - 1) + kk
        xc = xc + cw_ref[kk:kk + 1, :] * buf_ref[start:start + ts, :]
    r = jax.nn.sigmoid(_dot(xc, wa_ref[...]) + ba_ref[...])
    ig = jax.nn.sigmoid(_dot(xc, wx_ref[...]) + bx_ref[...])
    log_a = (-RG_C) * r * _softplus(-lam_ref[...])
    a = jnp.exp(log_a)
    u = jnp.sqrt(-jnp.tanh(log_a) * (a * a + 1.0)) * (ig * xc)
    t_idx = lax.broadcasted_iota(jnp.int32, (ts, 1), 0)
    d = 1
    while d < ts:
        valid = t_idx >= d
        a_sh = pltpu.roll(a, d, axis=0)
        u_sh = pltpu.roll(u, d, axis=0)
        u = jnp.where(valid, a * u_sh, 0.0) + u
        a = jnp.where(valid, a * a_sh, a)
        d *= 2
    h = a * hprev_ref[...] + u
    hprev_ref[...] = h[ts - 1:ts, :]
    o_ref[0] = h * jax.nn.gelu(gate_ref[0], approximate=True)


def rg_lru_mixer(proj3, conv_w, conv_b, wa_bd, ba, wx_bd, bx, lam, ts=256):
    B, S, _ = proj3.shape
    W = conv_w.shape[1]
    full = lambda shape: pl.BlockSpec(shape, lambda b, t: (0,) * len(shape))
    row = lambda v: v.reshape(1, W)
    return pl.pallas_call(
        functools.partial(_rglru_kernel, ts=ts),
        grid=(B, S // ts),
        in_specs=[pl.BlockSpec((1, ts, W), lambda b, t: (b, t, OD_RX // 4)),
                  pl.BlockSpec((1, ts, W), lambda b, t: (b, t, OD_RG // 4)),
                  full((RG_CONV, W)), full((1, W)), full((W, W)), full((1, W)), full((W, W)),
                  full((1, W)), full((1, W))],
        out_specs=pl.BlockSpec((1, ts, W), lambda b, t: (b, t, 0)),
        out_shape=jax.ShapeDtypeStruct((B, S, W), F32),
        scratch_shapes=[pltpu.VMEM((ts + 8, W), F32), pltpu.VMEM((8, W), F32), pltpu.VMEM((1, W), F32)],
        compiler_params=_cparams(("parallel", "arbitrary")),
        name="rg_lru",
    )(proj3, proj3, conv_w, row(conv_b), wa_bd, row(ba), wx_bd, row(bx), row(lam))


def _xattn_kernel(x_ref, kv_ref, g_ref, wq_ref, qn_ref, kn_ref, wo_ref, o_ref):
    x = x_ref[0]
    q = _dot(_rms(x, g_ref[...]), wq_ref[...])
    kv = kv_ref[0]
    hw = XA_HEADS * XA_DH
    outs = []
    for h in range(XA_HEADS):
        hl = slice(h * XA_DH, (h + 1) * XA_DH)
        qh = _rms(q[:, hl], qn_ref[...])
        kh = _rms(kv[:, hl], kn_ref[...])
        vh = kv[:, hw + h * XA_DH: hw + (h + 1) * XA_DH]
        s = _dot_nt(qh, kh) * (XA_DH ** -0.5)
        s = s - jnp.max(s, axis=1, keepdims=True)
        p = jnp.exp(s)
        p = p / jnp.sum(p, axis=1, keepdims=True)
        outs.append(_dot(p, vh))
    o_ref[0] = x + _dot(jnp.concatenate(outs, axis=1), wo_ref[...])


def cross_attention(x3, kv3, g, wq_bf16, qn, kn, wo_bf16, tq=256):
    B, S, D = x3.shape
    M = kv3.shape[1]
    hw = XA_HEADS * XA_DH
    full = lambda shape: pl.BlockSpec(shape, lambda b, i: (0,) * len(shape))
    return pl.pallas_call(
        _xattn_kernel,
        grid=(B, S // tq),
        in_specs=[pl.BlockSpec((1, tq, D), lambda b, i: (b, i, 0)),
                  pl.BlockSpec((1, M, 2 * hw), lambda b, i: (b, 0, 0)),
                  full((1, D)), full((D, hw)), full((1, XA_DH)), full((1, XA_DH)), full((hw, D))],
        out_specs=pl.BlockSpec((1, tq, D), lambda b, i: (b, i, 0)),
        out_shape=jax.ShapeDtypeStruct((B, S, D), F32),
        compiler_params=_cparams(("parallel", "parallel")),
        name="cross_attention",
    )(x3, kv3, g.reshape(1, D), wq_bf16, qn.reshape(1, -1), kn.reshape(1, -1), wo_bf16)


def _top_values(s, k):
    rows = []
    for _ in range(k):
        m = jnp.max(s, axis=0, keepdims=True)
        rows.append(m)
        s = jnp.where(s == m, -jnp.inf, s)
    return rows


def _peer_kernel(x_ref, g_ref, wq_ref, sk_ref, u_ref, vt_ref, o_ref,
                 xn_ref, thr_ref, a_ref, e2_ref, s2_ref, p_ref, acc_ref, *, ib):
    e = pl.program_id(1)
    nk = PEER_NKEYS
    K = PEER_TOPK

    @pl.when(e == 0)
    def _():
        xn = _rms(x_ref[...], g_ref[...]).astype(BF16)
        xn_ref[...] = xn
        acc_ref[...] = jnp.zeros_like(acc_ref)

        def route(h, c):
            q = jnp.dot(xn, wq_ref[h], preferred_element_type=F32)
            s1 = _dot_nt(sk_ref[2 * h], q[:, :nk])
            s2 = _dot_nt(sk_ref[2 * h + 1], q[:, nk:])
            v1 = _top_values(s1, K)
            v2 = _top_values(s2, K)
            v2m = jnp.concatenate(v2, axis=0)
            cand = jnp.concatenate([v1[a] + v2m for a in range(K)], axis=0)
            top = _top_values(cand, K)
            tau = top[K - 1]
            smax = top[0]
            z = top[0] * 0.0
            for t in top:
                z = z + jnp.exp(t - smax)
            thr = jnp.full(s1.shape, jnp.inf, F32)
            for bb in range(K):
                thr = jnp.minimum(thr, jnp.where(s1 + v2[bb] >= tau, v2[bb], jnp.inf))
            thr_ref[h] = thr
            a_ref[h] = jnp.exp(s1 - v1[0]) / z
            e2_ref[h] = jnp.exp(s2 - v2[0])
            s2_ref[h] = s2
            return c

        lax.fori_loop(0, PEER_HEADS, route, 0)

    xn = xn_ref[...]
    for ii in range(ib):
        i = e * ib + ii
        hid = _dot_nt(u_ref[ii * nk:(ii + 1) * nk, :], xn)
        act = 0.5 * hid * (1.0 + lax.erf(hid * np.float32(np.sqrt(0.5))))
        w = jnp.zeros_like(hid)
        for h in range(PEER_HEADS):
            thr_row = thr_ref[h, pl.ds(i, 1), :]
            a_row = a_ref[h, pl.ds(i, 1), :]
            w = w + jnp.where(s2_ref[h] >= thr_row, e2_ref[h] * a_row, 0.0)
        p_ref[ii * nk:(ii + 1) * nk, :] = (w * act).astype(BF16)
    acc_ref[...] += jnp.dot(vt_ref[...], p_ref[...], preferred_element_type=F32)

    @pl.when(e == pl.num_programs(1) - 1)
    def _():
        o_ref[...] = x_ref[...] + acc_ref[...].T


def peer_ffn(x2d, g, wq3_bf16, sk_bf16, u_bf16, vt_bf16, tq=256, ib=4):
    T, D = x2d.shape
    ne = PEER_NKEYS // ib
    ec = ib * PEER_NKEYS
    H = PEER_HEADS
    full = lambda shape: pl.BlockSpec(shape, lambda i, e: (0,) * len(shape))
    route_scratch = pltpu.VMEM((H, PEER_NKEYS, tq), F32)
    return pl.pallas_call(
        functools.partial(_peer_kernel, ib=ib),
        grid=(T // tq, ne),
        in_specs=[pl.BlockSpec((tq, D), lambda i, e: (i, 0)),
                  full((1, D)), full((H, D, 2 * PEER_NKEYS)), full((2 * H, PEER_NKEYS, PEER_NKEYS)),
                  pl.BlockSpec((ec, D), lambda i, e: (e, 0)),
                  pl.BlockSpec((D, ec), lambda i, e: (0, e))],
        out_specs=pl.BlockSpec((tq, D), lambda i, e: (i, 0)),
        out_shape=jax.ShapeDtypeStruct((T, D), F32),
        scratch_shapes=[pltpu.VMEM((tq, D), BF16), route_scratch, route_scratch, route_scratch,
                        route_scratch, pltpu.VMEM((ec, tq), BF16), pltpu.VMEM((D, tq), F32)],
        compiler_params=_cparams(("parallel", "arbitrary")),
        name="peer_ffn",
    )(x2d, g.reshape(1, D), wq3_bf16, sk_bf16, u_bf16, vt_bf16)


def _block_diag(w):
    G, n, _ = w.shape
    eye = jnp.eye(G, dtype=w.dtype)
    return (eye[:, None, :, None] * w[:, :, None, :]).reshape(G * n, G * n)


def even_layer(x2d, B, S, norm, w_in, gate_up, gate_b, out_norm, w_out):
    D = x2d.shape[1]
    gg0 = EV_GG * LANES
    wp = jnp.concatenate([w_in[:, :1536], w_in[:, 1536 + GLA_RANK:], w_in[:, 1536:1536 + GLA_RANK],
                          jnp.zeros((D, EV_COLS - gg0 - GLA_RANK), w_in.dtype)], axis=1).astype(BF16)
    proj = norm_proj(x2d, norm, wp).reshape(B, S, EV_COLS)
    gup = jnp.concatenate([gate_up, jnp.zeros((LANES - GLA_RANK, gate_up.shape[1]), gate_up.dtype)], axis=0)
    a_out = gla_mixer(proj, gup.astype(BF16), gate_b, out_norm)
    b_out = sb_attention(proj)
    return out_proj(x2d, a_out.reshape(B * S, -1), b_out.reshape(B * S, -1), w_out.astype(BF16))


def odd_layer(x2d, B, S, positions, norm, w_in, q_norm, k_norm, conv_w, conv_b, wa, ba, wx, bx, lam, w_out):
    proj = norm_proj(x2d, norm, w_in.astype(BF16)).reshape(B, S, -1)
    half = HEAD_DIM // 2
    inv_freq = ROPE_THETA ** (-jnp.arange(0, HEAD_DIM, 2, dtype=F32) / HEAD_DIM)
    inv_tile = jnp.tile(inv_freq, LANES // half).reshape(1, LANES)
    seg = np.arange(OD_W) // HEAD_DIM
    blockdiag = jnp.asarray(seg[:, None] == seg[None, :], dtype=BF16)
    q_t, k_rot, v_t, kmean = moba_prep(proj, positions.reshape(B, S, 1), inv_tile,
                                       jnp.tile(q_norm, OD_W // HEAD_DIM).reshape(1, OD_W),
                                       jnp.tile(k_norm, OD_W // HEAD_DIM).reshape(1, OD_W), blockdiag)
    c_out = moba_attention(q_t, k_rot, v_t, kmean.reshape(B, S // MOBA_BLOCK, OD_W))
    d_out = rg_lru_mixer(proj, conv_w, conv_b, _block_diag(wa).astype(BF16), ba,
                         _block_diag(wx).astype(BF16), bx, lam)
    return out_proj(x2d, c_out.reshape(B * S, -1), d_out.reshape(B * S, -1), w_out.astype(BF16))


def kernel(x, mem, positions, ev_norm, ev_w_in, ev_gla_gate_up, ev_gla_gate_b, ev_gla_out_norm, ev_w_out, od_norm, od_w_in, od_q_norm, od_k_norm, od_conv_w, od_conv_b, od_gate_a_w, od_gate_a_b, od_gate_x_w, od_gate_x_b, od_lambda, od_w_out, xa_norm, xa_mem_norm, xa_wq, xa_wkv, xa_q_norm, xa_k_norm, xa_wo, ffn_norm, peer_wq, peer_subkeys, peer_u, peer_v):
    B, S, D = x.shape
    M = mem.shape[1]
    depth = xa_norm.shape[0]
    x2d = x.reshape(B * S, D)
    mem2d = mem.reshape(B * M, D)
    for l in range(depth):
        if l % 2 == 0:
            e = l // 2
            x2d = even_layer(x2d, B, S, ev_norm[e], ev_w_in[e], ev_gla_gate_up[e], ev_gla_gate_b[e],
                             ev_gla_out_norm[e], ev_w_out[e])
        else:
            o = l // 2
            x2d = odd_layer(x2d, B, S, positions, od_norm[o], od_w_in[o], od_q_norm[o], od_k_norm[o],
                            od_conv_w[o], od_conv_b[o], od_gate_a_w[o], od_gate_a_b[o],
                            od_gate_x_w[o], od_gate_x_b[o], od_lambda[o], od_w_out[o])
        kv = norm_proj(mem2d, xa_mem_norm[l], xa_wkv[l].astype(BF16))
        x2d = cross_attention(x2d.reshape(B, S, D), kv.reshape(B, M, -1), xa_norm[l],
                              xa_wq[l].astype(BF16), xa_q_norm[l], xa_k_norm[l],
                              xa_wo[l].astype(BF16)).reshape(B * S, D)
        H = PEER_HEADS
        wq3 = peer_wq[l].reshape(D, H, 2 * PEER_NKEYS).transpose(1, 0, 2).astype(BF16)
        sk = peer_subkeys[l].reshape(2 * H, PEER_NKEYS, -1).astype(BF16)
        x2d = peer_ffn(x2d, ffn_norm[l], wq3, sk, peer_u[l].astype(BF16), peer_v[l].T.astype(BF16))
    return x2d.reshape(B, S, D)
```

```python
import functools

import jax
import jax.numpy as jnp
import numpy as np
from jax import lax
from jax.experimental import pallas as pl
from jax.experimental.pallas import tpu as pltpu

F32 = jnp.float32
BF16 = jnp.bfloat16

EPS = 1e-6
ROPE_THETA = 10000.0
LANES = 128
HEAD_DIM = 64
GLA_HEADS = 4
GLA_DV = 128
GLA_CHUNK = 64
GLA_TAU = 16.0
GLA_RANK = 16
MOBA_BLOCK = 256
MOBA_TOPK = 3
RG_C = 8.0
RG_CONV = 4
XA_HEADS = 4
XA_DH = 128
PEER_HEADS = 8
PEER_NKEYS = 128
PEER_TOPK = 16
NEG = -1e30
F32_LOG_TINY = -104.0
VMEM_LIMIT = 56 * 1024 * 1024


def _cparams(sem):
    return pltpu.CompilerParams(dimension_semantics=sem, vmem_limit_bytes=VMEM_LIMIT)


def _dot(a, b):
    return jnp.dot(a.astype(BF16), b.astype(BF16), preferred_element_type=F32)


def _dot_nt(a, b):
    return lax.dot_general(a.astype(BF16), b.astype(BF16), (((1,), (1,)), ((), ())),
                           preferred_element_type=F32)


def _dot_tn(a, b):
    return lax.dot_general(a.astype(BF16), b.astype(BF16), (((0,), (0,)), ((), ())),
                           preferred_element_type=F32)


def _split2(a):
    hi = a.astype(BF16)
    lo = (a - hi.astype(F32)).astype(BF16)
    return hi, lo


def _dot_exact_rhs(a, m_bf16):
    hi, lo = _split2(a)
    return (jnp.dot(hi, m_bf16, preferred_element_type=F32)
            + jnp.dot(lo, m_bf16, preferred_element_type=F32))


def _dot_exact_lhs(m_bf16, a):
    hi, lo = _split2(a)
    return (jnp.dot(m_bf16, hi, preferred_element_type=F32)
            + jnp.dot(m_bf16, lo, preferred_element_type=F32))


def _rms(x, g):
    return x * lax.rsqrt(jnp.mean(x * x, axis=-1, keepdims=True) + EPS) * g


def _softplus(z):
    return jnp.maximum(z, 0.0) + jnp.log1p(jnp.exp(-jnp.abs(z)))


def _head_mask(hh):
    lane = lax.broadcasted_iota(jnp.int32, (1, LANES), 1)
    return ((lane // HEAD_DIM) == hh).astype(F32)


def _norm_proj_kernel(x_ref, g_ref, w_ref, o_ref):
    xn = _rms(x_ref[...], g_ref[...])
    o_ref[...] = jnp.dot(xn.astype(BF16), w_ref[...], preferred_element_type=F32)


def norm_proj(x2d, g, w_bf16, tm=256):
    T, D = x2d.shape
    N = w_bf16.shape[1]
    return pl.pallas_call(
        _norm_proj_kernel,
        grid=(T // tm,),
        in_specs=[pl.BlockSpec((tm, D), lambda i: (i, 0)),
                  pl.BlockSpec((1, D), lambda i: (0, 0)),
                  pl.BlockSpec((D, N), lambda i: (0, 0))],
        out_specs=pl.BlockSpec((tm, N), lambda i: (i, 0)),
        out_shape=jax.ShapeDtypeStruct((T, N), F32),
        compiler_params=_cparams(("parallel",)),
        name="norm_proj",
    )(x2d, g.reshape(1, D), w_bf16)


def _out_proj_kernel(x_ref, a_ref, b_ref, w_ref, o_ref):
    ka = a_ref.shape[1]
    o_ref[...] = (x_ref[...] + _dot(a_ref[...], w_ref[0:ka, :])
                  + _dot(b_ref[...], w_ref[ka:, :]))


def out_proj(x2d, a2d, b2d, w_bf16, tm=512):
    T, D = x2d.shape
    ka, kb = a2d.shape[1], b2d.shape[1]
    return pl.pallas_call(
        _out_proj_kernel,
        grid=(T // tm,),
        in_specs=[pl.BlockSpec((tm, D), lambda i: (i, 0)),
                  pl.BlockSpec((tm, ka), lambda i: (i, 0)),
                  pl.BlockSpec((tm, kb), lambda i: (i, 0)),
                  pl.BlockSpec((ka + kb, D), lambda i: (0, 0))],
        out_specs=pl.BlockSpec((tm, D), lambda i: (i, 0)),
        out_shape=jax.ShapeDtypeStruct((T, D), F32),
        compiler_params=_cparams(("parallel",)),
        name="out_proj",
    )(x2d, a2d, b2d, w_bf16)


EV_Q, EV_K, EV_V, EV_R, EV_SQ, EV_SK, EV_SV, EV_GG = 0, 2, 4, 8, 12, 16, 20, 24
EV_COLS = 25 * LANES


def _gla_kernel(q_ref, k_ref, v_ref, r_ref, gg_ref, gup_ref, gb_ref, onorm_ref, o_ref, state_ref,
                *, ts):
    C = GLA_CHUNK

    @pl.when(pl.program_id(1) == 0)
    def _():
        state_ref[...] = jnp.zeros_like(state_ref)

    row = lax.broadcasted_iota(jnp.int32, (C, C), 0)
    col = lax.broadcasted_iota(jnp.int32, (C, C), 1)
    causal = row >= col
    tri = causal.astype(BF16)
    masks = [_head_mask(0), _head_mask(1)]
    scale = HEAD_DIM ** -0.5
    onorm = onorm_ref[...]

    for ci in range(ts // C):
        sl = slice(ci * C, (ci + 1) * C)
        pre = _dot(gg_ref[0, sl, :], gup_ref[...]) + gb_ref[...]
        g = (jnp.minimum(pre, 0.0) - jnp.log1p(jnp.exp(-jnp.abs(pre)))) * (1.0 / GLA_TAU)
        b = _dot_exact_lhs(tri, g)
        bmid = b[C // 2 - 1:C // 2, :]
        blast = b[C - 1:C, :]
        q = q_ref[0, sl, :] * scale
        k = k_ref[0, sl, :]
        qd = q * jnp.exp(b - bmid)
        kd = k * jnp.exp(bmid - b)
        qe = q * jnp.exp(b)
        kdec = k * jnp.exp(blast - b)
        eb_last = jnp.exp(blast)
        for p in range(GLA_HEADS // 2):
            lanes = slice(p * LANES, (p + 1) * LANES)
            st = state_ref[p]
            new_st = st * eb_last[:, lanes]
            for hh in range(2):
                h = 2 * p + hh
                m = masks[hh]
                hl = slice(h * GLA_DV, (h + 1) * GLA_DV)
                a = _dot_nt(qd[:, lanes] * m, kd[:, lanes])
                a = jnp.where(causal, a, 0.0)
                v_h = v_ref[0, sl, hl]
                o = _dot(a, v_h) + _dot_nt(qe[:, lanes] * m, st)
                o = _rms(o, onorm)
                r_h = r_ref[0, sl, hl]
                o_ref[0, sl, hl] = o * (r_h * jax.nn.sigmoid(r_h))
                new_st = new_st + _dot_tn(v_h, kdec[:, lanes] * m)
            state_ref[p] = new_st


def gla_mixer(proj3, gate_up_pad, gate_b, out_norm, ts=256):
    B, S, _ = proj3.shape
    nh = GLA_HEADS
    w = nh * GLA_DV

    def col(blk_w, tile):
        idx = tile * LANES // blk_w
        return pl.BlockSpec((1, ts, blk_w), lambda b, c: (b, c, idx))

    return pl.pallas_call(
        functools.partial(_gla_kernel, ts=ts),
        grid=(B, S // ts),
        in_specs=[col(nh * HEAD_DIM, EV_Q), col(nh * HEAD_DIM, EV_K), col(w, EV_V), col(w, EV_R),
                  col(LANES, EV_GG),
                  pl.BlockSpec((LANES, nh * HEAD_DIM), lambda b, c: (0, 0)),
                  pl.BlockSpec((1, nh * HEAD_DIM), lambda b, c: (0, 0)),
                  pl.BlockSpec((1, GLA_DV), lambda b, c: (0, 0))],
        out_specs=pl.BlockSpec((1, ts, w), lambda b, c: (b, c, 0)),
        out_shape=jax.ShapeDtypeStruct((B, S, w), F32),
        scratch_shapes=[pltpu.VMEM((nh // 2, GLA_DV, LANES), F32)],
        compiler_params=_cparams(("parallel", "arbitrary")),
        name="gla",
    )(proj3, proj3, proj3, proj3, proj3, gate_up_pad, gate_b.reshape(1, -1), out_norm.reshape(1, -1))


def _sb_kernel(q_ref, k_ref, v_ref, o_ref, acc_ref, carry_ref, *, tq):
    i = pl.program_id(2)
    q = q_ref[0] * (HEAD_DIM ** -0.5)
    masks = [_head_mask(0), _head_mask(1)]
    qh = [(q * m).astype(BF16) for m in masks]
    row = lax.broadcasted_iota(jnp.int32, (tq, tq), 0)
    col = lax.broadcasted_iota(jnp.int32, (tq, tq), 1)
    upper = (row > col).astype(BF16)
    past = col < row
    acc_ref[...] = jnp.zeros_like(acc_ref)
    carry_ref[...] = jnp.zeros_like(carry_ref)

    def tile(j, diag):
        off = pl.multiple_of(j * tq, tq)
        k = k_ref[0, pl.ds(off, tq), :].astype(BF16)
        v = v_ref[0, pl.ds(off, tq), :].astype(BF16)
        for hh in range(2):
            z = lax.dot_general(qh[hh], k, (((1,), (1,)), ((), ())), preferred_element_type=F32)
            sp = _softplus(z)
            log_rem = jnp.where(past, -sp, 0.0) if diag else -sp
            carry = carry_ref[hh]
            after = _dot_exact_rhs(log_rem, upper) + carry
            w = jnp.exp((z - sp) + after)
            if diag:
                w = jnp.where(past, w, 0.0)
            acc_ref[hh] += jnp.dot(w.astype(BF16), v, preferred_element_type=F32)
            carry_ref[hh] = carry + jnp.sum(log_rem, axis=1, keepdims=True)

    tile(i, True)

    def live(c):
        j, worst = c
        return jnp.logical_and(j >= 0, worst > F32_LOG_TINY)

    def body(c):
        j, _ = c
        tile(j, False)
        return j - 1, jnp.max(carry_ref[...])

    lax.while_loop(live, body, (i - 1, jnp.max(carry_ref[...])))
    o_ref[0] = acc_ref[0] * masks[0] + acc_ref[1] * masks[1]


def sb_attention(proj3, tq=256):
    B, S, _ = proj3.shape
    npairs = 4
    return pl.pallas_call(
        functools.partial(_sb_kernel, tq=tq),
        grid=(B, npairs, S // tq),
        in_specs=[pl.BlockSpec((1, tq, LANES), lambda b, p, i: (b, i, EV_SQ + p)),
                  pl.BlockSpec((1, S, LANES), lambda b, p, i: (b, 0, EV_SK + p)),
                  pl.BlockSpec((1, S, LANES), lambda b, p, i: (b, 0, EV_SV + p))],
        out_specs=pl.BlockSpec((1, tq, LANES), lambda b, p, i: (b, i, p)),
        out_shape=jax.ShapeDtypeStruct((B, S, npairs * LANES), F32),
        scratch_shapes=[pltpu.VMEM((2, tq, LANES), F32), pltpu.VMEM((2, tq, 1), F32)],
        compiler_params=_cparams(("parallel", "parallel", "arbitrary")),
        name="sb_attention",
    )(proj3, proj3, proj3)


OD_Q, OD_K, OD_V, OD_RX, OD_RG = 0, 4, 8, 12, 16
OD_W = 4 * LANES


def _moba_prep_kernel(q_ref, k_ref, v_ref, pos_ref, invf_ref, qn_ref, kn_ref, bd_ref,
                      qt_ref, ko_ref, vt_ref, km_ref):
    ang = pos_ref[0].astype(F32) * invf_ref[...]
    cos1, sin1 = jnp.cos(ang), jnp.sin(ang)
    lane = lax.broadcasted_iota(jnp.int32, (1, LANES), 1)
    first_half = (lane % HEAD_DIM) < (HEAD_DIM // 2)
    bd = bd_ref[...]

    def norm_rope(x, gain):
        ms = _dot_exact_rhs(x * x, bd) * (1.0 / HEAD_DIM)
        xn = x * lax.rsqrt(ms + EPS) * gain
        outs = []
        for t in range(OD_W // LANES):
            xb = xn[:, t * LANES:(t + 1) * LANES]
            up = pltpu.roll(xb, LANES - HEAD_DIM // 2, axis=1)
            dn = pltpu.roll(xb, HEAD_DIM // 2, axis=1)
            outs.append(xb * cos1 + jnp.where(first_half, -up, dn) * sin1)
        return jnp.concatenate(outs, axis=1)

    qt_ref[0] = norm_rope(q_ref[0], qn_ref[...]).T.astype(BF16)
    kr = norm_rope(k_ref[0], kn_ref[...])
    ko_ref[0] = kr.astype(BF16)
    km_ref[0, 0] = jnp.mean(kr, axis=0, keepdims=True)
    vt_ref[0] = v_ref[0].T.astype(BF16)


def moba_prep(proj3, pos3, inv_freq_tile, qn_tile, kn_tile, blockdiag):
    B, S, _ = proj3.shape
    tb = MOBA_BLOCK
    nb = S // tb
    full = lambda shape: pl.BlockSpec(shape, lambda b, i: (0,) * len(shape))
    return pl.pallas_call(
        _moba_prep_kernel,
        grid=(B, nb),
        in_specs=[pl.BlockSpec((1, tb, OD_W), lambda b, i: (b, i, OD_Q // 4)),
                  pl.BlockSpec((1, tb, OD_W), lambda b, i: (b, i, OD_K // 4)),
                  pl.BlockSpec((1, tb, OD_W), lambda b, i: (b, i, OD_V // 4)),
                  pl.BlockSpec((1, tb, 1), lambda b, i: (b, i, 0)),
                  full((1, LANES)), full((1, OD_W)), full((1, OD_W)), full((OD_W, OD_W))],
        out_specs=[pl.BlockSpec((1, OD_W, tb), lambda b, i: (b, 0, i)),
                   pl.BlockSpec((1, tb, OD_W), lambda b, i: (b, i, 0)),
                   pl.BlockSpec((1, OD_W, tb), lambda b, i: (b, 0, i)),
                   pl.BlockSpec((1, 1, 1, OD_W), lambda b, i: (b, i, 0, 0))],
        out_shape=[jax.ShapeDtypeStruct((B, OD_W, S), BF16),
                   jax.ShapeDtypeStruct((B, S, OD_W), BF16),
                   jax.ShapeDtypeStruct((B, OD_W, S), BF16),
                   jax.ShapeDtypeStruct((B, nb, 1, OD_W), F32)],
        compiler_params=_cparams(("parallel", "parallel")),
        name="moba_prep",
    )(proj3, proj3, proj3, pos3, inv_freq_tile, qn_tile, kn_tile, blockdiag)


def _moba_kernel(qt_ref, k_ref, vt_ref, km_ref, o_ref, acc_ref, m_ref, sel_ref):
    tb = MOBA_BLOCK
    nbp = km_ref.shape[1]
    i = pl.program_id(2)
    qt = qt_ref[0]
    km = km_ref[0].astype(BF16)
    chan = lax.broadcasted_iota(jnp.int32, (LANES, 1), 0)
    in_head = [(chan // HEAD_DIM) == hh for hh in range(2)]
    blk = lax.broadcasted_iota(jnp.int32, (nbp, tb), 0).astype(F32)
    own = i.astype(F32)
    key = lax.broadcasted_iota(jnp.int32, (tb, tb), 0)
    qry = lax.broadcasted_iota(jnp.int32, (tb, tb), 1)
    visible = key <= qry
    off_own = pl.multiple_of(i * tb, tb)
    k_own = k_ref[0, pl.ds(off_own, tb), :]
    vt_own = vt_ref[0, :, pl.ds(off_own, tb)]
    zero = jnp.zeros_like(qt)
    one = jnp.ones_like(qt)
    qs = []
    for hh in range(2):
        qm = jnp.where(in_head[hh], qt, zero)
        gate = jnp.dot(km, qm, preferred_element_type=F32)
        g = jnp.where(blk < own, gate, -jnp.inf)
        sel = jnp.zeros((nbp, tb), F32)
        for r in range(MOBA_TOPK):
            mx = jnp.max(g, axis=0, keepdims=True)
            idx = jnp.min(jnp.where(g == mx, blk, float(nbp)), axis=0, keepdims=True)
            hit = blk == idx
            keep = jnp.where(i > r, 1.0, 0.0)
            sel = sel + jnp.where(hit, keep, 0.0)
            g = jnp.where(hit, -jnp.inf, g)
        sel_ref[hh] = sel
        qb = qm * (HEAD_DIM ** -0.5)
        qs.append(qb)
        s = jnp.dot(k_own, qb, preferred_element_type=F32)
        s = jnp.where(visible, s, NEG)
        m0 = jnp.max(s, axis=0, keepdims=True)
        p = jnp.exp(s - m0)
        m_ref[hh] = m0
        acc_ref[hh] = jnp.dot(jnp.where(in_head[hh], vt_own, one), p.astype(BF16),
                              preferred_element_type=F32)

    def past_blocks(n, nblk):
        off = pl.multiple_of(n * tb, tb)
        k = k_ref[0, pl.ds(off, nblk * tb), :]
        vt = vt_ref[0, :, pl.ds(off, nblk * tb)]
        scores = [jnp.dot(k, qs[hh], preferred_element_type=F32) for hh in range(2)]
        probs, alphas = [], []
        for hh in range(2):
            parts = []
            for j in range(nblk):
                chosen = sel_ref[hh, pl.ds(n + j, 1), :]
                parts.append(jnp.where(chosen > 0.0, scores[hh][j * tb:(j + 1) * tb, :], NEG))
            m_old = m_ref[hh]
            m_new = m_old
            for part in parts:
                m_new = jnp.maximum(m_new, jnp.max(part, axis=0, keepdims=True))
            alphas.append(jnp.exp(m_old - m_new))
            probs.append(jnp.concatenate([jnp.exp(part - m_new).astype(BF16) for part in parts], axis=0))
            m_ref[hh] = m_new
        ones_v = jnp.ones_like(vt)
        pv = [jnp.dot(jnp.where(in_head[hh], vt, ones_v), probs[hh], preferred_element_type=F32)
              for hh in range(2)]
        for hh in range(2):
            acc_ref[hh] = acc_ref[hh] * alphas[hh] + pv[hh]

    def two_blocks(t, c):
        past_blocks(2 * t, 2)
        return c

    lax.fori_loop(0, i // 2, two_blocks, 0)

    @pl.when(i % 2 == 1)
    def _():
        past_blocks(i - 1, 1)

    a0, a1 = acc_ref[0], acc_ref[1]
    den0 = a0[HEAD_DIM:HEAD_DIM + 1, :]
    den1 = a1[0:1, :]
    o_ref[0] = jnp.where(in_head[0], a0 / den0, a1 / den1).T


def moba_attention(q_t, k_rot, v_t, kmean):
    B, S, _ = k_rot.shape
    tb = MOBA_BLOCK
    nb = S // tb
    npairs = OD_W // LANES
    nbp = -(-nb // LANES) * LANES
    kmean = jnp.pad(kmean, ((0, 0), (0, nbp - nb), (0, 0)))
    return pl.pallas_call(
        _moba_kernel,
        grid=(B, npairs, nb),
        in_specs=[pl.BlockSpec((1, LANES, tb), lambda b, p, i: (b, p, i)),
                  pl.BlockSpec((1, S, LANES), lambda b, p, i: (b, 0, p)),
                  pl.BlockSpec((1, LANES, S), lambda b, p, i: (b, p, 0)),
                  pl.BlockSpec((1, nbp, LANES), lambda b, p, i: (b, 0, p))],
        out_specs=pl.BlockSpec((1, tb, LANES), lambda b, p, i: (b, i, p)),
        out_shape=jax.ShapeDtypeStruct((B, S, OD_W), F32),
        scratch_shapes=[pltpu.VMEM((2, LANES, tb), F32), pltpu.VMEM((2, 1, tb), F32),
                        pltpu.VMEM((2, nbp, tb), F32)],
        compiler_params=_cparams(("parallel", "parallel", "arbitrary")),
        name="moba_attention",
    )(q_t, k_rot, v_t, kmean)


def _rglru_kernel(x_ref, gate_ref, cw_ref, cb_ref, wa_ref, ba_ref, wx_ref, bx_ref, lam_ref, o_ref,
                  buf_ref, xprev_ref, hprev_ref, *, ts):
    @pl.when(pl.program_id(1) == 0)
    def _():
        xprev_ref[...] = jnp.zeros_like(xprev_ref)
        hprev_ref[...] = jnp.zeros_like(hprev_ref)

    x = x_ref[0]
    buf_ref[0:8, :] = xprev_ref[...]
    buf_ref[8:8 + ts, :] = x
    xprev_ref[...] = x[ts - 8:ts, :]
    xc = cb_ref[...]
    for kk in range(RG_CONV):
        start = 8 - (RG_CONV - 1) + kk
        xc = xc + cw_ref[kk:kk + 1, :] * buf_ref[start:start + ts, :]
    r = jax.nn.sigmoid(_dot(xc, wa_ref[...]) + ba_ref[...])
    ig = jax.nn.sigmoid(_dot(xc, wx_ref[...]) + bx_ref[...])
    log_a = (-RG_C) * r * _softplus(-lam_ref[...])
    a = jnp.exp(log_a)
    u = jnp.sqrt(-jnp.tanh(log_a) * (a * a + 1.0)) * (ig * xc)
    t_idx = lax.broadcasted_iota(jnp.int32, (ts, 1), 0)
    d = 1
    while d < ts:
        valid = t_idx >= d
        a_sh = pltpu.roll(a, d, axis=0)
        u_sh = pltpu.roll(u, d, axis=0)
        u = jnp.where(valid, a * u_sh, 0.0) + u
        a = jnp.where(valid, a * a_sh, a)
        d *= 2
    h = a * hprev_ref[...] + u
    hprev_ref[...] = h[ts - 1:ts, :]
    o_ref[0] = h * jax.nn.gelu(gate_ref[0], approximate=True)


def rg_lru_mixer(proj3, conv_w, conv_b, wa_bd, ba, wx_bd, bx, lam, ts=256):
    B, S, _ = proj3.shape
    W = conv_w.shape[1]
    full = lambda shape: pl.BlockSpec(shape, lambda b, t: (0,) * len(shape))
    row = lambda v: v.reshape(1, W)
    return pl.pallas_call(
        functools.partial(_rglru_kernel, ts=ts),
        grid=(B, S // ts),
        in_specs=[pl.BlockSpec((1, ts, W), lambda b, t: (b, t, OD_RX // 4)),
                  pl.BlockSpec((1, ts, W), lambda b, t: (b, t, OD_RG // 4)),
                  full((RG_CONV, W)), full((1, W)), full((W, W)), full((1, W)), full((W, W)),
                  full((1, W)), full((1, W))],
        out_specs=pl.BlockSpec((1, ts, W), lambda b, t: (b, t, 0)),
        out_shape=jax.ShapeDtypeStruct((B, S, W), F32),
        scratch_shapes=[pltpu.VMEM((ts + 8, W), F32), pltpu.VMEM((8, W), F32), pltpu.VMEM((1, W), F32)],
        compiler_params=_cparams(("parallel", "arbitrary")),
        name="rg_lru",
    )(proj3, proj3, conv_w, row(conv_b), wa_bd, row(ba), wx_bd, row(bx), row(lam))


def _xattn_kernel(x_ref, kv_ref, g_ref, wq_ref, qn_ref, kn_ref, wo_ref, o_ref):
    x = x_ref[0]
    q = _dot(_rms(x, g_ref[...]), wq_ref[...])
    kv = kv_ref[0]
    hw = XA_HEADS * XA_DH
    outs = []
    for h in range(XA_HEADS):
        hl = slice(h * XA_DH, (h + 1) * XA_DH)
        qh = _rms(q[:, hl], qn_ref[...])
        kh = _rms(kv[:, hl], kn_ref[...])
        vh = kv[:, hw + h * XA_DH: hw + (h + 1) * XA_DH]
        s = _dot_nt(qh, kh) * (XA_DH ** -0.5)
        s = s - jnp.max(s, axis=1, keepdims=True)
        p = jnp.exp(s)
        p = p / jnp.sum(p, axis=1, keepdims=True)
        outs.append(_dot(p, vh))
    o_ref[0] = x + _dot(jnp.concatenate(outs, axis=1), wo_ref[...])


def cross_attention(x3, kv3, g, wq_bf16, qn, kn, wo_bf16, tq=256):
    B, S, D = x3.shape
    M = kv3.shape[1]
    hw = XA_HEADS * XA_DH
    full = lambda shape: pl.BlockSpec(shape, lambda b, i: (0,) * len(shape))
    return pl.pallas_call(
        _xattn_kernel,
        grid=(B, S // tq),
        in_specs=[pl.BlockSpec((1, tq, D), lambda b, i: (b, i, 0)),
                  pl.BlockSpec((1, M, 2 * hw), lambda b, i: (b, 0, 0)),
                  full((1, D)), full((D, hw)), full((1, XA_DH)), full((1, XA_DH)), full((hw, D))],
        out_specs=pl.BlockSpec((1, tq, D), lambda b, i: (b, i, 0)),
        out_shape=jax.ShapeDtypeStruct((B, S, D), F32),
        compiler_params=_cparams(("parallel", "parallel")),
        name="cross_attention",
    )(x3, kv3, g.reshape(1, D), wq_bf16, qn.reshape(1, -1), kn.reshape(1, -1), wo_bf16)


BF16_ROWS = 16


def _top_values(s, k, want_rank=False):
    rows = []
    rank = jnp.full(s.shape, float(k), F32) if want_rank else None
    for r in range(k):
        m = jnp.max(s, axis=0, keepdims=True)
        rows.append(m)
        hit = s == m
        if want_rank:
            rank = jnp.where(hit, float(r), rank)
        s = jnp.where(hit, -jnp.inf, s)
    return rows, rank


def _peer_kernel(x_ref, g_ref, wq_ref, sk_ref, u_ref, vt_ref, o_ref,
                 xnt_ref, c_ref, a_ref, r2_ref, e2_ref, acc_ref, *, ib, isub):
    e = pl.program_id(1)
    nk = PEER_NKEYS
    K = PEER_TOPK
    tq = x_ref.shape[0]

    @pl.when(e == 0)
    def _():
        xn32 = _rms(x_ref[...], g_ref[...])
        xn = xn32.astype(BF16)
        xnt_ref[...] = xn32.T.astype(BF16)
        acc_ref[...] = jnp.zeros_like(acc_ref)
        sub = lax.broadcasted_iota(jnp.int32, (8, 1), 0)

        def route(h, c):
            q = jnp.dot(xn, wq_ref[h], preferred_element_type=F32)
            s1 = _dot_nt(sk_ref[2 * h], q[:, :nk])
            s2 = _dot_nt(sk_ref[2 * h + 1], q[:, nk:])
            v1, _ = _top_values(s1, K)
            v2, rank2 = _top_values(s2, K, want_rank=True)
            v1m = jnp.concatenate(v1, axis=0)
            v2m = jnp.concatenate(v2, axis=0)
            groups = [v1[0] + v2m]
            for a in range(1, 8):
                groups.append(jnp.where(sub < K // (a + 1), v1[a] + v2m[0:8], -jnp.inf))
            groups.append(v1m[8:16] + v2[0])
            top, _ = _top_values(jnp.concatenate(groups, axis=0), K)
            tau = top[K - 1]
            z = jnp.zeros_like(tau)
            for t in top:
                z = z + jnp.exp(t - top[0])
            count = jnp.zeros(s1.shape, F32)
            for bb in range(K):
                count = count + jnp.where(s1 + v2[bb] >= tau, 1.0, 0.0)
            c_ref[h] = count
            a_ref[h] = jnp.exp(s1 - v1[0]) / z
            r2_ref[h] = rank2.astype(BF16)
            e2_ref[h] = jnp.exp(s2 - v2[0]).astype(BF16)
            return c

        lax.fori_loop(0, PEER_HEADS, route, 0)

    ngrp = nk // BF16_ROWS
    zero = jnp.zeros((BF16_ROWS, tq), BF16)
    xnt = xnt_ref[...]
    first = pl.multiple_of(e * ib, ib)
    step_c = [c_ref[h, pl.ds(first, ib), :] for h in range(PEER_HEADS)]
    step_a = [a_ref[h, pl.ds(first, ib), :] for h in range(PEER_HEADS)]
    total = None
    for sc in range(ib // isub):
        lo = sc * isub * nk
        hid_sub = jnp.dot(u_ref[lo:lo + isub * nk, :], xnt, preferred_element_type=F32)
        pieces = []
        for ii in range(isub):
            il = sc * isub + ii
            hid = hid_sub[ii * nk:(ii + 1) * nk, :]
            act = (0.5 * hid * (1.0 + lax.erf(hid * np.float32(np.sqrt(0.5))))).astype(BF16)
            w = [zero] * ngrp
            for h in range(PEER_HEADS):
                cnt = jnp.broadcast_to(step_c[h][il:il + 1, :], (BF16_ROWS, tq)).astype(BF16)
                amp = jnp.broadcast_to(step_a[h][il:il + 1, :], (BF16_ROWS, tq)).astype(BF16)
                for gi in range(ngrp):
                    rows = slice(gi * BF16_ROWS, (gi + 1) * BF16_ROWS)
                    w[gi] = w[gi] + jnp.where(r2_ref[h, rows, :] < cnt, e2_ref[h, rows, :] * amp, zero)
            pieces += [w[gi] * act[gi * BF16_ROWS:(gi + 1) * BF16_ROWS, :] for gi in range(ngrp)]
        part = jnp.dot(vt_ref[:, lo:lo + isub * nk], jnp.concatenate(pieces, axis=0),
                       preferred_element_type=F32)
        total = part if total is None else total + part
    acc_ref[...] += total

    @pl.when(e == pl.num_programs(1) - 1)
    def _():
        o_ref[...] = x_ref[...] + acc_ref[...].T


def peer_ffn(x2d, g, wq3_bf16, sk_bf16, u_bf16, vt_bf16, tq=256, ib=16, isub=4):
    T, D = x2d.shape
    ne = PEER_NKEYS // ib
    ec = ib * PEER_NKEYS
    H = PEER_HEADS
    full = lambda shape: pl.BlockSpec(shape, lambda i, e: (0,) * len(shape))
    words = pltpu.VMEM((H, PEER_NKEYS, tq), F32)
    halves = pltpu.VMEM((H, PEER_NKEYS, tq), BF16)
    return pl.pallas_call(
        functools.partial(_peer_kernel, ib=ib, isub=isub),
        grid=(T // tq, ne),
        in_specs=[pl.BlockSpec((tq, D), lambda i, e: (i, 0)),
                  full((1, D)), full((H, D, 2 * PEER_NKEYS)), full((2 * H, PEER_NKEYS, PEER_NKEYS)),
                  pl.BlockSpec((ec, D), lambda i, e: (e, 0)),
                  pl.BlockSpec((D, ec), lambda i, e: (0, e))],
        out_specs=pl.BlockSpec((tq, D), lambda i, e: (i, 0)),
        out_shape=jax.ShapeDtypeStruct((T, D), F32),
        scratch_shapes=[pltpu.VMEM((D, tq), BF16), words, words, halves, halves,
                        pltpu.VMEM((D, tq), F32)],
        compiler_params=_cparams(("parallel", "arbitrary")),
        name="peer_ffn",
    )(x2d, g.reshape(1, D), wq3_bf16, sk_bf16, u_bf16, vt_bf16)


def _block_diag(w):
    G, n, _ = w.shape
    eye = jnp.eye(G, dtype=w.dtype)
    return (eye[:, None, :, None] * w[:, :, None, :]).reshape(G * n, G * n)


def even_layer(x2d, B, S, norm, w_in, gate_up, gate_b, out_norm, w_out):
    D = x2d.shape[1]
    gg0 = EV_GG * LANES
    wp = jnp.concatenate([w_in[:, :1536], w_in[:, 1536 + GLA_RANK:], w_in[:, 1536:1536 + GLA_RANK],
                          jnp.zeros((D, EV_COLS - gg0 - GLA_RANK), w_in.dtype)], axis=1).astype(BF16)
    proj = norm_proj(x2d, norm, wp).reshape(B, S, EV_COLS)
    gup = jnp.concatenate([gate_up, jnp.zeros((LANES - GLA_RANK, gate_up.shape[1]), gate_up.dtype)], axis=0)
    a_out = gla_mixer(proj, gup.astype(BF16), gate_b, out_norm)
    b_out = sb_attention(proj)
    return out_proj(x2d, a_out.reshape(B * S, -1), b_out.reshape(B * S, -1), w_out.astype(BF16))


def odd_layer(x2d, B, S, positions, norm, w_in, q_norm, k_norm, conv_w, conv_b, wa, ba, wx, bx, lam, w_out):
    proj = norm_proj(x2d, norm, w_in.astype(BF16)).reshape(B, S, -1)
    half = HEAD_DIM // 2
    inv_freq = ROPE_THETA ** (-jnp.arange(0, HEAD_DIM, 2, dtype=F32) / HEAD_DIM)
    inv_tile = jnp.tile(inv_freq, LANES // half).reshape(1, LANES)
    seg = np.arange(OD_W) // HEAD_DIM
    blockdiag = jnp.asarray(seg[:, None] == seg[None, :], dtype=BF16)
    q_t, k_rot, v_t, kmean = moba_prep(proj, positions.reshape(B, S, 1), inv_tile,
                                       jnp.tile(q_norm, OD_W // HEAD_DIM).reshape(1, OD_W),
                                       jnp.tile(k_norm, OD_W // HEAD_DIM).reshape(1, OD_W), blockdiag)
    c_out = moba_attention(q_t, k_rot, v_t, kmean.reshape(B, S // MOBA_BLOCK, OD_W))
    d_out = rg_lru_mixer(proj, conv_w, conv_b, _block_diag(wa).astype(BF16), ba,
                         _block_diag(wx).astype(BF16), bx, lam)
    return out_proj(x2d, c_out.reshape(B * S, -1), d_out.reshape(B * S, -1), w_out.astype(BF16))


def kernel(x, mem, positions, ev_norm, ev_w_in, ev_gla_gate_up, ev_gla_gate_b, ev_gla_out_norm, ev_w_out, od_norm, od_w_in, od_q_norm, od_k_norm, od_conv_w, od_conv_b, od_gate_a_w, od_gate_a_b, od_gate_x_w, od_gate_x_b, od_lambda, od_w_out, xa_norm, xa_mem_norm, xa_wq, xa_wkv, xa_q_norm, xa_k_norm, xa_wo, ffn_norm, peer_wq, peer_subkeys, peer_u, peer_v):
    B, S, D = x.shape
    M = mem.shape[1]
    depth = xa_norm.shape[0]
    x2d = x.reshape(B * S, D)
    mem2d = mem.reshape(B * M, D)
    for l in range(depth):
        if l % 2 == 0:
            e = l // 2
            x2d = even_layer(x2d, B, S, ev_norm[e], ev_w_in[e], ev_gla_gate_up[e], ev_gla_gate_b[e],
                             ev_gla_out_norm[e], ev_w_out[e])
        else:
            o = l // 2
            x2d = odd_layer(x2d, B, S, positions, od_norm[o], od_w_in[o], od_q_norm[o], od_k_norm[o],
                            od_conv_w[o], od_conv_b[o], od_gate_a_w[o], od_gate_a_b[o],
                            od_gate_x_w[o], od_gate_x_b[o], od_lambda[o], od_w_out[o])
        kv = norm_proj(mem2d, xa_mem_norm[l], xa_wkv[l].astype(BF16))
        x2d = cross_attention(x2d.reshape(B, S, D), kv.reshape(B, M, -1), xa_norm[l],
                              xa_wq[l].astype(BF16), xa_q_norm[l], xa_k_norm[l],
                              xa_wo[l].astype(BF16)).reshape(B * S, D)
        H = PEER_HEADS
        wq3 = peer_wq[l].reshape(D, H, 2 * PEER_NKEYS).transpose(1, 0, 2).astype(BF16)
        sk = peer_subkeys[l].reshape(2 * H, PEER_NKEYS, -1).astype(BF16)
        x2d = peer_ffn(x2d, ffn_norm[l], wq3, sk, peer_u[l].astype(BF16), peer_v[l].T.astype(BF16))
    return x2d.reshape(B, S, D)
```

```python
import functools

import jax
import jax.numpy as jnp
import numpy as np
from jax import lax
from jax.experimental import pallas as pl
from jax.experimental.pallas import tpu as pltpu

F32 = jnp.float32
BF16 = jnp.bfloat16

EPS = 1e-6
ROPE_THETA = 10000.0
LANES = 128
HEAD_DIM = 64
GLA_HEADS = 4
GLA_DV = 128
GLA_CHUNK = 64
GLA_TAU = 16.0
GLA_RANK = 16
MOBA_BLOCK = 256
MOBA_TOPK = 3
RG_C = 8.0
RG_CONV = 4
XA_HEADS = 4
XA_DH = 128
PEER_HEADS = 8
PEER_NKEYS = 128
PEER_TOPK = 16
NEG = -1e30
F32_LOG_TINY = -104.0
VMEM_LIMIT = 56 * 1024 * 1024


def _cparams(sem):
    return pltpu.CompilerParams(dimension_semantics=sem, vmem_limit_bytes=VMEM_LIMIT)


def _dot(a, b):
    return jnp.dot(a.astype(BF16), b.astype(BF16), preferred_element_type=F32)


def _dot_nt(a, b):
    return lax.dot_general(a.astype(BF16), b.astype(BF16), (((1,), (1,)), ((), ())),
                           preferred_element_type=F32)


def _dot_tn(a, b):
    return lax.dot_general(a.astype(BF16), b.astype(BF16), (((0,), (0,)), ((), ())),
                           preferred_element_type=F32)


def _split2(a):
    hi = a.astype(BF16)
    lo = (a - hi.astype(F32)).astype(BF16)
    return hi, lo


def _dot_exact_rhs(a, m_bf16):
    hi, lo = _split2(a)
    return (jnp.dot(hi, m_bf16, preferred_element_type=F32)
            + jnp.dot(lo, m_bf16, preferred_element_type=F32))


def _dot_exact_lhs(m_bf16, a):
    hi, lo = _split2(a)
    return (jnp.dot(m_bf16, hi, preferred_element_type=F32)
            + jnp.dot(m_bf16, lo, preferred_element_type=F32))


def _rms(x, g):
    return x * lax.rsqrt(jnp.mean(x * x, axis=-1, keepdims=True) + EPS) * g


def _softplus(z):
    return jnp.maximum(z, 0.0) + jnp.log1p(jnp.exp(-jnp.abs(z)))


def _head_mask(hh):
    lane = lax.broadcasted_iota(jnp.int32, (1, LANES), 1)
    return ((lane // HEAD_DIM) == hh).astype(F32)


def _norm_proj_kernel(x_ref, g_ref, w_ref, o_ref):
    xn = _rms(x_ref[...], g_ref[...])
    o_ref[...] = jnp.dot(xn.astype(BF16), w_ref[...], preferred_element_type=F32)


def norm_proj(x2d, g, w_bf16, tm=256):
    T, D = x2d.shape
    N = w_bf16.shape[1]
    return pl.pallas_call(
        _norm_proj_kernel,
        grid=(T // tm,),
        in_specs=[pl.BlockSpec((tm, D), lambda i: (i, 0)),
                  pl.BlockSpec((1, D), lambda i: (0, 0)),
                  pl.BlockSpec((D, N), lambda i: (0, 0))],
        out_specs=pl.BlockSpec((tm, N), lambda i: (i, 0)),
        out_shape=jax.ShapeDtypeStruct((T, N), F32),
        compiler_params=_cparams(("parallel",)),
        name="norm_proj",
    )(x2d, g.reshape(1, D), w_bf16)


def _out_proj_kernel(x_ref, a_ref, b_ref, w_ref, o_ref):
    ka = a_ref.shape[1]
    o_ref[...] = (x_ref[...] + _dot(a_ref[...], w_ref[0:ka, :])
                  + _dot(b_ref[...], w_ref[ka:, :]))


def out_proj(x2d, a2d, b2d, w_bf16, tm=512):
    T, D = x2d.shape
    ka, kb = a2d.shape[1], b2d.shape[1]
    return pl.pallas_call(
        _out_proj_kernel,
        grid=(T // tm,),
        in_specs=[pl.BlockSpec((tm, D), lambda i: (i, 0)),
                  pl.BlockSpec((tm, ka), lambda i: (i, 0)),
                  pl.BlockSpec((tm, kb), lambda i: (i, 0)),
                  pl.BlockSpec((ka + kb, D), lambda i: (0, 0))],
        out_specs=pl.BlockSpec((tm, D), lambda i: (i, 0)),
        out_shape=jax.ShapeDtypeStruct((T, D), F32),
        compiler_params=_cparams(("parallel",)),
        name="out_proj",
    )(x2d, a2d, b2d, w_bf16)


EV_Q, EV_K, EV_V, EV_R, EV_SQ, EV_SK, EV_SV, EV_GG = 0, 2, 4, 8, 12, 16, 20, 24
EV_COLS = 25 * LANES


def _gla_kernel(q_ref, k_ref, v_ref, r_ref, gg_ref, gup_ref, gb_ref, onorm_ref, o_ref, state_ref,
                *, ts):
    C = GLA_CHUNK

    @pl.when(pl.program_id(1) == 0)
    def _():
        state_ref[...] = jnp.zeros_like(state_ref)

    row = lax.broadcasted_iota(jnp.int32, (C, C), 0)
    col = lax.broadcasted_iota(jnp.int32, (C, C), 1)
    causal = row >= col
    tri = causal.astype(BF16)
    masks = [_head_mask(0), _head_mask(1)]
    scale = HEAD_DIM ** -0.5
    onorm = onorm_ref[...]

    for ci in range(ts // C):
        sl = slice(ci * C, (ci + 1) * C)
        pre = _dot(gg_ref[0, sl, :], gup_ref[...]) + gb_ref[...]
        g = (jnp.minimum(pre, 0.0) - jnp.log1p(jnp.exp(-jnp.abs(pre)))) * (1.0 / GLA_TAU)
        b = _dot_exact_lhs(tri, g)
        bmid = b[C // 2 - 1:C // 2, :]
        blast = b[C - 1:C, :]
        q = q_ref[0, sl, :] * scale
        k = k_ref[0, sl, :]
        qd = q * jnp.exp(b - bmid)
        kd = k * jnp.exp(bmid - b)
        qe = q * jnp.exp(b)
        kdec = k * jnp.exp(blast - b)
        eb_last = jnp.exp(blast)
        for p in range(GLA_HEADS // 2):
            lanes = slice(p * LANES, (p + 1) * LANES)
            st = state_ref[p]
            new_st = st * eb_last[:, lanes]
            for hh in range(2):
                h = 2 * p + hh
                m = masks[hh]
                hl = slice(h * GLA_DV, (h + 1) * GLA_DV)
                a = _dot_nt(qd[:, lanes] * m, kd[:, lanes])
                a = jnp.where(causal, a, 0.0)
                v_h = v_ref[0, sl, hl]
                o = _dot(a, v_h) + _dot_nt(qe[:, lanes] * m, st)
                o = _rms(o, onorm)
                r_h = r_ref[0, sl, hl]
                o_ref[0, sl, hl] = o * (r_h * jax.nn.sigmoid(r_h))
                new_st = new_st + _dot_tn(v_h, kdec[:, lanes] * m)
            state_ref[p] = new_st


def gla_mixer(proj3, gate_up_pad, gate_b, out_norm, ts=256):
    B, S, _ = proj3.shape
    nh = GLA_HEADS
    w = nh * GLA_DV

    def col(blk_w, tile):
        idx = tile * LANES // blk_w
        return pl.BlockSpec((1, ts, blk_w), lambda b, c: (b, c, idx))

    return pl.pallas_call(
        functools.partial(_gla_kernel, ts=ts),
        grid=(B, S // ts),
        in_specs=[col(nh * HEAD_DIM, EV_Q), col(nh * HEAD_DIM, EV_K), col(w, EV_V), col(w, EV_R),
                  col(LANES, EV_GG),
                  pl.BlockSpec((LANES, nh * HEAD_DIM), lambda b, c: (0, 0)),
                  pl.BlockSpec((1, nh * HEAD_DIM), lambda b, c: (0, 0)),
                  pl.BlockSpec((1, GLA_DV), lambda b, c: (0, 0))],
        out_specs=pl.BlockSpec((1, ts, w), lambda b, c: (b, c, 0)),
        out_shape=jax.ShapeDtypeStruct((B, S, w), F32),
        scratch_shapes=[pltpu.VMEM((nh // 2, GLA_DV, LANES), F32)],
        compiler_params=_cparams(("parallel", "arbitrary")),
        name="gla",
    )(proj3, proj3, proj3, proj3, proj3, gate_up_pad, gate_b.reshape(1, -1), out_norm.reshape(1, -1))


def _sb_kernel(q_ref, k_ref, v_ref, o_ref, acc_ref, carry_ref, *, tq):
    i = pl.program_id(2)
    q = q_ref[0] * (HEAD_DIM ** -0.5)
    masks = [_head_mask(0), _head_mask(1)]
    qh = [(q * m).astype(BF16) for m in masks]
    row = lax.broadcasted_iota(jnp.int32, (tq, tq), 0)
    col = lax.broadcasted_iota(jnp.int32, (tq, tq), 1)
    upper = (row > col).astype(BF16)
    past = col < row
    acc_ref[...] = jnp.zeros_like(acc_ref)
    carry_ref[...] = jnp.zeros_like(carry_ref)

    def tile(j, diag):
        off = pl.multiple_of(j * tq, tq)
        k = k_ref[0, pl.ds(off, tq), :].astype(BF16)
        v = v_ref[0, pl.ds(off, tq), :].astype(BF16)
        for hh in range(2):
            z = lax.dot_general(qh[hh], k, (((1,), (1,)), ((), ())), preferred_element_type=F32)
            sp = _softplus(z)
            log_rem = jnp.where(past, -sp, 0.0) if diag else -sp
            carry = carry_ref[hh]
            after = _dot_exact_rhs(log_rem, upper) + carry
            w = jnp.exp((z - sp) + after)
            if diag:
                w = jnp.where(past, w, 0.0)
            acc_ref[hh] += jnp.dot(w.astype(BF16), v, preferred_element_type=F32)
            carry_ref[hh] = carry + jnp.sum(log_rem, axis=1, keepdims=True)

    tile(i, True)

    def live(c):
        j, worst = c
        return jnp.logical_and(j >= 0, worst > F32_LOG_TINY)

    def body(c):
        j, _ = c
        tile(j, False)
        return j - 1, jnp.max(carry_ref[...])

    lax.while_loop(live, body, (i - 1, jnp.max(carry_ref[...])))
    o_ref[0] = acc_ref[0] * masks[0] + acc_ref[1] * masks[1]


def sb_attention(proj3, tq=256):
    B, S, _ = proj3.shape
    npairs = 4
    return pl.pallas_call(
        functools.partial(_sb_kernel, tq=tq),
        grid=(B, npairs, S // tq),
        in_specs=[pl.BlockSpec((1, tq, LANES), lambda b, p, i: (b, i, EV_SQ + p)),
                  pl.BlockSpec((1, S, LANES), lambda b, p, i: (b, 0, EV_SK + p)),
                  pl.BlockSpec((1, S, LANES), lambda b, p, i: (b, 0, EV_SV + p))],
        out_specs=pl.BlockSpec((1, tq, LANES), lambda b, p, i: (b, i, p)),
        out_shape=jax.ShapeDtypeStruct((B, S, npairs * LANES), F32),
        scratch_shapes=[pltpu.VMEM((2, tq, LANES), F32), pltpu.VMEM((2, tq, 1), F32)],
        compiler_params=_cparams(("parallel", "parallel", "arbitrary")),
        name="sb_attention",
    )(proj3, proj3, proj3)


OD_Q, OD_K, OD_V, OD_RX, OD_RG = 0, 4, 8, 12, 16
OD_W = 4 * LANES


def _moba_prep_kernel(q_ref, k_ref, v_ref, pos_ref, invf_ref, qn_ref, kn_ref, bd_ref,
                      qt_ref, ko_ref, vt_ref, km_ref):
    ang = pos_ref[0].astype(F32) * invf_ref[...]
    cos1, sin1 = jnp.cos(ang), jnp.sin(ang)
    lane = lax.broadcasted_iota(jnp.int32, (1, LANES), 1)
    first_half = (lane % HEAD_DIM) < (HEAD_DIM // 2)
    bd = bd_ref[...]

    def norm_rope(x, gain):
        ms = _dot_exact_rhs(x * x, bd) * (1.0 / HEAD_DIM)
        xn = x * lax.rsqrt(ms + EPS) * gain
        outs = []
        for t in range(OD_W // LANES):
            xb = xn[:, t * LANES:(t + 1) * LANES]
            up = pltpu.roll(xb, LANES - HEAD_DIM // 2, axis=1)
            dn = pltpu.roll(xb, HEAD_DIM // 2, axis=1)
            outs.append(xb * cos1 + jnp.where(first_half, -up, dn) * sin1)
        return jnp.concatenate(outs, axis=1)

    qt_ref[0] = norm_rope(q_ref[0], qn_ref[...]).T.astype(BF16)
    kr = norm_rope(k_ref[0], kn_ref[...])
    ko_ref[0] = kr.astype(BF16)
    km_ref[0, 0] = jnp.mean(kr, axis=0, keepdims=True)
    vt_ref[0] = v_ref[0].T.astype(BF16)


def moba_prep(proj3, pos3, inv_freq_tile, qn_tile, kn_tile, blockdiag):
    B, S, _ = proj3.shape
    tb = MOBA_BLOCK
    nb = S // tb
    full = lambda shape: pl.BlockSpec(shape, lambda b, i: (0,) * len(shape))
    return pl.pallas_call(
        _moba_prep_kernel,
        grid=(B, nb),
        in_specs=[pl.BlockSpec((1, tb, OD_W), lambda b, i: (b, i, OD_Q // 4)),
                  pl.BlockSpec((1, tb, OD_W), lambda b, i: (b, i, OD_K // 4)),
                  pl.BlockSpec((1, tb, OD_W), lambda b, i: (b, i, OD_V // 4)),
                  pl.BlockSpec((1, tb, 1), lambda b, i: (b, i, 0)),
                  full((1, LANES)), full((1, OD_W)), full((1, OD_W)), full((OD_W, OD_W))],
        out_specs=[pl.BlockSpec((1, OD_W, tb), lambda b, i: (b, 0, i)),
                   pl.BlockSpec((1, tb, OD_W), lambda b, i: (b, i, 0)),
                   pl.BlockSpec((1, OD_W, tb), lambda b, i: (b, 0, i)),
                   pl.BlockSpec((1, 1, 1, OD_W), lambda b, i: (b, i, 0, 0))],
        out_shape=[jax.ShapeDtypeStruct((B, OD_W, S), BF16),
                   jax.ShapeDtypeStruct((B, S, OD_W), BF16),
                   jax.ShapeDtypeStruct((B, OD_W, S), BF16),
                   jax.ShapeDtypeStruct((B, nb, 1, OD_W), F32)],
        compiler_params=_cparams(("parallel", "parallel")),
        name="moba_prep",
    )(proj3, proj3, proj3, pos3, inv_freq_tile, qn_tile, kn_tile, blockdiag)


def _moba_kernel(qt_ref, k_ref, vt_ref, km_ref, o_ref, acc_ref, m_ref, sel_ref):
    tb = MOBA_BLOCK
    nbp = km_ref.shape[1]
    i = pl.program_id(2)
    qt = qt_ref[0]
    km = km_ref[0].astype(BF16)
    chan = lax.broadcasted_iota(jnp.int32, (LANES, 1), 0)
    in_head = [(chan // HEAD_DIM) == hh for hh in range(2)]
    blk = lax.broadcasted_iota(jnp.int32, (nbp, tb), 0).astype(F32)
    own = i.astype(F32)
    key = lax.broadcasted_iota(jnp.int32, (tb, tb), 0)
    qry = lax.broadcasted_iota(jnp.int32, (tb, tb), 1)
    visible = key <= qry
    off_own = pl.multiple_of(i * tb, tb)
    k_own = k_ref[0, pl.ds(off_own, tb), :]
    vt_own = vt_ref[0, :, pl.ds(off_own, tb)]
    zero = jnp.zeros_like(qt)
    one = jnp.ones_like(qt)
    qs = []
    for hh in range(2):
        qm = jnp.where(in_head[hh], qt, zero)
        gate = jnp.dot(km, qm, preferred_element_type=F32)
        g = jnp.where(blk < own, gate, -jnp.inf)
        sel = jnp.zeros((nbp, tb), F32)
        for r in range(MOBA_TOPK):
            mx = jnp.max(g, axis=0, keepdims=True)
            idx = jnp.min(jnp.where(g == mx, blk, float(nbp)), axis=0, keepdims=True)
            hit = blk == idx
            keep = jnp.where(i > r, 1.0, 0.0)
            sel = sel + jnp.where(hit, keep, 0.0)
            g = jnp.where(hit, -jnp.inf, g)
        sel_ref[hh] = sel
        qb = qm * (HEAD_DIM ** -0.5)
        qs.append(qb)
        s = jnp.dot(k_own, qb, preferred_element_type=F32)
        s = jnp.where(visible, s, NEG)
        m0 = jnp.max(s, axis=0, keepdims=True)
        p = jnp.exp(s - m0)
        m_ref[hh] = m0
        acc_ref[hh] = jnp.dot(jnp.where(in_head[hh], vt_own, one), p.astype(BF16),
                              preferred_element_type=F32)

    def past_blocks(n, nblk):
        off = pl.multiple_of(n * tb, tb)
        k = k_ref[0, pl.ds(off, nblk * tb), :]
        vt = vt_ref[0, :, pl.ds(off, nblk * tb)]
        scores = [jnp.dot(k, qs[hh], preferred_element_type=F32) for hh in range(2)]
        probs, alphas = [], []
        for hh in range(2):
            parts = []
            for j in range(nblk):
                chosen = sel_ref[hh, pl.ds(n + j, 1), :]
                parts.append(jnp.where(chosen > 0.0, scores[hh][j * tb:(j + 1) * tb, :], NEG))
            m_old = m_ref[hh]
            m_new = m_old
            for part in parts:
                m_new = jnp.maximum(m_new, jnp.max(part, axis=0, keepdims=True))
            alphas.append(jnp.exp(m_old - m_new))
            probs.append(jnp.concatenate([jnp.exp(part - m_new).astype(BF16) for part in parts], axis=0))
            m_ref[hh] = m_new
        ones_v = jnp.ones_like(vt)
        pv = [jnp.dot(jnp.where(in_head[hh], vt, ones_v), probs[hh], preferred_element_type=F32)
              for hh in range(2)]
        for hh in range(2):
            acc_ref[hh] = acc_ref[hh] * alphas[hh] + pv[hh]

    def four_blocks(t, c):
        past_blocks(4 * t, 4)
        return c

    lax.fori_loop(0, i // 4, four_blocks, 0)
    rem = i % 4

    @pl.when(rem >= 2)
    def _():
        past_blocks(i - rem, 2)

    @pl.when(rem % 2 == 1)
    def _():
        past_blocks(i - 1, 1)

    a0, a1 = acc_ref[0], acc_ref[1]
    den0 = a0[HEAD_DIM:HEAD_DIM + 1, :]
    den1 = a1[0:1, :]
    o_ref[0] = jnp.where(in_head[0], a0 / den0, a1 / den1).T


def moba_attention(q_t, k_rot, v_t, kmean):
    B, S, _ = k_rot.shape
    tb = MOBA_BLOCK
    nb = S // tb
    npairs = OD_W // LANES
    nbp = -(-nb // LANES) * LANES
    kmean = jnp.pad(kmean, ((0, 0), (0, nbp - nb), (0, 0)))
    return pl.pallas_call(
        _moba_kernel,
        grid=(B, npairs, nb),
        in_specs=[pl.BlockSpec((1, LANES, tb), lambda b, p, i: (b, p, i)),
                  pl.BlockSpec((1, S, LANES), lambda b, p, i: (b, 0, p)),
                  pl.BlockSpec((1, LANES, S), lambda b, p, i: (b, p, 0)),
                  pl.BlockSpec((1, nbp, LANES), lambda b, p, i: (b, 0, p))],
        out_specs=pl.BlockSpec((1, tb, LANES), lambda b, p, i: (b, i, p)),
        out_shape=jax.ShapeDtypeStruct((B, S, OD_W), F32),
        scratch_shapes=[pltpu.VMEM((2, LANES, tb), F32), pltpu.VMEM((2, 1, tb), F32),
                        pltpu.VMEM((2, nbp, tb), F32)],
        compiler_params=_cparams(("parallel", "parallel", "arbitrary")),
        name="moba_attention",
    )(q_t, k_rot, v_t, kmean)


def _rglru_kernel(x_ref, gate_ref, cw_ref, cb_ref, wa_ref, ba_ref, wx_ref, bx_ref, lam_ref, o_ref,
                  buf_ref, xprev_ref, hprev_ref, *, ts):
    @pl.when(pl.program_id(1) == 0)
    def _():
        xprev_ref[...] = jnp.zeros_like(xprev_ref)
        hprev_ref[...] = jnp.zeros_like(hprev_ref)

    x = x_ref[0]
    buf_ref[0:8, :] = xprev_ref[...]
    buf_ref[8:8 + ts, :] = x
    xprev_ref[...] = x[ts - 8:ts, :]
    xc = cb_ref[...]
    for kk in range(RG_CONV):
        start = 8 - (RG_CONV - 1) + kk
        xc = xc + cw_ref[kk:kk + 1, :] * buf_ref[start:start + ts, :]
    r = jax.nn.sigmoid(_dot(xc, wa_ref[...]) + ba_ref[...])
    ig = jax.nn.sigmoid(_dot(xc, wx_ref[...]) + bx_ref[...])
    log_a = (-RG_C) * r * _softplus(-lam_ref[...])
    a = jnp.exp(log_a)
    u = jnp.sqrt(-jnp.tanh(log_a) * (a * a + 1.0)) * (ig * xc)
    t_idx = lax.broadcasted_iota(jnp.int32, (ts, 1), 0)
    d = 1
    while d < ts:
        valid = t_idx >= d
        a_sh = pltpu.roll(a, d, axis=0)
        u_sh = pltpu.roll(u, d, axis=0)
        u = jnp.where(valid, a * u_sh, 0.0) + u
        a = jnp.where(valid, a * a_sh, a)
        d *= 2
    h = a * hprev_ref[...] + u
    hprev_ref[...] = h[ts - 1:ts, :]
    o_ref[0] = h * jax.nn.gelu(gate_ref[0], approximate=True)


def rg_lru_mixer(proj3, conv_w, conv_b, wa_bd, ba, wx_bd, bx, lam, ts=256):
    B, S, _ = proj3.shape
    W = conv_w.shape[1]
    full = lambda shape: pl.BlockSpec(shape, lambda b, t: (0,) * len(shape))
    row = lambda v: v.reshape(1, W)
    return pl.pallas_call(
        functools.partial(_rglru_kernel, ts=ts),
        grid=(B, S // ts),
        in_specs=[pl.BlockSpec((1, ts, W), lambda b, t: (b, t, OD_RX // 4)),
                  pl.BlockSpec((1, ts, W), lambda b, t: (b, t, OD_RG // 4)),
                  full((RG_CONV, W)), full((1, W)), full((W, W)), full((1, W)), full((W, W)),
                  full((1, W)), full((1, W))],
        out_specs=pl.BlockSpec((1, ts, W), lambda b, t: (b, t, 0)),
        out_shape=jax.ShapeDtypeStruct((B, S, W), F32),
        scratch_shapes=[pltpu.VMEM((ts + 8, W), F32), pltpu.VMEM((8, W), F32), pltpu.VMEM((1, W), F32)],
        compiler_params=_cparams(("parallel", "arbitrary")),
        name="rg_lru",
    )(proj3, proj3, conv_w, row(conv_b), wa_bd, row(ba), wx_bd, row(bx), row(lam))


def _xattn_kernel(x_ref, kv_ref, g_ref, wq_ref, qn_ref, kn_ref, wo_ref, o_ref):
    x = x_ref[0]
    q = _dot(_rms(x, g_ref[...]), wq_ref[...])
    kv = kv_ref[0]
    hw = XA_HEADS * XA_DH
    outs = []
    for h in range(XA_HEADS):
        hl = slice(h * XA_DH, (h + 1) * XA_DH)
        qh = _rms(q[:, hl], qn_ref[...])
        kh = _rms(kv[:, hl], kn_ref[...])
        vh = kv[:, hw + h * XA_DH: hw + (h + 1) * XA_DH]
        s = _dot_nt(qh, kh) * (XA_DH ** -0.5)
        s = s - jnp.max(s, axis=1, keepdims=True)
        p = jnp.exp(s)
        p = p / jnp.sum(p, axis=1, keepdims=True)
        outs.append(_dot(p, vh))
    o_ref[0] = x + _dot(jnp.concatenate(outs, axis=1), wo_ref[...])


def cross_attention(x3, kv3, g, wq_bf16, qn, kn, wo_bf16, tq=256):
    B, S, D = x3.shape
    M = kv3.shape[1]
    hw = XA_HEADS * XA_DH
    full = lambda shape: pl.BlockSpec(shape, lambda b, i: (0,) * len(shape))
    return pl.pallas_call(
        _xattn_kernel,
        grid=(B, S // tq),
        in_specs=[pl.BlockSpec((1, tq, D), lambda b, i: (b, i, 0)),
                  pl.BlockSpec((1, M, 2 * hw), lambda b, i: (b, 0, 0)),
                  full((1, D)), full((D, hw)), full((1, XA_DH)), full((1, XA_DH)), full((hw, D))],
        out_specs=pl.BlockSpec((1, tq, D), lambda b, i: (b, i, 0)),
        out_shape=jax.ShapeDtypeStruct((B, S, D), F32),
        compiler_params=_cparams(("parallel", "parallel")),
        name="cross_attention",
    )(x3, kv3, g.reshape(1, D), wq_bf16, qn.reshape(1, -1), kn.reshape(1, -1), wo_bf16)


BF16_ROWS = 16


def _top_values(s, k, want_rank=False):
    rows = []
    rank = jnp.full(s.shape, float(k), F32) if want_rank else None
    for r in range(k):
        m = jnp.max(s, axis=0, keepdims=True)
        rows.append(m)
        hit = s == m
        if want_rank:
            rank = jnp.where(hit, float(r), rank)
        s = jnp.where(hit, -jnp.inf, s)
    return rows, rank


def _peer_kernel(x_ref, g_ref, wq_ref, sk_ref, u_ref, vt_ref, o_ref,
                 xnt_ref, c_ref, a_ref, r2_ref, e2_ref, acc_ref, *, ib, isub):
    e = pl.program_id(1)
    nk = PEER_NKEYS
    K = PEER_TOPK
    tq = x_ref.shape[0]

    @pl.when(e == 0)
    def _():
        xn32 = _rms(x_ref[...], g_ref[...])
        xn = xn32.astype(BF16)
        xnt_ref[...] = xn32.T.astype(BF16)
        acc_ref[...] = jnp.zeros_like(acc_ref)
        sub = lax.broadcasted_iota(jnp.int32, (8, 1), 0)

        def route(h, c):
            q = jnp.dot(xn, wq_ref[h], preferred_element_type=F32)
            s1 = _dot_nt(sk_ref[2 * h], q[:, :nk])
            s2 = _dot_nt(sk_ref[2 * h + 1], q[:, nk:])
            v1, _ = _top_values(s1, K)
            v2, rank2 = _top_values(s2, K, want_rank=True)
            v1m = jnp.concatenate(v1, axis=0)
            v2m = jnp.concatenate(v2, axis=0)
            groups = [v1[0] + v2m]
            for a in range(1, 8):
                groups.append(jnp.where(sub < K // (a + 1), v1[a] + v2m[0:8], -jnp.inf))
            groups.append(v1m[8:16] + v2[0])
            top, _ = _top_values(jnp.concatenate(groups, axis=0), K)
            tau = top[K - 1]
            z = jnp.zeros_like(tau)
            for tv in top:
                z = z + jnp.exp(tv - top[0])
            count = jnp.zeros(s1.shape, F32)
            for bb in range(K):
                count = count + jnp.where(s1 + v2[bb] >= tau, 1.0, 0.0)
            c_ref[h] = count
            a_ref[h] = jnp.exp(s1 - v1[0]) / z
            r2_ref[h] = rank2.astype(BF16)
            e2_ref[h] = jnp.exp(s2 - v2[0]).astype(BF16)
            return c

        lax.fori_loop(0, PEER_HEADS, route, 0)

    ngrp = nk // BF16_ROWS
    zero = jnp.zeros((BF16_ROWS, tq), BF16)
    xnt = xnt_ref[...]
    first = pl.multiple_of(e * ib, ib)
    step_c = [c_ref[h, pl.ds(first, ib), :] for h in range(PEER_HEADS)]
    step_a = [a_ref[h, pl.ds(first, ib), :] for h in range(PEER_HEADS)]
    total = None
    for sc in range(ib // isub):
        lo = sc * isub * nk
        u_sub = pltpu.bitcast(u_ref[lo // 2:(lo + isub * nk) // 2, :], BF16)
        hid_sub = jnp.dot(u_sub, xnt, preferred_element_type=F32)
        pieces = []
        for ii in range(isub):
            il = sc * isub + ii
            hid = hid_sub[ii * nk:(ii + 1) * nk, :]
            act = (0.5 * hid * (1.0 + lax.erf(hid * np.float32(np.sqrt(0.5))))).astype(BF16)
            w = [zero] * ngrp
            for h in range(PEER_HEADS):
                cnt = jnp.broadcast_to(step_c[h][il:il + 1, :], (BF16_ROWS, tq)).astype(BF16)
                amp = jnp.broadcast_to(step_a[h][il:il + 1, :], (BF16_ROWS, tq)).astype(BF16)
                for gi in range(ngrp):
                    rows = slice(gi * BF16_ROWS, (gi + 1) * BF16_ROWS)
                    w[gi] = w[gi] + jnp.where(r2_ref[h, rows, :] < cnt, e2_ref[h, rows, :] * amp, zero)
            pieces += [w[gi] * act[gi * BF16_ROWS:(gi + 1) * BF16_ROWS, :] for gi in range(ngrp)]
        vt_sub = pltpu.bitcast(vt_ref[:, lo:lo + isub * nk], BF16)
        part = jnp.dot(vt_sub, jnp.concatenate(pieces, axis=0), preferred_element_type=F32)
        total = part if total is None else total + part
    acc_ref[...] += total

    @pl.when(e == pl.num_programs(1) - 1)
    def _():
        o_ref[...] = x_ref[...] + acc_ref[...].T


def _pair_rows(w_bf16):
    R, C = w_bf16.shape
    return lax.bitcast_convert_type(w_bf16.reshape(R // 2, 2, C).transpose(0, 2, 1), jnp.uint32)


def peer_ffn(x2d, g, wq3_bf16, sk_bf16, u_pairs, vt_pairs, tq=256, ib=16, isub=4):
    T, D = x2d.shape
    ne = PEER_NKEYS // ib
    ec = ib * PEER_NKEYS
    H = PEER_HEADS
    full = lambda shape: pl.BlockSpec(shape, lambda i, e: (0,) * len(shape))
    words = pltpu.VMEM((H, PEER_NKEYS, tq), F32)
    halves = pltpu.VMEM((H, PEER_NKEYS, tq), BF16)
    return pl.pallas_call(
        functools.partial(_peer_kernel, ib=ib, isub=isub),
        grid=(T // tq, ne),
        in_specs=[pl.BlockSpec((tq, D), lambda i, e: (i, 0)),
                  full((1, D)), full((H, D, 2 * PEER_NKEYS)), full((2 * H, PEER_NKEYS, PEER_NKEYS)),
                  pl.BlockSpec((ec // 2, D), lambda i, e: (e, 0)),
                  pl.BlockSpec((D // 2, ec), lambda i, e: (0, e))],
        out_specs=pl.BlockSpec((tq, D), lambda i, e: (i, 0)),
        out_shape=jax.ShapeDtypeStruct((T, D), F32),
        scratch_shapes=[pltpu.VMEM((D, tq), BF16), words, words, halves, halves,
                        pltpu.VMEM((D, tq), F32)],
        compiler_params=_cparams(("parallel", "arbitrary")),
        name="peer_ffn",
    )(x2d, g.reshape(1, D), wq3_bf16, sk_bf16, u_pairs, vt_pairs)


def _block_diag(w):
    G, n, _ = w.shape
    eye = jnp.eye(G, dtype=w.dtype)
    return (eye[:, None, :, None] * w[:, :, None, :]).reshape(G * n, G * n)


def even_layer(x2d, B, S, norm, w_in, gate_up, gate_b, out_norm, w_out):
    D = x2d.shape[1]
    gg0 = EV_GG * LANES
    wp = jnp.concatenate([w_in[:, :1536], w_in[:, 1536 + GLA_RANK:], w_in[:, 1536:1536 + GLA_RANK],
                          jnp.zeros((D, EV_COLS - gg0 - GLA_RANK), w_in.dtype)], axis=1).astype(BF16)
    proj = norm_proj(x2d, norm, wp).reshape(B, S, EV_COLS)
    gup = jnp.concatenate([gate_up, jnp.zeros((LANES - GLA_RANK, gate_up.shape[1]), gate_up.dtype)], axis=0)
    a_out = gla_mixer(proj, gup.astype(BF16), gate_b, out_norm)
    b_out = sb_attention(proj)
    return out_proj(x2d, a_out.reshape(B * S, -1), b_out.reshape(B * S, -1), w_out.astype(BF16))


def odd_layer(x2d, B, S, positions, norm, w_in, q_norm, k_norm, conv_w, conv_b, wa, ba, wx, bx, lam, w_out):
    proj = norm_proj(x2d, norm, w_in.astype(BF16)).reshape(B, S, -1)
    half = HEAD_DIM // 2
    inv_freq = ROPE_THETA ** (-jnp.arange(0, HEAD_DIM, 2, dtype=F32) / HEAD_DIM)
    inv_tile = jnp.tile(inv_freq, LANES // half).reshape(1, LANES)
    seg = np.arange(OD_W) // HEAD_DIM
    blockdiag = jnp.asarray(seg[:, None] == seg[None, :], dtype=BF16)
    q_t, k_rot, v_t, kmean = moba_prep(proj, positions.reshape(B, S, 1), inv_tile,
                                       jnp.tile(q_norm, OD_W // HEAD_DIM).reshape(1, OD_W),
                                       jnp.tile(k_norm, OD_W // HEAD_DIM).reshape(1, OD_W), blockdiag)
    c_out = moba_attention(q_t, k_rot, v_t, kmean.reshape(B, S // MOBA_BLOCK, OD_W))
    d_out = rg_lru_mixer(proj, conv_w, conv_b, _block_diag(wa).astype(BF16), ba,
                         _block_diag(wx).astype(BF16), bx, lam)
    return out_proj(x2d, c_out.reshape(B * S, -1), d_out.reshape(B * S, -1), w_out.astype(BF16))


def kernel(x, mem, positions, ev_norm, ev_w_in, ev_gla_gate_up, ev_gla_gate_b, ev_gla_out_norm, ev_w_out, od_norm, od_w_in, od_q_norm, od_k_norm, od_conv_w, od_conv_b, od_gate_a_w, od_gate_a_b, od_gate_x_w, od_gate_x_b, od_lambda, od_w_out, xa_norm, xa_mem_norm, xa_wq, xa_wkv, xa_q_norm, xa_k_norm, xa_wo, ffn_norm, peer_wq, peer_subkeys, peer_u, peer_v):
    B, S, D = x.shape
    M = mem.shape[1]
    depth = xa_norm.shape[0]
    x2d = x.reshape(B * S, D)
    mem2d = mem.reshape(B * M, D)
    for l in range(depth):
        if l % 2 == 0:
            e = l // 2
            x2d = even_layer(x2d, B, S, ev_norm[e], ev_w_in[e], ev_gla_gate_up[e], ev_gla_gate_b[e],
                             ev_gla_out_norm[e], ev_w_out[e])
        else:
            o = l // 2
            x2d = odd_layer(x2d, B, S, positions, od_norm[o], od_w_in[o], od_q_norm[o], od_k_norm[o],
                            od_conv_w[o], od_conv_b[o], od_gate_a_w[o], od_gate_a_b[o],
                            od_gate_x_w[o], od_gate_x_b[o], od_lambda[o], od_w_out[o])
        kv = norm_proj(mem2d, xa_mem_norm[l], xa_wkv[l].astype(BF16))
        x2d = cross_attention(x2d.reshape(B, S, D), kv.reshape(B, M, -1), xa_norm[l],
                              xa_wq[l].astype(BF16), xa_q_norm[l], xa_k_norm[l],
                              xa_wo[l].astype(BF16)).reshape(B * S, D)
        H = PEER_HEADS
        wq3 = peer_wq[l].reshape(D, H, 2 * PEER_NKEYS).transpose(1, 0, 2).astype(BF16)
        sk = peer_subkeys[l].reshape(2 * H, PEER_NKEYS, -1).astype(BF16)
        x2d = peer_ffn(x2d, ffn_norm[l], wq3, sk, _pair_rows(peer_u[l].astype(BF16)),
                       _pair_rows(peer_v[l].T.astype(BF16)))
    return x2d.reshape(B, S, D)
```

```python
import functools

import jax
import jax.numpy as jnp
import numpy as np
from jax import lax
from jax.experimental import pallas as pl
from jax.experimental.pallas import tpu as pltpu

F32 = jnp.float32
BF16 = jnp.bfloat16

EPS = 1e-6
ROPE_THETA = 10000.0
LANES = 128
HEAD_DIM = 64
GLA_HEADS = 4
GLA_DV = 128
GLA_CHUNK = 64
GLA_TAU = 16.0
GLA_RANK = 16
MOBA_BLOCK = 256
MOBA_TOPK = 3
RG_C = 8.0
RG_CONV = 4
XA_HEADS = 4
XA_DH = 128
PEER_HEADS = 8
PEER_NKEYS = 128
PEER_TOPK = 16
NEG = -1e30
F32_LOG_TINY = -104.0
VMEM_LIMIT = 56 * 1024 * 1024


def _cparams(sem):
    return pltpu.CompilerParams(dimension_semantics=sem, vmem_limit_bytes=VMEM_LIMIT)


def _dot(a, b):
    return jnp.dot(a.astype(BF16), b.astype(BF16), preferred_element_type=F32)


def _dot_nt(a, b):
    return lax.dot_general(a.astype(BF16), b.astype(BF16), (((1,), (1,)), ((), ())),
                           preferred_element_type=F32)


def _dot_tn(a, b):
    return lax.dot_general(a.astype(BF16), b.astype(BF16), (((0,), (0,)), ((), ())),
                           preferred_element_type=F32)


def _split2(a):
    hi = a.astype(BF16)
    lo = (a - hi.astype(F32)).astype(BF16)
    return hi, lo


def _dot_exact_rhs(a, m_bf16):
    hi, lo = _split2(a)
    return (jnp.dot(hi, m_bf16, preferred_element_type=F32)
            + jnp.dot(lo, m_bf16, preferred_element_type=F32))


def _dot_exact_lhs(m_bf16, a):
    hi, lo = _split2(a)
    return (jnp.dot(m_bf16, hi, preferred_element_type=F32)
            + jnp.dot(m_bf16, lo, preferred_element_type=F32))


def _rms(x, g):
    return x * lax.rsqrt(jnp.mean(x * x, axis=-1, keepdims=True) + EPS) * g


def _softplus(z):
    return jnp.maximum(z, 0.0) + jnp.log1p(jnp.exp(-jnp.abs(z)))


def _head_mask(hh):
    lane = lax.broadcasted_iota(jnp.int32, (1, LANES), 1)
    return ((lane // HEAD_DIM) == hh).astype(F32)


def _norm_proj_kernel(x_ref, g_ref, w_ref, o_ref):
    xn = _rms(x_ref[...], g_ref[...])
    o_ref[...] = jnp.dot(xn.astype(BF16), w_ref[...], preferred_element_type=F32)


def norm_proj(x2d, g, w_bf16, tm=256):
    T, D = x2d.shape
    N = w_bf16.shape[1]
    return pl.pallas_call(
        _norm_proj_kernel,
        grid=(T // tm,),
        in_specs=[pl.BlockSpec((tm, D), lambda i: (i, 0)),
                  pl.BlockSpec((1, D), lambda i: (0, 0)),
                  pl.BlockSpec((D, N), lambda i: (0, 0))],
        out_specs=pl.BlockSpec((tm, N), lambda i: (i, 0)),
        out_shape=jax.ShapeDtypeStruct((T, N), F32),
        compiler_params=_cparams(("parallel",)),
        name="norm_proj",
    )(x2d, g.reshape(1, D), w_bf16)


def _out_proj_kernel(x_ref, a_ref, b_ref, w_ref, o_ref):
    ka = a_ref.shape[1]
    o_ref[...] = (x_ref[...] + _dot(a_ref[...], w_ref[0:ka, :])
                  + _dot(b_ref[...], w_ref[ka:, :]))


def out_proj(x2d, a2d, b2d, w_bf16, tm=512):
    T, D = x2d.shape
    ka, kb = a2d.shape[1], b2d.shape[1]
    return pl.pallas_call(
        _out_proj_kernel,
        grid=(T // tm,),
        in_specs=[pl.BlockSpec((tm, D), lambda i: (i, 0)),
                  pl.BlockSpec((tm, ka), lambda i: (i, 0)),
                  pl.BlockSpec((tm, kb), lambda i: (i, 0)),
                  pl.BlockSpec((ka + kb, D), lambda i: (0, 0))],
        out_specs=pl.BlockSpec((tm, D), lambda i: (i, 0)),
        out_shape=jax.ShapeDtypeStruct((T, D), F32),
        compiler_params=_cparams(("parallel",)),
        name="out_proj",
    )(x2d, a2d, b2d, w_bf16)


EV_Q, EV_K, EV_V, EV_R, EV_SQ, EV_SK, EV_SV, EV_GG = 0, 2, 4, 8, 12, 16, 20, 24
EV_COLS = 25 * LANES


def _gla_kernel(q_ref, k_ref, v_ref, r_ref, gg_ref, gup_ref, gb_ref, onorm_ref, o_ref, state_ref,
                *, ts):
    C = GLA_CHUNK

    @pl.when(pl.program_id(1) == 0)
    def _():
        state_ref[...] = jnp.zeros_like(state_ref)

    row = lax.broadcasted_iota(jnp.int32, (C, C), 0)
    col = lax.broadcasted_iota(jnp.int32, (C, C), 1)
    causal = row >= col
    tri = causal.astype(BF16)
    masks = [_head_mask(0), _head_mask(1)]
    scale = HEAD_DIM ** -0.5
    onorm = onorm_ref[...]

    for ci in range(ts // C):
        sl = slice(ci * C, (ci + 1) * C)
        pre = _dot(gg_ref[0, sl, :], gup_ref[...]) + gb_ref[...]
        g = (jnp.minimum(pre, 0.0) - jnp.log1p(jnp.exp(-jnp.abs(pre)))) * (1.0 / GLA_TAU)
        b = _dot_exact_lhs(tri, g)
        bmid = b[C // 2 - 1:C // 2, :]
        blast = b[C - 1:C, :]
        q = q_ref[0, sl, :] * scale
        k = k_ref[0, sl, :]
        qd = q * jnp.exp(b - bmid)
        kd = k * jnp.exp(bmid - b)
        qe = q * jnp.exp(b)
        kdec = k * jnp.exp(blast - b)
        eb_last = jnp.exp(blast)
        for p in range(GLA_HEADS // 2):
            lanes = slice(p * LANES, (p + 1) * LANES)
            st = state_ref[p]
            new_st = st * eb_last[:, lanes]
            for hh in range(2):
                h = 2 * p + hh
                m = masks[hh]
                hl = slice(h * GLA_DV, (h + 1) * GLA_DV)
                a = _dot_nt(qd[:, lanes] * m, kd[:, lanes])
                a = jnp.where(causal, a, 0.0)
                v_h = v_ref[0, sl, hl]
                o = _dot(a, v_h) + _dot_nt(qe[:, lanes] * m, st)
                o = _rms(o, onorm)
                r_h = r_ref[0, sl, hl]
                o_ref[0, sl, hl] = o * (r_h * jax.nn.sigmoid(r_h))
                new_st = new_st + _dot_tn(v_h, kdec[:, lanes] * m)
            state_ref[p] = new_st


def gla_mixer(proj3, gate_up_pad, gate_b, out_norm, ts=256):
    B, S, _ = proj3.shape
    nh = GLA_HEADS
    w = nh * GLA_DV

    def col(blk_w, tile):
        idx = tile * LANES // blk_w
        return pl.BlockSpec((1, ts, blk_w), lambda b, c: (b, c, idx))

    return pl.pallas_call(
        functools.partial(_gla_kernel, ts=ts),
        grid=(B, S // ts),
        in_specs=[col(nh * HEAD_DIM, EV_Q), col(nh * HEAD_DIM, EV_K), col(w, EV_V), col(w, EV_R),
                  col(LANES, EV_GG),
                  pl.BlockSpec((LANES, nh * HEAD_DIM), lambda b, c: (0, 0)),
                  pl.BlockSpec((1, nh * HEAD_DIM), lambda b, c: (0, 0)),
                  pl.BlockSpec((1, GLA_DV), lambda b, c: (0, 0))],
        out_specs=pl.BlockSpec((1, ts, w), lambda b, c: (b, c, 0)),
        out_shape=jax.ShapeDtypeStruct((B, S, w), F32),
        scratch_shapes=[pltpu.VMEM((nh // 2, GLA_DV, LANES), F32)],
        compiler_params=_cparams(("parallel", "arbitrary")),
        name="gla",
    )(proj3, proj3, proj3, proj3, proj3, gate_up_pad, gate_b.reshape(1, -1), out_norm.reshape(1, -1))


def _sb_kernel(q_ref, k_ref, v_ref, o_ref, acc_ref, carry_ref, *, tq):
    i = pl.program_id(2)
    q = q_ref[0] * (HEAD_DIM ** -0.5)
    masks = [_head_mask(0), _head_mask(1)]
    qh = [(q * m).astype(BF16) for m in masks]
    row = lax.broadcasted_iota(jnp.int32, (tq, tq), 0)
    col = lax.broadcasted_iota(jnp.int32, (tq, tq), 1)
    upper = (row > col).astype(BF16)
    past = col < row
    acc_ref[...] = jnp.zeros_like(acc_ref)
    carry_ref[...] = jnp.zeros_like(carry_ref)

    def tile(j, diag):
        off = pl.multiple_of(j * tq, tq)
        k = k_ref[0, pl.ds(off, tq), :].astype(BF16)
        v = v_ref[0, pl.ds(off, tq), :].astype(BF16)
        for hh in range(2):
            z = lax.dot_general(qh[hh], k, (((1,), (1,)), ((), ())), preferred_element_type=F32)
            sp = _softplus(z)
            log_rem = jnp.where(past, -sp, 0.0) if diag else -sp
            carry = carry_ref[hh]
            after = _dot_exact_rhs(log_rem, upper) + carry
            w = jnp.exp((z - sp) + after)
            if diag:
                w = jnp.where(past, w, 0.0)
            acc_ref[hh] += jnp.dot(w.astype(BF16), v, preferred_element_type=F32)
            carry_ref[hh] = carry + jnp.sum(log_rem, axis=1, keepdims=True)

    tile(i, True)

    def live(c):
        j, worst = c
        return jnp.logical_and(j >= 0, worst > F32_LOG_TINY)

    def body(c):
        j, _ = c
        tile(j, False)
        return j - 1, jnp.max(carry_ref[...])

    lax.while_loop(live, body, (i - 1, jnp.max(carry_ref[...])))
    o_ref[0] = acc_ref[0] * masks[0] + acc_ref[1] * masks[1]


def sb_attention(proj3, tq=256):
    B, S, _ = proj3.shape
    npairs = 4
    return pl.pallas_call(
        functools.partial(_sb_kernel, tq=tq),
        grid=(B, npairs, S // tq),
        in_specs=[pl.BlockSpec((1, tq, LANES), lambda b, p, i: (b, i, EV_SQ + p)),
                  pl.BlockSpec((1, S, LANES), lambda b, p, i: (b, 0, EV_SK + p)),
                  pl.BlockSpec((1, S, LANES), lambda b, p, i: (b, 0, EV_SV + p))],
        out_specs=pl.BlockSpec((1, tq, LANES), lambda b, p, i: (b, i, p)),
        out_shape=jax.ShapeDtypeStruct((B, S, npairs * LANES), F32),
        scratch_shapes=[pltpu.VMEM((2, tq, LANES), F32), pltpu.VMEM((2, tq, 1), F32)],
        compiler_params=_cparams(("parallel", "parallel", "arbitrary")),
        name="sb_attention",
    )(proj3, proj3, proj3)


OD_Q, OD_K, OD_V, OD_RX, OD_RG = 0, 4, 8, 12, 16
OD_W = 4 * LANES


def _moba_prep_kernel(q_ref, k_ref, v_ref, pos_ref, invf_ref, qn_ref, kn_ref, bd_ref,
                      qt_ref, ko_ref, vt_ref, km_ref):
    ang = pos_ref[0].astype(F32) * invf_ref[...]
    cos1, sin1 = jnp.cos(ang), jnp.sin(ang)
    lane = lax.broadcasted_iota(jnp.int32, (1, LANES), 1)
    first_half = (lane % HEAD_DIM) < (HEAD_DIM // 2)
    bd = bd_ref[...]

    def norm_rope(x, gain):
        ms = _dot_exact_rhs(x * x, bd) * (1.0 / HEAD_DIM)
        xn = x * lax.rsqrt(ms + EPS) * gain
        outs = []
        for t in range(OD_W // LANES):
            xb = xn[:, t * LANES:(t + 1) * LANES]
            up = pltpu.roll(xb, LANES - HEAD_DIM // 2, axis=1)
            dn = pltpu.roll(xb, HEAD_DIM // 2, axis=1)
            outs.append(xb * cos1 + jnp.where(first_half, -up, dn) * sin1)
        return jnp.concatenate(outs, axis=1)

    qt_ref[0] = norm_rope(q_ref[0], qn_ref[...]).T.astype(BF16)
    kr = norm_rope(k_ref[0], kn_ref[...])
    ko_ref[0] = kr.astype(BF16)
    km_ref[0, 0] = jnp.mean(kr, axis=0, keepdims=True)
    vt_ref[0] = v_ref[0].T.astype(BF16)


def moba_prep(proj3, pos3, inv_freq_tile, qn_tile, kn_tile, blockdiag):
    B, S, _ = proj3.shape
    tb = MOBA_BLOCK
    nb = S // tb
    full = lambda shape: pl.BlockSpec(shape, lambda b, i: (0,) * len(shape))
    return pl.pallas_call(
        _moba_prep_kernel,
        grid=(B, nb),
        in_specs=[pl.BlockSpec((1, tb, OD_W), lambda b, i: (b, i, OD_Q // 4)),
                  pl.BlockSpec((1, tb, OD_W), lambda b, i: (b, i, OD_K // 4)),
                  pl.BlockSpec((1, tb, OD_W), lambda b, i: (b, i, OD_V // 4)),
                  pl.BlockSpec((1, tb, 1), lambda b, i: (b, i, 0)),
                  full((1, LANES)), full((1, OD_W)), full((1, OD_W)), full((OD_W, OD_W))],
        out_specs=[pl.BlockSpec((1, OD_W, tb), lambda b, i: (b, 0, i)),
                   pl.BlockSpec((1, tb, OD_W), lambda b, i: (b, i, 0)),
                   pl.BlockSpec((1, OD_W, tb), lambda b, i: (b, 0, i)),
                   pl.BlockSpec((1, 1, 1, OD_W), lambda b, i: (b, i, 0, 0))],
        out_shape=[jax.ShapeDtypeStruct((B, OD_W, S), BF16),
                   jax.ShapeDtypeStruct((B, S, OD_W), BF16),
                   jax.ShapeDtypeStruct((B, OD_W, S), BF16),
                   jax.ShapeDtypeStruct((B, nb, 1, OD_W), F32)],
        compiler_params=_cparams(("parallel", "parallel")),
        name="moba_prep",
    )(proj3, proj3, proj3, pos3, inv_freq_tile, qn_tile, kn_tile, blockdiag)


def _moba_kernel(qt_ref, k_ref, vt_ref, km_ref, o_ref, acc_ref, m_ref, sel_ref):
    tb = MOBA_BLOCK
    nbp = km_ref.shape[1]
    i = pl.program_id(2)
    qt = qt_ref[0]
    km = km_ref[0].astype(BF16)
    chan = lax.broadcasted_iota(jnp.int32, (LANES, 1), 0)
    in_head = [(chan // HEAD_DIM) == hh for hh in range(2)]
    blk = lax.broadcasted_iota(jnp.int32, (nbp, tb), 0).astype(F32)
    own = i.astype(F32)
    key = lax.broadcasted_iota(jnp.int32, (tb, tb), 0)
    qry = lax.broadcasted_iota(jnp.int32, (tb, tb), 1)
    visible = key <= qry
    off_own = pl.multiple_of(i * tb, tb)
    k_own = k_ref[0, pl.ds(off_own, tb), :]
    vt_own = vt_ref[0, :, pl.ds(off_own, tb)]
    zero = jnp.zeros_like(qt)
    one = jnp.ones_like(qt)
    qs = []
    for hh in range(2):
        qm = jnp.where(in_head[hh], qt, zero)
        gate = jnp.dot(km, qm, preferred_element_type=F32)
        g = jnp.where(blk < own, gate, -jnp.inf)
        sel = jnp.zeros((nbp, tb), F32)
        for r in range(MOBA_TOPK):
            mx = jnp.max(g, axis=0, keepdims=True)
            idx = jnp.min(jnp.where(g == mx, blk, float(nbp)), axis=0, keepdims=True)
            hit = blk == idx
            keep = jnp.where(i > r, 1.0, 0.0)
            sel = sel + jnp.where(hit, keep, 0.0)
            g = jnp.where(hit, -jnp.inf, g)
        sel_ref[hh] = sel
        qb = qm * (HEAD_DIM ** -0.5)
        qs.append(qb)
        s = jnp.dot(k_own, qb, preferred_element_type=F32)
        s = jnp.where(visible, s, NEG)
        m0 = jnp.max(s, axis=0, keepdims=True)
        p = jnp.exp(s - m0)
        m_ref[hh] = m0
        acc_ref[hh] = jnp.dot(jnp.where(in_head[hh], vt_own, one), p.astype(BF16),
                              preferred_element_type=F32)

    def past_blocks(n, nblk):
        off = pl.multiple_of(n * tb, tb)
        k = k_ref[0, pl.ds(off, nblk * tb), :]
        vt = vt_ref[0, :, pl.ds(off, nblk * tb)]
        scores = [jnp.dot(k, qs[hh], preferred_element_type=F32) for hh in range(2)]
        probs, alphas = [], []
        for hh in range(2):
            parts = []
            for j in range(nblk):
                chosen = sel_ref[hh, pl.ds(n + j, 1), :]
                parts.append(jnp.where(chosen > 0.0, scores[hh][j * tb:(j + 1) * tb, :], NEG))
            m_old = m_ref[hh]
            m_new = m_old
            for part in parts:
                m_new = jnp.maximum(m_new, jnp.max(part, axis=0, keepdims=True))
            alphas.append(jnp.exp(m_old - m_new))
            probs.append(jnp.concatenate([jnp.exp(part - m_new).astype(BF16) for part in parts], axis=0))
            m_ref[hh] = m_new
        ones_v = jnp.ones_like(vt)
        pv = [jnp.dot(jnp.where(in_head[hh], vt, ones_v), probs[hh], preferred_element_type=F32)
              for hh in range(2)]
        for hh in range(2):
            acc_ref[hh] = acc_ref[hh] * alphas[hh] + pv[hh]

    def four_blocks(t, c):
        past_blocks(4 * t, 4)
        return c

    lax.fori_loop(0, i // 4, four_blocks, 0)
    rem = i % 4

    @pl.when(rem >= 2)
    def _():
        past_blocks(i - rem, 2)

    @pl.when(rem % 2 == 1)
    def _():
        past_blocks(i - 1, 1)

    a0, a1 = acc_ref[0], acc_ref[1]
    den0 = a0[HEAD_DIM:HEAD_DIM + 1, :]
    den1 = a1[0:1, :]
    o_ref[0] = jnp.where(in_head[0], a0 / den0, a1 / den1).T


def moba_attention(q_t, k_rot, v_t, kmean):
    B, S, _ = k_rot.shape
    tb = MOBA_BLOCK
    nb = S // tb
    npairs = OD_W // LANES
    nbp = -(-nb // LANES) * LANES
    kmean = jnp.pad(kmean, ((0, 0), (0, nbp - nb), (0, 0)))
    return pl.pallas_call(
        _moba_kernel,
        grid=(B, npairs, nb),
        in_specs=[pl.BlockSpec((1, LANES, tb), lambda b, p, i: (b, p, i)),
                  pl.BlockSpec((1, S, LANES), lambda b, p, i: (b, 0, p)),
                  pl.BlockSpec((1, LANES, S), lambda b, p, i: (b, p, 0)),
                  pl.BlockSpec((1, nbp, LANES), lambda b, p, i: (b, 0, p))],
        out_specs=pl.BlockSpec((1, tb, LANES), lambda b, p, i: (b, i, p)),
        out_shape=jax.ShapeDtypeStruct((B, S, OD_W), F32),
        scratch_shapes=[pltpu.VMEM((2, LANES, tb), F32), pltpu.VMEM((2, 1, tb), F32),
                        pltpu.VMEM((2, nbp, tb), F32)],
        compiler_params=_cparams(("parallel", "parallel", "arbitrary")),
        name="moba_attention",
    )(q_t, k_rot, v_t, kmean)


def _rglru_kernel(x_ref, gate_ref, cw_ref, cb_ref, wa_ref, ba_ref, wx_ref, bx_ref, lam_ref, o_ref,
                  buf_ref, xprev_ref, hprev_ref, *, ts):
    @pl.when(pl.program_id(1) == 0)
    def _():
        xprev_ref[...] = jnp.zeros_like(xprev_ref)
        hprev_ref[...] = jnp.zeros_like(hprev_ref)

    x = x_ref[0]
    buf_ref[0:8, :] = xprev_ref[...]
    buf_ref[8:8 + ts, :] = x
    xprev_ref[...] = x[ts - 8:ts, :]
    xc = cb_ref[...]
    for kk in range(RG_CONV):
        start = 8 - (RG_CONV - 1) + kk
        xc = xc + cw_ref[kk:kk + 1, :] * buf_ref[start:start + ts, :]
    r = jax.nn.sigmoid(_dot(xc, wa_ref[...]) + ba_ref[...])
    ig = jax.nn.sigmoid(_dot(xc, wx_ref[...]) + bx_ref[...])
    log_a = (-RG_C) * r * _softplus(-lam_ref[...])
    a = jnp.exp(log_a)
    u = jnp.sqrt(-jnp.tanh(log_a) * (a * a + 1.0)) * (ig * xc)
    t_idx = lax.broadcasted_iota(jnp.int32, (ts, 1), 0)
    d = 1
    while d < ts:
        valid = t_idx >= d
        a_sh = pltpu.roll(a, d, axis=0)
        u_sh = pltpu.roll(u, d, axis=0)
        u = jnp.where(valid, a * u_sh, 0.0) + u
        a = jnp.where(valid, a * a_sh, a)
        d *= 2
    h = a * hprev_ref[...] + u
    hprev_ref[...] = h[ts - 1:ts, :]
    o_ref[0] = h * jax.nn.gelu(gate_ref[0], approximate=True)


def rg_lru_mixer(proj3, conv_w, conv_b, wa_bd, ba, wx_bd, bx, lam, ts=256):
    B, S, _ = proj3.shape
    W = conv_w.shape[1]
    full = lambda shape: pl.BlockSpec(shape, lambda b, t: (0,) * len(shape))
    row = lambda v: v.reshape(1, W)
    return pl.pallas_call(
        functools.partial(_rglru_kernel, ts=ts),
        grid=(B, S // ts),
        in_specs=[pl.BlockSpec((1, ts, W), lambda b, t: (b, t, OD_RX // 4)),
                  pl.BlockSpec((1, ts, W), lambda b, t: (b, t, OD_RG // 4)),
                  full((RG_CONV, W)), full((1, W)), full((W, W)), full((1, W)), full((W, W)),
                  full((1, W)), full((1, W))],
        out_specs=pl.BlockSpec((1, ts, W), lambda b, t: (b, t, 0)),
        out_shape=jax.ShapeDtypeStruct((B, S, W), F32),
        scratch_shapes=[pltpu.VMEM((ts + 8, W), F32), pltpu.VMEM((8, W), F32), pltpu.VMEM((1, W), F32)],
        compiler_params=_cparams(("parallel", "arbitrary")),
        name="rg_lru",
    )(proj3, proj3, conv_w, row(conv_b), wa_bd, row(ba), wx_bd, row(bx), row(lam))


def _xattn_kernel(x_ref, kv_ref, g_ref, wq_ref, qn_ref, kn_ref, wo_ref, o_ref):
    x = x_ref[0]
    q = _dot(_rms(x, g_ref[...]), wq_ref[...])
    kv = kv_ref[0]
    hw = XA_HEADS * XA_DH
    outs = []
    for h in range(XA_HEADS):
        hl = slice(h * XA_DH, (h + 1) * XA_DH)
        qh = _rms(q[:, hl], qn_ref[...])
        kh = _rms(kv[:, hl], kn_ref[...])
        vh = kv[:, hw + h * XA_DH: hw + (h + 1) * XA_DH]
        s = _dot_nt(qh, kh) * (XA_DH ** -0.5)
        s = s - jnp.max(s, axis=1, keepdims=True)
        p = jnp.exp(s)
        p = p / jnp.sum(p, axis=1, keepdims=True)
        outs.append(_dot(p, vh))
    o_ref[0] = x + _dot(jnp.concatenate(outs, axis=1), wo_ref[...])


def cross_attention(x3, kv3, g, wq_bf16, qn, kn, wo_bf16, tq=256):
    B, S, D = x3.shape
    M = kv3.shape[1]
    hw = XA_HEADS * XA_DH
    full = lambda shape: pl.BlockSpec(shape, lambda b, i: (0,) * len(shape))
    return pl.pallas_call(
        _xattn_kernel,
        grid=(B, S // tq),
        in_specs=[pl.BlockSpec((1, tq, D), lambda b, i: (b, i, 0)),
                  pl.BlockSpec((1, M, 2 * hw), lambda b, i: (b, 0, 0)),
                  full((1, D)), full((D, hw)), full((1, XA_DH)), full((1, XA_DH)), full((hw, D))],
        out_specs=pl.BlockSpec((1, tq, D), lambda b, i: (b, i, 0)),
        out_shape=jax.ShapeDtypeStruct((B, S, D), F32),
        compiler_params=_cparams(("parallel", "parallel")),
        name="cross_attention",
    )(x3, kv3, g.reshape(1, D), wq_bf16, qn.reshape(1, -1), kn.reshape(1, -1), wo_bf16)


BF16_ROWS = 16


def _top_values(s, k, want_rank=False):
    rows = []
    rank = jnp.full(s.shape, float(k), F32) if want_rank else None
    for r in range(k):
        m = jnp.max(s, axis=0, keepdims=True)
        rows.append(m)
        hit = s == m
        if want_rank:
            rank = jnp.where(hit, float(r), rank)
        s = jnp.where(hit, -jnp.inf, s)
    return rows, rank


def _routing_stages(s1, s2, sub, finish, nstage):
    K = PEER_TOPK
    st = {"a1": s1, "a2": s2, "v1": [], "v2": [], "rank2": jnp.full(s2.shape, float(K), F32)}

    def extract(src, dst, rounds, want_rank):
        def run(nil):
            arr = st[src]
            gone = nil - jnp.inf
            for r in rounds:
                m = jnp.max(arr, axis=0, keepdims=True)
                st[dst].append(m)
                hit = arr == m
                if want_rank:
                    st["rank2"] = jnp.where(hit, float(r), st["rank2"])
                arr = jnp.where(hit, gone, arr)
            st[src] = arr
            st["last"] = m
        return run

    def candidates(nil):
        v1, v2 = st["v1"], st["v2"]
        v1m = jnp.concatenate(v1, axis=0)
        v2m = jnp.concatenate(v2, axis=0)
        groups = [(v1[0] + nil) + v2m]
        for a in range(1, 8):
            groups.append(jnp.where(sub < K // (a + 1), v1[a] + v2m[0:8], -jnp.inf))
        groups.append(v1m[8:16] + v2[0])
        st["cand"] = jnp.concatenate(groups, axis=0)
        st["top"] = []
        st["last"] = v2[0]

    def threshold(nil):
        top = st["top"]
        st["tau"] = top[K - 1]
        z = nil
        for tv in top:
            z = z + jnp.exp(tv - top[0])
        st["z"] = z
        st["last"] = z

    def counts(nil):
        v1, v2, tau = st["v1"], st["v2"], st["tau"]
        count = jnp.zeros(s1.shape, F32)
        for bb in range(K // 2):
            count = count + jnp.where(s1 + v2[bb] >= tau, 1.0, 0.0)
        best = nil
        for bb in range(K // 2, K):
            best = best + jnp.where(v1[0] + v2[bb] >= tau, 1.0, 0.0)
        count = count + jnp.where(s1 == v1[0], best, 0.0)
        finish(count, jnp.exp(s1 - v1[0]) / st["z"], st["rank2"].astype(BF16),
               jnp.exp(s2 - v2[0]).astype(BF16))
        st["last"] = best

    quarter = [range(q * 4, q * 4 + 4) for q in range(4)]
    work = ([extract("a1", "v1", r, False) for r in quarter]
            + [extract("a2", "v2", r, True) for r in quarter]
            + [candidates] + [extract("cand", "top", r, False) for r in quarter]
            + [threshold, counts])
    bounds = [round(j * len(work) / nstage) for j in range(nstage + 1)]

    def stage(j):
        def run(nil):
            for piece in work[bounds[j]:bounds[j + 1]]:
                piece(nil)
            return st["last"]
        return run

    return [stage(j) for j in range(nstage)]


def _peer_kernel(x_ref, xnext_ref, g_ref, wq_ref, sk_ref, u_ref, vt_ref, o_ref,
                 xn_ref, xnt_ref, c_ref, a_ref, r2_ref, e2_ref, acc_ref, *, ib, isub):
    t = pl.program_id(0)
    e = pl.program_id(1)
    nk = PEER_NKEYS
    tq = x_ref.shape[0]
    slot = t % 2
    nslot = 1 - slot
    sub = lax.broadcasted_iota(jnp.int32, (8, 1), 0)
    nsub = ib // isub

    def prepare(src_ref, s):
        xn32 = _rms(src_ref[...], g_ref[...])
        xn_ref[...] = xn32.astype(BF16)
        xnt_ref[s] = xn32.T.astype(BF16)

    def route_scores(h):
        q = jnp.dot(xn_ref[...], wq_ref[h], preferred_element_type=F32)
        return _dot_nt(sk_ref[2 * h], q[:, :nk]), _dot_nt(sk_ref[2 * h + 1], q[:, nk:])

    def table_writer(h, s):
        def finish(count, amp, rank2, e2):
            c_ref[s, h] = count
            a_ref[s, h] = amp
            r2_ref[s, h] = rank2
            e2_ref[s, h] = e2
        return finish

    @pl.when(jnp.logical_and(t == 0, e == 0))
    def _():
        prepare(x_ref, 0)

        def one(h, c):
            s1, s2 = route_scores(h)
            for run in _routing_stages(s1, s2, sub, table_writer(h, 0), 1):
                run(jnp.zeros((1, tq), F32))
            return c

        lax.fori_loop(0, PEER_HEADS, one, 0)

    @pl.when(e == 0)
    def _():
        acc_ref[...] = jnp.zeros_like(acc_ref)
        prepare(xnext_ref, nslot)

    ngrp = nk // BF16_ROWS
    xnt = xnt_ref[slot]
    first = pl.multiple_of(e * ib, ib)
    step_c = [c_ref[slot, h, pl.ds(first, ib), :] for h in range(PEER_HEADS)]
    step_a = [a_ref[slot, h, pl.ds(first, ib), :] for h in range(PEER_HEADS)]

    def expert_matmul(sc):
        lo = sc * isub * nk
        u_sub = pltpu.bitcast(u_ref[lo // 2:(lo + isub * nk) // 2, :], BF16)
        return jnp.dot(u_sub, xnt, preferred_element_type=F32)

    hid_next = expert_matmul(0)
    next_s1, next_s2 = route_scores(e)
    stages = _routing_stages(next_s1, next_s2, sub, table_writer(e, nslot), nsub)

    total = None
    after = jnp.zeros((1, tq), F32)
    for sc in range(nsub):
        zero = jnp.broadcast_to(after, (BF16_ROWS, tq)).astype(BF16)
        lo = sc * isub * nk
        hid_sub = hid_next
        if sc + 1 < nsub:
            hid_next = expert_matmul(sc + 1)
        pieces = []
        for ii in range(isub):
            il = sc * isub + ii
            hid = hid_sub[ii * nk:(ii + 1) * nk, :]
            act = (0.5 * hid * (1.0 + lax.erf(hid * np.float32(np.sqrt(0.5))))).astype(BF16)
            w = [zero] * ngrp
            for h in range(PEER_HEADS):
                cnt = jnp.broadcast_to(step_c[h][il:il + 1, :], (BF16_ROWS, tq)).astype(BF16)
                amp = jnp.broadcast_to(step_a[h][il:il + 1, :], (BF16_ROWS, tq)).astype(BF16)
                for gi in range(ngrp):
                    rows = slice(gi * BF16_ROWS, (gi + 1) * BF16_ROWS)
                    w[gi] = w[gi] + jnp.where(r2_ref[slot, h, rows, :] < cnt,
                                              e2_ref[slot, h, rows, :] * amp, zero)
            pieces += [w[gi] * act[gi * BF16_ROWS:(gi + 1) * BF16_ROWS, :] for gi in range(ngrp)]
        after = stages[sc](pieces[-1][0:1, :].astype(F32) * 0.0) * 0.0
        vt_sub = pltpu.bitcast(vt_ref[:, lo:lo + isub * nk], BF16)
        part = jnp.dot(vt_sub, jnp.concatenate(pieces, axis=0), preferred_element_type=F32)
        total = part if total is None else total + part
    acc_ref[...] += total

    @pl.when(e == pl.num_programs(1) - 1)
    def _():
        o_ref[...] = x_ref[...] + acc_ref[...].T


def _pair_rows(w_bf16):
    bits = lax.bitcast_convert_type(w_bf16, jnp.uint16).astype(jnp.uint32)
    return bits[0::2] | (bits[1::2] << 16)


def peer_ffn(x2d, g, wq3_bf16, sk_bf16, u_pairs, vt_pairs, tq=256, isub=2):
    T, D = x2d.shape
    H = PEER_HEADS
    ne = H
    ib = PEER_NKEYS // ne
    ec = ib * PEER_NKEYS
    nt = T // tq
    full = lambda shape: pl.BlockSpec(shape, lambda i, e: (0,) * len(shape))
    words = pltpu.VMEM((2, H, PEER_NKEYS, tq), F32)
    halves = pltpu.VMEM((2, H, PEER_NKEYS, tq), BF16)
    return pl.pallas_call(
        functools.partial(_peer_kernel, ib=ib, isub=isub),
        grid=(nt, ne),
        in_specs=[pl.BlockSpec((tq, D), lambda i, e: (i, 0)),
                  pl.BlockSpec((tq, D), lambda i, e: (jnp.minimum(i + 1, nt - 1), 0)),
                  full((1, D)), full((H, D, 2 * PEER_NKEYS)), full((2 * H, PEER_NKEYS, PEER_NKEYS)),
                  pl.BlockSpec((ec // 2, D), lambda i, e: (e, 0)),
                  pl.BlockSpec((D // 2, ec), lambda i, e: (0, e))],
        out_specs=pl.BlockSpec((tq, D), lambda i, e: (i, 0)),
        out_shape=jax.ShapeDtypeStruct((T, D), F32),
        scratch_shapes=[pltpu.VMEM((tq, D), BF16), pltpu.VMEM((2, D, tq), BF16),
                        words, words, halves, halves, pltpu.VMEM((D, tq), F32)],
        compiler_params=_cparams(("arbitrary", "arbitrary")),
        name="peer_ffn",
    )(x2d, x2d, g.reshape(1, D), wq3_bf16, sk_bf16, u_pairs, vt_pairs)


def _block_diag(w):
    G, n, _ = w.shape
    eye = jnp.eye(G, dtype=w.dtype)
    return (eye[:, None, :, None] * w[:, :, None, :]).reshape(G * n, G * n)


def even_layer(x2d, B, S, norm, w_in, gate_up, gate_b, out_norm, w_out):
    D = x2d.shape[1]
    gg0 = EV_GG * LANES
    wp = jnp.concatenate([w_in[:, :1536], w_in[:, 1536 + GLA_RANK:], w_in[:, 1536:1536 + GLA_RANK],
                          jnp.zeros((D, EV_COLS - gg0 - GLA_RANK), w_in.dtype)], axis=1).astype(BF16)
    proj = norm_proj(x2d, norm, wp).reshape(B, S, EV_COLS)
    gup = jnp.concatenate([gate_up, jnp.zeros((LANES - GLA_RANK, gate_up.shape[1]), gate_up.dtype)], axis=0)
    a_out = gla_mixer(proj, gup.astype(BF16), gate_b, out_norm)
    b_out = sb_attention(proj)
    return out_proj(x2d, a_out.reshape(B * S, -1), b_out.reshape(B * S, -1), w_out.astype(BF16))


def odd_layer(x2d, B, S, positions, norm, w_in, q_norm, k_norm, conv_w, conv_b, wa, ba, wx, bx, lam, w_out):
    proj = norm_proj(x2d, norm, w_in.astype(BF16)).reshape(B, S, -1)
    half = HEAD_DIM // 2
    inv_freq = ROPE_THETA ** (-jnp.arange(0, HEAD_DIM, 2, dtype=F32) / HEAD_DIM)
    inv_tile = jnp.tile(inv_freq, LANES // half).reshape(1, LANES)
    seg = np.arange(OD_W) // HEAD_DIM
    blockdiag = jnp.asarray(seg[:, None] == seg[None, :], dtype=BF16)
    q_t, k_rot, v_t, kmean = moba_prep(proj, positions.reshape(B, S, 1), inv_tile,
                                       jnp.tile(q_norm, OD_W // HEAD_DIM).reshape(1, OD_W),
                                       jnp.tile(k_norm, OD_W // HEAD_DIM).reshape(1, OD_W), blockdiag)
    c_out = moba_attention(q_t, k_rot, v_t, kmean.reshape(B, S // MOBA_BLOCK, OD_W))
    d_out = rg_lru_mixer(proj, conv_w, conv_b, _block_diag(wa).astype(BF16), ba,
                         _block_diag(wx).astype(BF16), bx, lam)
    return out_proj(x2d, c_out.reshape(B * S, -1), d_out.reshape(B * S, -1), w_out.astype(BF16))


def kernel(x, mem, positions, ev_norm, ev_w_in, ev_gla_gate_up, ev_gla_gate_b, ev_gla_out_norm, ev_w_out, od_norm, od_w_in, od_q_norm, od_k_norm, od_conv_w, od_conv_b, od_gate_a_w, od_gate_a_b, od_gate_x_w, od_gate_x_b, od_lambda, od_w_out, xa_norm, xa_mem_norm, xa_wq, xa_wkv, xa_q_norm, xa_k_norm, xa_wo, ffn_norm, peer_wq, peer_subkeys, peer_u, peer_v):
    B, S, D = x.shape
    M = mem.shape[1]
    depth = xa_norm.shape[0]
    x2d = x.reshape(B * S, D)
    mem2d = mem.reshape(B * M, D)
    for l in range(depth):
        if l % 2 == 0:
            e = l // 2
            x2d = even_layer(x2d, B, S, ev_norm[e], ev_w_in[e], ev_gla_gate_up[e], ev_gla_gate_b[e],
                             ev_gla_out_norm[e], ev_w_out[e])
        else:
            o = l // 2
            x2d = odd_layer(x2d, B, S, positions, od_norm[o], od_w_in[o], od_q_norm[o], od_k_norm[o],
                            od_conv_w[o], od_conv_b[o], od_gate_a_w[o], od_gate_a_b[o],
                            od_gate_x_w[o], od_gate_x_b[o], od_lambda[o], od_w_out[o])
        kv = norm_proj(mem2d, xa_mem_norm[l], xa_wkv[l].astype(BF16))
        x2d = cross_attention(x2d.reshape(B, S, D), kv.reshape(B, M, -1), xa_norm[l],
                              xa_wq[l].astype(BF16), xa_q_norm[l], xa_k_norm[l],
                              xa_wo[l].astype(BF16)).reshape(B * S, D)
        H = PEER_HEADS
        wq3 = peer_wq[l].reshape(D, H, 2 * PEER_NKEYS).transpose(1, 0, 2).astype(BF16)
        sk = peer_subkeys[l].reshape(2 * H, PEER_NKEYS, -1).astype(BF16)
        x2d = peer_ffn(x2d, ffn_norm[l], wq3, sk, _pair_rows(peer_u[l].astype(BF16)),
                       _pair_rows(peer_v[l].T.astype(BF16)))
    return x2d.reshape(B, S, D)
```

```python
import functools

import jax
import jax.numpy as jnp
import numpy as np
from jax import lax
from jax.experimental import pallas as pl
from jax.experimental.pallas import tpu as pltpu

F32 = jnp.float32
BF16 = jnp.bfloat16

EPS = 1e-6
ROPE_THETA = 10000.0
LANES = 128
HEAD_DIM = 64
GLA_HEADS = 4
GLA_DV = 128
GLA_CHUNK = 64
GLA_TAU = 16.0
GLA_RANK = 16
MOBA_BLOCK = 256
MOBA_TOPK = 3
RG_C = 8.0
RG_CONV = 4
XA_HEADS = 4
XA_DH = 128
PEER_HEADS = 8
PEER_NKEYS = 128
PEER_TOPK = 16
NEG = -1e30
F32_LOG_TINY = -104.0
VMEM_LIMIT = 56 * 1024 * 1024


def _cparams(sem):
    return pltpu.CompilerParams(dimension_semantics=sem, vmem_limit_bytes=VMEM_LIMIT)


def _dot(a, b):
    return jnp.dot(a.astype(BF16), b.astype(BF16), preferred_element_type=F32)


def _dot_nt(a, b):
    return lax.dot_general(a.astype(BF16), b.astype(BF16), (((1,), (1,)), ((), ())),
                           preferred_element_type=F32)


def _dot_tn(a, b):
    return lax.dot_general(a.astype(BF16), b.astype(BF16), (((0,), (0,)), ((), ())),
                           preferred_element_type=F32)


def _split2(a):
    hi = a.astype(BF16)
    lo = (a - hi.astype(F32)).astype(BF16)
    return hi, lo


def _dot_exact_rhs(a, m_bf16):
    hi, lo = _split2(a)
    return (jnp.dot(hi, m_bf16, preferred_element_type=F32)
            + jnp.dot(lo, m_bf16, preferred_element_type=F32))


def _dot_exact_lhs(m_bf16, a):
    hi, lo = _split2(a)
    return (jnp.dot(m_bf16, hi, preferred_element_type=F32)
            + jnp.dot(m_bf16, lo, preferred_element_type=F32))


def _rms(x, g):
    return x * lax.rsqrt(jnp.mean(x * x, axis=-1, keepdims=True) + EPS) * g


def _softplus(z):
    return jnp.maximum(z, 0.0) + jnp.log1p(jnp.exp(-jnp.abs(z)))


def _head_mask(hh):
    lane = lax.broadcasted_iota(jnp.int32, (1, LANES), 1)
    return ((lane // HEAD_DIM) == hh).astype(F32)


def _norm_proj_kernel(x_ref, g_ref, w_ref, o_ref):
    xn = _rms(x_ref[...], g_ref[...])
    o_ref[...] = jnp.dot(xn.astype(BF16), w_ref[...], preferred_element_type=F32)


def norm_proj(x2d, g, w_bf16, tm=256):
    T, D = x2d.shape
    N = w_bf16.shape[1]
    return pl.pallas_call(
        _norm_proj_kernel,
        grid=(T // tm,),
        in_specs=[pl.BlockSpec((tm, D), lambda i: (i, 0)),
                  pl.BlockSpec((1, D), lambda i: (0, 0)),
                  pl.BlockSpec((D, N), lambda i: (0, 0))],
        out_specs=pl.BlockSpec((tm, N), lambda i: (i, 0)),
        out_shape=jax.ShapeDtypeStruct((T, N), F32),
        compiler_params=_cparams(("parallel",)),
        name="norm_proj",
    )(x2d, g.reshape(1, D), w_bf16)


def _out_proj_kernel(x_ref, a_ref, b_ref, w_ref, o_ref):
    ka = a_ref.shape[1]
    o_ref[...] = (x_ref[...] + _dot(a_ref[...], w_ref[0:ka, :])
                  + _dot(b_ref[...], w_ref[ka:, :]))


def out_proj(x2d, a2d, b2d, w_bf16, tm=512):
    T, D = x2d.shape
    ka, kb = a2d.shape[1], b2d.shape[1]
    return pl.pallas_call(
        _out_proj_kernel,
        grid=(T // tm,),
        in_specs=[pl.BlockSpec((tm, D), lambda i: (i, 0)),
                  pl.BlockSpec((tm, ka), lambda i: (i, 0)),
                  pl.BlockSpec((tm, kb), lambda i: (i, 0)),
                  pl.BlockSpec((ka + kb, D), lambda i: (0, 0))],
        out_specs=pl.BlockSpec((tm, D), lambda i: (i, 0)),
        out_shape=jax.ShapeDtypeStruct((T, D), F32),
        compiler_params=_cparams(("parallel",)),
        name="out_proj",
    )(x2d, a2d, b2d, w_bf16)


EV_Q, EV_K, EV_V, EV_R, EV_SQ, EV_SK, EV_SV, EV_GG = 0, 2, 4, 8, 12, 16, 20, 24
EV_COLS = 25 * LANES


def _gla_kernel(q_ref, k_ref, v_ref, r_ref, gg_ref, gup_ref, gb_ref, onorm_ref, o_ref, state_ref,
                *, ts):
    C = GLA_CHUNK

    @pl.when(pl.program_id(1) == 0)
    def _():
        state_ref[...] = jnp.zeros_like(state_ref)

    row = lax.broadcasted_iota(jnp.int32, (C, C), 0)
    col = lax.broadcasted_iota(jnp.int32, (C, C), 1)
    causal = row >= col
    tri = causal.astype(BF16)
    masks = [_head_mask(0), _head_mask(1)]
    scale = HEAD_DIM ** -0.5
    onorm = onorm_ref[...]

    for ci in range(ts // C):
        sl = slice(ci * C, (ci + 1) * C)
        pre = _dot(gg_ref[0, sl, :], gup_ref[...]) + gb_ref[...]
        g = (jnp.minimum(pre, 0.0) - jnp.log1p(jnp.exp(-jnp.abs(pre)))) * (1.0 / GLA_TAU)
        b = _dot_exact_lhs(tri, g)
        bmid = b[C // 2 - 1:C // 2, :]
        blast = b[C - 1:C, :]
        q = q_ref[0, sl, :] * scale
        k = k_ref[0, sl, :]
        qd = q * jnp.exp(b - bmid)
        kd = k * jnp.exp(bmid - b)
        qe = q * jnp.exp(b)
        kdec = k * jnp.exp(blast - b)
        eb_last = jnp.exp(blast)
        for p in range(GLA_HEADS // 2):
            lanes = slice(p * LANES, (p + 1) * LANES)
            st = state_ref[p]
            new_st = st * eb_last[:, lanes]
            for hh in range(2):
                h = 2 * p + hh
                m = masks[hh]
                hl = slice(h * GLA_DV, (h + 1) * GLA_DV)
                a = _dot_nt(qd[:, lanes] * m, kd[:, lanes])
                a = jnp.where(causal, a, 0.0)
                v_h = v_ref[0, sl, hl]
                o = _dot(a, v_h) + _dot_nt(qe[:, lanes] * m, st)
                o = _rms(o, onorm)
                r_h = r_ref[0, sl, hl]
                o_ref[0, sl, hl] = o * (r_h * jax.nn.sigmoid(r_h))
                new_st = new_st + _dot_tn(v_h, kdec[:, lanes] * m)
            state_ref[p] = new_st


def gla_mixer(proj3, gate_up_pad, gate_b, out_norm, ts=256):
    B, S, _ = proj3.shape
    nh = GLA_HEADS
    w = nh * GLA_DV

    def col(blk_w, tile):
        idx = tile * LANES // blk_w
        return pl.BlockSpec((1, ts, blk_w), lambda b, c: (b, c, idx))

    return pl.pallas_call(
        functools.partial(_gla_kernel, ts=ts),
        grid=(B, S // ts),
        in_specs=[col(nh * HEAD_DIM, EV_Q), col(nh * HEAD_DIM, EV_K), col(w, EV_V), col(w, EV_R),
                  col(LANES, EV_GG),
                  pl.BlockSpec((LANES, nh * HEAD_DIM), lambda b, c: (0, 0)),
                  pl.BlockSpec((1, nh * HEAD_DIM), lambda b, c: (0, 0)),
                  pl.BlockSpec((1, GLA_DV), lambda b, c: (0, 0))],
        out_specs=pl.BlockSpec((1, ts, w), lambda b, c: (b, c, 0)),
        out_shape=jax.ShapeDtypeStruct((B, S, w), F32),
        scratch_shapes=[pltpu.VMEM((nh // 2, GLA_DV, LANES), F32)],
        compiler_params=_cparams(("parallel", "arbitrary")),
        name="gla",
    )(proj3, proj3, proj3, proj3, proj3, gate_up_pad, gate_b.reshape(1, -1), out_norm.reshape(1, -1))


def _sb_kernel(q_ref, k_ref, v_ref, o_ref, acc_ref, carry_ref, *, tq):
    i = pl.program_id(2)
    q = q_ref[0] * (HEAD_DIM ** -0.5)
    masks = [_head_mask(0), _head_mask(1)]
    qh = [(q * m).astype(BF16) for m in masks]
    row = lax.broadcasted_iota(jnp.int32, (tq, tq), 0)
    col = lax.broadcasted_iota(jnp.int32, (tq, tq), 1)
    upper = (row > col).astype(BF16)
    past = col < row
    acc_ref[...] = jnp.zeros_like(acc_ref)
    carry_ref[...] = jnp.zeros_like(carry_ref)

    def tile(j, diag):
        off = pl.multiple_of(j * tq, tq)
        k = k_ref[0, pl.ds(off, tq), :].astype(BF16)
        v = v_ref[0, pl.ds(off, tq), :].astype(BF16)
        for hh in range(2):
            z = lax.dot_general(qh[hh], k, (((1,), (1,)), ((), ())), preferred_element_type=F32)
            sp = _softplus(z)
            log_rem = jnp.where(past, -sp, 0.0) if diag else -sp
            carry = carry_ref[hh]
            after = _dot_exact_rhs(log_rem, upper) + carry
            w = jnp.exp((z - sp) + after)
            if diag:
                w = jnp.where(past, w, 0.0)
            acc_ref[hh] += jnp.dot(w.astype(BF16), v, preferred_element_type=F32)
            carry_ref[hh] = carry + jnp.sum(log_rem, axis=1, keepdims=True)

    tile(i, True)

    def live(c):
        j, worst = c
        return jnp.logical_and(j >= 0, worst > F32_LOG_TINY)

    def body(c):
        j, _ = c
        tile(j, False)
        return j - 1, jnp.max(carry_ref[...])

    lax.while_loop(live, body, (i - 1, jnp.max(carry_ref[...])))
    o_ref[0] = acc_ref[0] * masks[0] + acc_ref[1] * masks[1]


def sb_attention(proj3, tq=256):
    B, S, _ = proj3.shape
    npairs = 4
    return pl.pallas_call(
        functools.partial(_sb_kernel, tq=tq),
        grid=(B, npairs, S // tq),
        in_specs=[pl.BlockSpec((1, tq, LANES), lambda b, p, i: (b, i, EV_SQ + p)),
                  pl.BlockSpec((1, S, LANES), lambda b, p, i: (b, 0, EV_SK + p)),
                  pl.BlockSpec((1, S, LANES), lambda b, p, i: (b, 0, EV_SV + p))],
        out_specs=pl.BlockSpec((1, tq, LANES), lambda b, p, i: (b, i, p)),
        out_shape=jax.ShapeDtypeStruct((B, S, npairs * LANES), F32),
        scratch_shapes=[pltpu.VMEM((2, tq, LANES), F32), pltpu.VMEM((2, tq, 1), F32)],
        compiler_params=_cparams(("parallel", "parallel", "arbitrary")),
        name="sb_attention",
    )(proj3, proj3, proj3)


OD_Q, OD_K, OD_V, OD_RX, OD_RG = 0, 4, 8, 12, 16
OD_W = 4 * LANES


def _moba_prep_kernel(q_ref, k_ref, v_ref, pos_ref, invf_ref, qn_ref, kn_ref, bd_ref,
                      qt_ref, ko_ref, vt_ref, km_ref):
    ang = pos_ref[0].astype(F32) * invf_ref[...]
    cos1, sin1 = jnp.cos(ang), jnp.sin(ang)
    lane = lax.broadcasted_iota(jnp.int32, (1, LANES), 1)
    first_half = (lane % HEAD_DIM) < (HEAD_DIM // 2)
    bd = bd_ref[...]

    def norm_rope(x, gain):
        ms = _dot_exact_rhs(x * x, bd) * (1.0 / HEAD_DIM)
        xn = x * lax.rsqrt(ms + EPS) * gain
        outs = []
        for t in range(OD_W // LANES):
            xb = xn[:, t * LANES:(t + 1) * LANES]
            up = pltpu.roll(xb, LANES - HEAD_DIM // 2, axis=1)
            dn = pltpu.roll(xb, HEAD_DIM // 2, axis=1)
            outs.append(xb * cos1 + jnp.where(first_half, -up, dn) * sin1)
        return jnp.concatenate(outs, axis=1)

    qt_ref[0] = norm_rope(q_ref[0], qn_ref[...]).T.astype(BF16)
    kr = norm_rope(k_ref[0], kn_ref[...])
    ko_ref[0] = kr.astype(BF16)
    km_ref[0, 0] = jnp.mean(kr, axis=0, keepdims=True)
    vt_ref[0] = v_ref[0].T.astype(BF16)


def moba_prep(proj3, pos3, inv_freq_tile, qn_tile, kn_tile, blockdiag):
    B, S, _ = proj3.shape
    tb = MOBA_BLOCK
    nb = S // tb
    full = lambda shape: pl.BlockSpec(shape, lambda b, i: (0,) * len(shape))
    return pl.pallas_call(
        _moba_prep_kernel,
        grid=(B, nb),
        in_specs=[pl.BlockSpec((1, tb, OD_W), lambda b, i: (b, i, OD_Q // 4)),
                  pl.BlockSpec((1, tb, OD_W), lambda b, i: (b, i, OD_K // 4)),
                  pl.BlockSpec((1, tb, OD_W), lambda b, i: (b, i, OD_V // 4)),
                  pl.BlockSpec((1, tb, 1), lambda b, i: (b, i, 0)),
                  full((1, LANES)), full((1, OD_W)), full((1, OD_W)), full((OD_W, OD_W))],
        out_specs=[pl.BlockSpec((1, OD_W, tb), lambda b, i: (b, 0, i)),
                   pl.BlockSpec((1, tb, OD_W), lambda b, i: (b, i, 0)),
                   pl.BlockSpec((1, OD_W, tb), lambda b, i: (b, 0, i)),
                   pl.BlockSpec((1, 1, 1, OD_W), lambda b, i: (b, i, 0, 0))],
        out_shape=[jax.ShapeDtypeStruct((B, OD_W, S), BF16),
                   jax.ShapeDtypeStruct((B, S, OD_W), BF16),
                   jax.ShapeDtypeStruct((B, OD_W, S), BF16),
                   jax.ShapeDtypeStruct((B, nb, 1, OD_W), F32)],
        compiler_params=_cparams(("parallel", "parallel")),
        name="moba_prep",
    )(proj3, proj3, proj3, pos3, inv_freq_tile, qn_tile, kn_tile, blockdiag)


def _moba_kernel(qt_ref, k_ref, vt_ref, km_ref, o_ref, acc_ref, m_ref, sel_ref):
    tb = MOBA_BLOCK
    nbp = km_ref.shape[1]
    i = pl.program_id(2)
    qt = qt_ref[0]
    km = km_ref[0].astype(BF16)
    chan = lax.broadcasted_iota(jnp.int32, (LANES, 1), 0)
    in_head = [(chan // HEAD_DIM) == hh for hh in range(2)]
    blk = lax.broadcasted_iota(jnp.int32, (nbp, tb), 0).astype(F32)
    own = i.astype(F32)
    key = lax.broadcasted_iota(jnp.int32, (tb, tb), 0)
    qry = lax.broadcasted_iota(jnp.int32, (tb, tb), 1)
    visible = key <= qry
    off_own = pl.multiple_of(i * tb, tb)
    k_own = k_ref[0, pl.ds(off_own, tb), :]
    vt_own = vt_ref[0, :, pl.ds(off_own, tb)]
    zero = jnp.zeros_like(qt)
    one = jnp.ones_like(qt)
    qs = []
    for hh in range(2):
        qm = jnp.where(in_head[hh], qt, zero)
        gate = jnp.dot(km, qm, preferred_element_type=F32)
        g = jnp.where(blk < own, gate, -jnp.inf)
        sel = jnp.zeros((nbp, tb), F32)
        for r in range(MOBA_TOPK):
            mx = jnp.max(g, axis=0, keepdims=True)
            idx = jnp.min(jnp.where(g == mx, blk, float(nbp)), axis=0, keepdims=True)
            hit = blk == idx
            keep = jnp.where(i > r, 1.0, 0.0)
            sel = sel + jnp.where(hit, keep, 0.0)
            g = jnp.where(hit, -jnp.inf, g)
        sel_ref[hh] = sel
        qb = qm * (HEAD_DIM ** -0.5)
        qs.append(qb)
        s = jnp.dot(k_own, qb, preferred_element_type=F32)
        s = jnp.where(visible, s, NEG)
        m0 = jnp.max(s, axis=0, keepdims=True)
        p = jnp.exp(s - m0)
        m_ref[hh] = m0
        acc_ref[hh] = jnp.dot(jnp.where(in_head[hh], vt_own, one), p.astype(BF16),
                              preferred_element_type=F32)

    def past_blocks(n, nblk):
        off = pl.multiple_of(n * tb, tb)
        k = k_ref[0, pl.ds(off, nblk * tb), :]
        vt = vt_ref[0, :, pl.ds(off, nblk * tb)]
        scores = [jnp.dot(k, qs[hh], preferred_element_type=F32) for hh in range(2)]
        probs, alphas = [], []
        for hh in range(2):
            parts = []
            for j in range(nblk):
                chosen = sel_ref[hh, pl.ds(n + j, 1), :]
                parts.append(jnp.where(chosen > 0.0, scores[hh][j * tb:(j + 1) * tb, :], NEG))
            m_old = m_ref[hh]
            m_new = m_old
            for part in parts:
                m_new = jnp.maximum(m_new, jnp.max(part, axis=0, keepdims=True))
            alphas.append(jnp.exp(m_old - m_new))
            probs.append(jnp.concatenate([jnp.exp(part - m_new).astype(BF16) for part in parts], axis=0))
            m_ref[hh] = m_new
        ones_v = jnp.ones_like(vt)
        pv = [jnp.dot(jnp.where(in_head[hh], vt, ones_v), probs[hh], preferred_element_type=F32)
              for hh in range(2)]
        for hh in range(2):
            acc_ref[hh] = acc_ref[hh] * alphas[hh] + pv[hh]

    def four_blocks(t, c):
        past_blocks(4 * t, 4)
        return c

    lax.fori_loop(0, i // 4, four_blocks, 0)
    rem = i % 4

    @pl.when(rem >= 2)
    def _():
        past_blocks(i - rem, 2)

    @pl.when(rem % 2 == 1)
    def _():
        past_blocks(i - 1, 1)

    a0, a1 = acc_ref[0], acc_ref[1]
    den0 = a0[HEAD_DIM:HEAD_DIM + 1, :]
    den1 = a1[0:1, :]
    o_ref[0] = jnp.where(in_head[0], a0 / den0, a1 / den1).T


def moba_attention(q_t, k_rot, v_t, kmean):
    B, S, _ = k_rot.shape
    tb = MOBA_BLOCK
    nb = S // tb
    npairs = OD_W // LANES
    nbp = -(-nb // LANES) * LANES
    kmean = jnp.pad(kmean, ((0, 0), (0, nbp - nb), (0, 0)))
    return pl.pallas_call(
        _moba_kernel,
        grid=(B, npairs, nb),
        in_specs=[pl.BlockSpec((1, LANES, tb), lambda b, p, i: (b, p, i)),
                  pl.BlockSpec((1, S, LANES), lambda b, p, i: (b, 0, p)),
                  pl.BlockSpec((1, LANES, S), lambda b, p, i: (b, p, 0)),
                  pl.BlockSpec((1, nbp, LANES), lambda b, p, i: (b, 0, p))],
        out_specs=pl.BlockSpec((1, tb, LANES), lambda b, p, i: (b, i, p)),
        out_shape=jax.ShapeDtypeStruct((B, S, OD_W), F32),
        scratch_shapes=[pltpu.VMEM((2, LANES, tb), F32), pltpu.VMEM((2, 1, tb), F32),
                        pltpu.VMEM((2, nbp, tb), F32)],
        compiler_params=_cparams(("parallel", "parallel", "arbitrary")),
        name="moba_attention",
    )(q_t, k_rot, v_t, kmean)


def _rglru_kernel(x_ref, gate_ref, cw_ref, cb_ref, wa_ref, ba_ref, wx_ref, bx_ref, lam_ref, o_ref,
                  buf_ref, xprev_ref, hprev_ref, *, ts):
    @pl.when(pl.program_id(1) == 0)
    def _():
        xprev_ref[...] = jnp.zeros_like(xprev_ref)
        hprev_ref[...] = jnp.zeros_like(hprev_ref)

    x = x_ref[0]
    buf_ref[0:8, :] = xprev_ref[...]
    buf_ref[8:8 + ts, :] = x
    xprev_ref[...] = x[ts - 8:ts, :]
    xc = cb_ref[...]
    for kk in range(RG_CONV):
        start = 8 - (RG_CONV - 1) + kk
        xc = xc + cw_ref[kk:kk + 1, :] * buf_ref[start:start + ts, :]
    r = jax.nn.sigmoid(_dot(xc, wa_ref[...]) + ba_ref[...])
    ig = jax.nn.sigmoid(_dot(xc, wx_ref[...]) + bx_ref[...])
    log_a = (-RG_C) * r * _softplus(-lam_ref[...])
    a = jnp.exp(log_a)
    u = jnp.sqrt(-jnp.tanh(log_a) * (a * a + 1.0)) * (ig * xc)
    t_idx = lax.broadcasted_iota(jnp.int32, (ts, 1), 0)
    d = 1
    while d < ts:
        valid = t_idx >= d
        a_sh = pltpu.roll(a, d, axis=0)
        u_sh = pltpu.roll(u, d, axis=0)
        u = jnp.where(valid, a * u_sh, 0.0) + u
        a = jnp.where(valid, a * a_sh, a)
        d *= 2
    h = a * hprev_ref[...] + u
    hprev_ref[...] = h[ts - 1:ts, :]
    o_ref[0] = h * jax.nn.gelu(gate_ref[0], approximate=True)


def rg_lru_mixer(proj3, conv_w, conv_b, wa_bd, ba, wx_bd, bx, lam, ts=256):
    B, S, _ = proj3.shape
    W = conv_w.shape[1]
    full = lambda shape: pl.BlockSpec(shape, lambda b, t: (0,) * len(shape))
    row = lambda v: v.reshape(1, W)
    return pl.pallas_call(
        functools.partial(_rglru_kernel, ts=ts),
        grid=(B, S // ts),
        in_specs=[pl.BlockSpec((1, ts, W), lambda b, t: (b, t, OD_RX // 4)),
                  pl.BlockSpec((1, ts, W), lambda b, t: (b, t, OD_RG // 4)),
                  full((RG_CONV, W)), full((1, W)), full((W, W)), full((1, W)), full((W, W)),
                  full((1, W)), full((1, W))],
        out_specs=pl.BlockSpec((1, ts, W), lambda b, t: (b, t, 0)),
        out_shape=jax.ShapeDtypeStruct((B, S, W), F32),
        scratch_shapes=[pltpu.VMEM((ts + 8, W), F32), pltpu.VMEM((8, W), F32), pltpu.VMEM((1, W), F32)],
        compiler_params=_cparams(("parallel", "arbitrary")),
        name="rg_lru",
    )(proj3, proj3, conv_w, row(conv_b), wa_bd, row(ba), wx_bd, row(bx), row(lam))


def _xattn_kernel(x_ref, kv_ref, g_ref, wq_ref, qn_ref, kn_ref, wo_ref, o_ref):
    x = x_ref[0]
    q = _dot(_rms(x, g_ref[...]), wq_ref[...])
    kv = kv_ref[0]
    hw = XA_HEADS * XA_DH
    outs = []
    for h in range(XA_HEADS):
        hl = slice(h * XA_DH, (h + 1) * XA_DH)
        qh = _rms(q[:, hl], qn_ref[...])
        kh = _rms(kv[:, hl], kn_ref[...])
        vh = kv[:, hw + h * XA_DH: hw + (h + 1) * XA_DH]
        s = _dot_nt(qh, kh) * (XA_DH ** -0.5)
        s = s - jnp.max(s, axis=1, keepdims=True)
        p = jnp.exp(s)
        p = p / jnp.sum(p, axis=1, keepdims=True)
        outs.append(_dot(p, vh))
    o_ref[0] = x + _dot(jnp.concatenate(outs, axis=1), wo_ref[...])


def cross_attention(x3, kv3, g, wq_bf16, qn, kn, wo_bf16, tq=256):
    B, S, D = x3.shape
    M = kv3.shape[1]
    hw = XA_HEADS * XA_DH
    full = lambda shape: pl.BlockSpec(shape, lambda b, i: (0,) * len(shape))
    return pl.pallas_call(
        _xattn_kernel,
        grid=(B, S // tq),
        in_specs=[pl.BlockSpec((1, tq, D), lambda b, i: (b, i, 0)),
                  pl.BlockSpec((1, M, 2 * hw), lambda b, i: (b, 0, 0)),
                  full((1, D)), full((D, hw)), full((1, XA_DH)), full((1, XA_DH)), full((hw, D))],
        out_specs=pl.BlockSpec((1, tq, D), lambda b, i: (b, i, 0)),
        out_shape=jax.ShapeDtypeStruct((B, S, D), F32),
        compiler_params=_cparams(("parallel", "parallel")),
        name="cross_attention",
    )(x3, kv3, g.reshape(1, D), wq_bf16, qn.reshape(1, -1), kn.reshape(1, -1), wo_bf16)


BF16_ROWS = 16


def _top_values(s, k, want_rank=False):
    rows = []
    rank = jnp.full(s.shape, float(k), F32) if want_rank else None
    for r in range(k):
        m = jnp.max(s, axis=0, keepdims=True)
        rows.append(m)
        hit = s == m
        if want_rank:
            rank = jnp.where(hit, float(r), rank)
        s = jnp.where(hit, -jnp.inf, s)
    return rows, rank


def _routing_stages(s1, s2, sub, finish, nstage):
    K = PEER_TOPK
    st = {"a1": s1, "a2": s2, "v1": [], "v2": [], "rank2": jnp.full(s2.shape, float(K), F32)}

    def extract(src, dst, rounds, want_rank):
        def run(nil):
            arr = st[src]
            gone = nil - jnp.inf
            for r in rounds:
                m = jnp.max(arr, axis=0, keepdims=True)
                st[dst].append(m)
                hit = arr == m
                if want_rank:
                    st["rank2"] = jnp.where(hit, float(r), st["rank2"])
                arr = jnp.where(hit, gone, arr)
            st[src] = arr
            st["last"] = m
        return run

    def candidates(nil):
        v1, v2 = st["v1"], st["v2"]
        v1m = jnp.concatenate(v1, axis=0)
        v2m = jnp.concatenate(v2, axis=0)
        groups = [(v1[0] + nil) + v2m]
        for a in range(1, 8):
            groups.append(jnp.where(sub < K // (a + 1), v1[a] + v2m[0:8], -jnp.inf))
        groups.append(v1m[8:16] + v2[0])
        st["cand"] = jnp.concatenate(groups, axis=0)
        st["top"] = []
        st["last"] = v2[0]

    def threshold(nil):
        top = st["top"]
        st["tau"] = top[K - 1]
        z = nil
        for tv in top:
            z = z + jnp.exp(tv - top[0])
        st["z"] = z
        st["last"] = z

    def counts(nil):
        v1, v2, tau = st["v1"], st["v2"], st["tau"]
        count = jnp.zeros(s1.shape, F32)
        for bb in range(K // 2):
            count = count + jnp.where(s1 + v2[bb] >= tau, 1.0, 0.0)
        best = nil
        for bb in range(K // 2, K):
            best = best + jnp.where(v1[0] + v2[bb] >= tau, 1.0, 0.0)
        count = count + jnp.where(s1 == v1[0], best, 0.0)
        finish(count, jnp.exp(s1 - v1[0]) / st["z"], st["rank2"].astype(BF16),
               jnp.exp(s2 - v2[0]).astype(BF16))
        st["last"] = best

    quarter = [range(q * 4, q * 4 + 4) for q in range(4)]
    work = ([extract("a1", "v1", r, False) for r in quarter]
            + [extract("a2", "v2", r, True) for r in quarter]
            + [candidates] + [extract("cand", "top", r, False) for r in quarter]
            + [threshold, counts])
    bounds = [round(j * len(work) / nstage) for j in range(nstage + 1)]

    def stage(j):
        def run(nil):
            for piece in work[bounds[j]:bounds[j + 1]]:
                piece(nil)
            return st["last"]
        return run

    return [stage(j) for j in range(nstage)]


def _peer_kernel(x_ref, xnext_ref, g_ref, wq_ref, sk_ref, u_ref, vt_ref, o_ref,
                 xn_ref, xnt_ref, c_ref, a_ref, r2_ref, e2_ref, acc_ref, *, ib, isub):
    t = pl.program_id(0)
    e = pl.program_id(1)
    nk = PEER_NKEYS
    tq = x_ref.shape[0]
    slot = t % 2
    nslot = 1 - slot
    sub = lax.broadcasted_iota(jnp.int32, (8, 1), 0)
    nsub = ib // isub

    def prepare(src_ref, s):
        xn32 = _rms(src_ref[...], g_ref[...])
        xn_ref[...] = xn32.astype(BF16)
        xnt_ref[s] = xn32.T.astype(BF16)

    def route_scores(h):
        q = jnp.dot(xn_ref[...], wq_ref[h], preferred_element_type=F32)
        return _dot_nt(sk_ref[2 * h], q[:, :nk]), _dot_nt(sk_ref[2 * h + 1], q[:, nk:])

    def table_writer(h, s):
        def finish(count, amp, rank2, e2):
            c_ref[s, h] = count
            a_ref[s, h] = amp
            r2_ref[s, h] = rank2
            e2_ref[s, h] = e2
        return finish

    @pl.when(jnp.logical_and(t == 0, e == 0))
    def _():
        prepare(x_ref, 0)

        def one(h, c):
            s1, s2 = route_scores(h)
            for run in _routing_stages(s1, s2, sub, table_writer(h, 0), 1):
                run(jnp.zeros((1, tq), F32))
            return c

        lax.fori_loop(0, PEER_HEADS, one, 0)

    @pl.when(e == 0)
    def _():
        acc_ref[...] = jnp.zeros_like(acc_ref)
        prepare(xnext_ref, nslot)

    ngrp = nk // BF16_ROWS
    xnt = xnt_ref[slot]
    first = pl.multiple_of(e * ib, ib)
    step_c = [c_ref[slot, h, pl.ds(first, ib), :] for h in range(PEER_HEADS)]
    step_a = [a_ref[slot, h, pl.ds(first, ib), :] for h in range(PEER_HEADS)]

    def expert_matmul(sc):
        lo = sc * isub * nk
        u_sub = pltpu.bitcast(u_ref[lo // 2:(lo + isub * nk) // 2, :], BF16)
        return jnp.dot(u_sub, xnt, preferred_element_type=F32)

    hid_next = expert_matmul(0)
    next_s1, next_s2 = route_scores(e)
    stages = _routing_stages(next_s1, next_s2, sub, table_writer(e, nslot), nsub)

    total = None
    after = jnp.zeros((1, tq), F32)
    for sc in range(nsub):
        zero = jnp.broadcast_to(after, (BF16_ROWS, tq)).astype(BF16)
        lo = sc * isub * nk
        hid_sub = hid_next
        if sc + 1 < nsub:
            hid_next = expert_matmul(sc + 1)
        pieces = []
        for ii in range(isub):
            il = sc * isub + ii
            hid = hid_sub[ii * nk:(ii + 1) * nk, :]
            act = (0.5 * hid * (1.0 + lax.erf(hid * np.float32(np.sqrt(0.5))))).astype(BF16)
            w = [zero] * ngrp
            for h in range(PEER_HEADS):
                cnt = jnp.broadcast_to(step_c[h][il:il + 1, :], (BF16_ROWS, tq)).astype(BF16)
                amp = jnp.broadcast_to(step_a[h][il:il + 1, :], (BF16_ROWS, tq)).astype(BF16)
                for gi in range(ngrp):
                    rows = slice(gi * BF16_ROWS, (gi + 1) * BF16_ROWS)
                    w[gi] = w[gi] + jnp.where(r2_ref[slot, h, rows, :] < cnt,
                                              e2_ref[slot, h, rows, :] * amp, zero)
            pieces += [w[gi] * act[gi * BF16_ROWS:(gi + 1) * BF16_ROWS, :] for gi in range(ngrp)]
        after = stages[sc](pieces[-1][0:1, :].astype(F32) * 0.0) * 0.0
        vt_sub = pltpu.bitcast(vt_ref[:, lo:lo + isub * nk], BF16)
        part = jnp.dot(vt_sub, jnp.concatenate(pieces, axis=0), preferred_element_type=F32)
        total = part if total is None else total + part
    acc_ref[...] += total

    @pl.when(e == pl.num_programs(1) - 1)
    def _():
        o_ref[...] = x_ref[...] + acc_ref[...].T


def _pair_rows_kernel(x_ref, o_ref, *, transpose):
    x = x_ref[...]
    if transpose:
        x = x.T
    o_ref[...] = pltpu.bitcast(x.astype(BF16), jnp.uint32)


def _pair_rows(w, transpose=False, tr=512):
    R, C = w.shape
    if transpose:
        in_spec = pl.BlockSpec((tr, C), lambda i: (i, 0))
        out_spec = pl.BlockSpec((C // 2, tr), lambda i: (0, i))
        out_shape = jax.ShapeDtypeStruct((C // 2, R), jnp.uint32)
    else:
        in_spec = pl.BlockSpec((tr, C), lambda i: (i, 0))
        out_spec = pl.BlockSpec((tr // 2, C), lambda i: (i, 0))
        out_shape = jax.ShapeDtypeStruct((R // 2, C), jnp.uint32)
    return pl.pallas_call(
        functools.partial(_pair_rows_kernel, transpose=transpose),
        grid=(R // tr,), in_specs=[in_spec], out_specs=out_spec, out_shape=out_shape,
        compiler_params=_cparams(("parallel",)),
        name="pair_rows",
    )(w)


def peer_ffn(x2d, g, wq3_bf16, sk_bf16, u_pairs, vt_pairs, tq=256, isub=2):
    T, D = x2d.shape
    H = PEER_HEADS
    ne = H
    ib = PEER_NKEYS // ne
    ec = ib * PEER_NKEYS
    nt = T // tq
    full = lambda shape: pl.BlockSpec(shape, lambda i, e: (0,) * len(shape))
    words = pltpu.VMEM((2, H, PEER_NKEYS, tq), F32)
    halves = pltpu.VMEM((2, H, PEER_NKEYS, tq), BF16)
    return pl.pallas_call(
        functools.partial(_peer_kernel, ib=ib, isub=isub),
        grid=(nt, ne),
        in_specs=[pl.BlockSpec((tq, D), lambda i, e: (i, 0)),
                  pl.BlockSpec((tq, D), lambda i, e: (jnp.minimum(i + 1, nt - 1), 0)),
                  full((1, D)), full((H, D, 2 * PEER_NKEYS)), full((2 * H, PEER_NKEYS, PEER_NKEYS)),
                  pl.BlockSpec((ec // 2, D), lambda i, e: (e, 0)),
                  pl.BlockSpec((D // 2, ec), lambda i, e: (0, e))],
        out_specs=pl.BlockSpec((tq, D), lambda i, e: (i, 0)),
        out_shape=jax.ShapeDtypeStruct((T, D), F32),
        scratch_shapes=[pltpu.VMEM((tq, D), BF16), pltpu.VMEM((2, D, tq), BF16),
                        words, words, halves, halves, pltpu.VMEM((D, tq), F32)],
        compiler_params=_cparams(("arbitrary", "arbitrary")),
        name="peer_ffn",
    )(x2d, x2d, g.reshape(1, D), wq3_bf16, sk_bf16, u_pairs, vt_pairs)


def _block_diag(w):
    G, n, _ = w.shape
    eye = jnp.eye(G, dtype=w.dtype)
    return (eye[:, None, :, None] * w[:, :, None, :]).reshape(G * n, G * n)


def even_layer(x2d, B, S, norm, w_in, gate_up, gate_b, out_norm, w_out):
    D = x2d.shape[1]
    gg0 = EV_GG * LANES
    wp = jnp.concatenate([w_in[:, :1536], w_in[:, 1536 + GLA_RANK:], w_in[:, 1536:1536 + GLA_RANK],
                          jnp.zeros((D, EV_COLS - gg0 - GLA_RANK), w_in.dtype)], axis=1).astype(BF16)
    proj = norm_proj(x2d, norm, wp).reshape(B, S, EV_COLS)
    gup = jnp.concatenate([gate_up, jnp.zeros((LANES - GLA_RANK, gate_up.shape[1]), gate_up.dtype)], axis=0)
    a_out = gla_mixer(proj, gup.astype(BF16), gate_b, out_norm)
    b_out = sb_attention(proj)
    return out_proj(x2d, a_out.reshape(B * S, -1), b_out.reshape(B * S, -1), w_out.astype(BF16))


def odd_layer(x2d, B, S, positions, norm, w_in, q_norm, k_norm, conv_w, conv_b, wa, ba, wx, bx, lam, w_out):
    proj = norm_proj(x2d, norm, w_in.astype(BF16)).reshape(B, S, -1)
    half = HEAD_DIM // 2
    inv_freq = ROPE_THETA ** (-jnp.arange(0, HEAD_DIM, 2, dtype=F32) / HEAD_DIM)
    inv_tile = jnp.tile(inv_freq, LANES // half).reshape(1, LANES)
    seg = np.arange(OD_W) // HEAD_DIM
    blockdiag = jnp.asarray(seg[:, None] == seg[None, :], dtype=BF16)
    q_t, k_rot, v_t, kmean = moba_prep(proj, positions.reshape(B, S, 1), inv_tile,
                                       jnp.tile(q_norm, OD_W // HEAD_DIM).reshape(1, OD_W),
                                       jnp.tile(k_norm, OD_W // HEAD_DIM).reshape(1, OD_W), blockdiag)
    c_out = moba_attention(q_t, k_rot, v_t, kmean.reshape(B, S // MOBA_BLOCK, OD_W))
    d_out = rg_lru_mixer(proj, conv_w, conv_b, _block_diag(wa).astype(BF16), ba,
                         _block_diag(wx).astype(BF16), bx, lam)
    return out_proj(x2d, c_out.reshape(B * S, -1), d_out.reshape(B * S, -1), w_out.astype(BF16))


def kernel(x, mem, positions, ev_norm, ev_w_in, ev_gla_gate_up, ev_gla_gate_b, ev_gla_out_norm, ev_w_out, od_norm, od_w_in, od_q_norm, od_k_norm, od_conv_w, od_conv_b, od_gate_a_w, od_gate_a_b, od_gate_x_w, od_gate_x_b, od_lambda, od_w_out, xa_norm, xa_mem_norm, xa_wq, xa_wkv, xa_q_norm, xa_k_norm, xa_wo, ffn_norm, peer_wq, peer_subkeys, peer_u, peer_v):
    B, S, D = x.shape
    M = mem.shape[1]
    depth = xa_norm.shape[0]
    x2d = x.reshape(B * S, D)
    mem2d = mem.reshape(B * M, D)
    for l in range(depth):
        if l % 2 == 0:
            e = l // 2
            x2d = even_layer(x2d, B, S, ev_norm[e], ev_w_in[e], ev_gla_gate_up[e], ev_gla_gate_b[e],
                             ev_gla_out_norm[e], ev_w_out[e])
        else:
            o = l // 2
            x2d = odd_layer(x2d, B, S, positions, od_norm[o], od_w_in[o], od_q_norm[o], od_k_norm[o],
                            od_conv_w[o], od_conv_b[o], od_gate_a_w[o], od_gate_a_b[o],
                            od_gate_x_w[o], od_gate_x_b[o], od_lambda[o], od_w_out[o])
        kv = norm_proj(mem2d, xa_mem_norm[l], xa_wkv[l].astype(BF16))
        x2d = cross_attention(x2d.reshape(B, S, D), kv.reshape(B, M, -1), xa_norm[l],
                              xa_wq[l].astype(BF16), xa_q_norm[l], xa_k_norm[l],
                              xa_wo[l].astype(BF16)).reshape(B * S, D)
        H = PEER_HEADS
        wq3 = peer_wq[l].reshape(D, H, 2 * PEER_NKEYS).transpose(1, 0, 2).astype(BF16)
        sk = peer_subkeys[l].reshape(2 * H, PEER_NKEYS, -1).astype(BF16)
        x2d = peer_ffn(x2d, ffn_norm[l], wq3, sk, _pair_rows(peer_u[l]),
                       _pair_rows(peer_v[l], transpose=True))
    return x2d.reshape(B, S, D)
```

```python
import functools

import jax
import jax.numpy as jnp
import numpy as np
from jax import lax
from jax.experimental import pallas as pl
from jax.experimental.pallas import tpu as pltpu

F32 = jnp.float32
BF16 = jnp.bfloat16

EPS = 1e-6
ROPE_THETA = 10000.0
LANES = 128
HEAD_DIM = 64
GLA_HEADS = 4
GLA_DV = 128
GLA_CHUNK = 64
GLA_TAU = 16.0
GLA_RANK = 16
MOBA_BLOCK = 256
MOBA_TOPK = 3
RG_C = 8.0
RG_CONV = 4
XA_HEADS = 4
XA_DH = 128
PEER_HEADS = 8
PEER_NKEYS = 128
PEER_TOPK = 16
PEER_STEP_EXPERTS = PEER_NKEYS // PEER_HEADS * PEER_NKEYS
NEG = -1e30
F32_LOG_TINY = -104.0
VMEM_LIMIT = 56 * 1024 * 1024


def _cparams(sem):
    return pltpu.CompilerParams(dimension_semantics=sem, vmem_limit_bytes=VMEM_LIMIT)


def _dot(a, b):
    return jnp.dot(a.astype(BF16), b.astype(BF16), preferred_element_type=F32)


def _dot_nt(a, b):
    return lax.dot_general(a.astype(BF16), b.astype(BF16), (((1,), (1,)), ((), ())),
                           preferred_element_type=F32)


def _dot_tn(a, b):
    return lax.dot_general(a.astype(BF16), b.astype(BF16), (((0,), (0,)), ((), ())),
                           preferred_element_type=F32)


def _split2(a):
    hi = a.astype(BF16)
    lo = (a - hi.astype(F32)).astype(BF16)
    return hi, lo


def _dot_exact_rhs(a, m_bf16):
    hi, lo = _split2(a)
    return (jnp.dot(hi, m_bf16, preferred_element_type=F32)
            + jnp.dot(lo, m_bf16, preferred_element_type=F32))


def _dot_exact_lhs(m_bf16, a):
    hi, lo = _split2(a)
    return (jnp.dot(m_bf16, hi, preferred_element_type=F32)
            + jnp.dot(m_bf16, lo, preferred_element_type=F32))


def _rms(x, g):
    return x * lax.rsqrt(jnp.mean(x * x, axis=-1, keepdims=True) + EPS) * g


def _softplus(z):
    return jnp.maximum(z, 0.0) + jnp.log1p(jnp.exp(-jnp.abs(z)))


def _head_mask(hh):
    lane = lax.broadcasted_iota(jnp.int32, (1, LANES), 1)
    return ((lane // HEAD_DIM) == hh).astype(F32)


def _norm_proj_kernel(x_ref, g_ref, w_ref, o_ref):
    xn = _rms(x_ref[...], g_ref[...])
    o_ref[...] = jnp.dot(xn.astype(BF16), w_ref[...], preferred_element_type=F32)


def norm_proj(x2d, g, w_bf16, tm=256):
    T, D = x2d.shape
    N = w_bf16.shape[1]
    return pl.pallas_call(
        _norm_proj_kernel,
        grid=(T // tm,),
        in_specs=[pl.BlockSpec((tm, D), lambda i: (i, 0)),
                  pl.BlockSpec((1, D), lambda i: (0, 0)),
                  pl.BlockSpec((D, N), lambda i: (0, 0))],
        out_specs=pl.BlockSpec((tm, N), lambda i: (i, 0)),
        out_shape=jax.ShapeDtypeStruct((T, N), F32),
        compiler_params=_cparams(("parallel",)),
        name="norm_proj",
    )(x2d, g.reshape(1, D), w_bf16)


def _out_proj_kernel(x_ref, a_ref, b_ref, w_ref, o_ref):
    ka = a_ref.shape[1]
    o_ref[...] = (x_ref[...] + _dot(a_ref[...], w_ref[0:ka, :])
                  + _dot(b_ref[...], w_ref[ka:, :]))


def out_proj(x2d, a2d, b2d, w_bf16, tm=512):
    T, D = x2d.shape
    ka, kb = a2d.shape[1], b2d.shape[1]
    return pl.pallas_call(
        _out_proj_kernel,
        grid=(T // tm,),
        in_specs=[pl.BlockSpec((tm, D), lambda i: (i, 0)),
                  pl.BlockSpec((tm, ka), lambda i: (i, 0)),
                  pl.BlockSpec((tm, kb), lambda i: (i, 0)),
                  pl.BlockSpec((ka + kb, D), lambda i: (0, 0))],
        out_specs=pl.BlockSpec((tm, D), lambda i: (i, 0)),
        out_shape=jax.ShapeDtypeStruct((T, D), F32),
        compiler_params=_cparams(("parallel",)),
        name="out_proj",
    )(x2d, a2d, b2d, w_bf16)


EV_Q, EV_K, EV_V, EV_R, EV_SQ, EV_SK, EV_SV, EV_GG = 0, 2, 4, 8, 12, 16, 20, 24
EV_COLS = 25 * LANES


def _gla_kernel(q_ref, k_ref, v_ref, r_ref, gg_ref, gup_ref, gb_ref, onorm_ref, o_ref, state_ref,
                *, ts):
    C = GLA_CHUNK

    @pl.when(pl.program_id(1) == 0)
    def _():
        state_ref[...] = jnp.zeros_like(state_ref)

    row = lax.broadcasted_iota(jnp.int32, (C, C), 0)
    col = lax.broadcasted_iota(jnp.int32, (C, C), 1)
    causal = row >= col
    tri = causal.astype(BF16)
    masks = [_head_mask(0), _head_mask(1)]
    scale = HEAD_DIM ** -0.5
    onorm = onorm_ref[...]

    for ci in range(ts // C):
        sl = slice(ci * C, (ci + 1) * C)
        pre = _dot(gg_ref[0, sl, :], gup_ref[...]) + gb_ref[...]
        g = (jnp.minimum(pre, 0.0) - jnp.log1p(jnp.exp(-jnp.abs(pre)))) * (1.0 / GLA_TAU)
        b = _dot_exact_lhs(tri, g)
        bmid = b[C // 2 - 1:C // 2, :]
        blast = b[C - 1:C, :]
        q = q_ref[0, sl, :] * scale
        k = k_ref[0, sl, :]
        qd = q * jnp.exp(b - bmid)
        kd = k * jnp.exp(bmid - b)
        qe = q * jnp.exp(b)
        kdec = k * jnp.exp(blast - b)
        eb_last = jnp.exp(blast)
        for p in range(GLA_HEADS // 2):
            lanes = slice(p * LANES, (p + 1) * LANES)
            st = state_ref[p]
            new_st = st * eb_last[:, lanes]
            for hh in range(2):
                h = 2 * p + hh
                m = masks[hh]
                hl = slice(h * GLA_DV, (h + 1) * GLA_DV)
                a = _dot_nt(qd[:, lanes] * m, kd[:, lanes])
                a = jnp.where(causal, a, 0.0)
                v_h = v_ref[0, sl, hl]
                o = _dot(a, v_h) + _dot_nt(qe[:, lanes] * m, st)
                o = _rms(o, onorm)
                r_h = r_ref[0, sl, hl]
                o_ref[0, sl, hl] = o * (r_h * jax.nn.sigmoid(r_h))
                new_st = new_st + _dot_tn(v_h, kdec[:, lanes] * m)
            state_ref[p] = new_st


def gla_mixer(proj3, gate_up_pad, gate_b, out_norm, ts=256):
    B, S, _ = proj3.shape
    nh = GLA_HEADS
    w = nh * GLA_DV

    def col(blk_w, tile):
        idx = tile * LANES // blk_w
        return pl.BlockSpec((1, ts, blk_w), lambda b, c: (b, c, idx))

    return pl.pallas_call(
        functools.partial(_gla_kernel, ts=ts),
        grid=(B, S // ts),
        in_specs=[col(nh * HEAD_DIM, EV_Q), col(nh * HEAD_DIM, EV_K), col(w, EV_V), col(w, EV_R),
                  col(LANES, EV_GG),
                  pl.BlockSpec((LANES, nh * HEAD_DIM), lambda b, c: (0, 0)),
                  pl.BlockSpec((1, nh * HEAD_DIM), lambda b, c: (0, 0)),
                  pl.BlockSpec((1, GLA_DV), lambda b, c: (0, 0))],
        out_specs=pl.BlockSpec((1, ts, w), lambda b, c: (b, c, 0)),
        out_shape=jax.ShapeDtypeStruct((B, S, w), F32),
        scratch_shapes=[pltpu.VMEM((nh // 2, GLA_DV, LANES), F32)],
        compiler_params=_cparams(("parallel", "arbitrary")),
        name="gla",
    )(proj3, proj3, proj3, proj3, proj3, gate_up_pad, gate_b.reshape(1, -1), out_norm.reshape(1, -1))


def _sb_kernel(q_ref, k_ref, v_ref, o_ref, acc_ref, carry_ref, *, tq):
    i = pl.program_id(2)
    q = q_ref[0] * (HEAD_DIM ** -0.5)
    masks = [_head_mask(0), _head_mask(1)]
    qh = [(q * m).astype(BF16) for m in masks]
    row = lax.broadcasted_iota(jnp.int32, (tq, tq), 0)
    col = lax.broadcasted_iota(jnp.int32, (tq, tq), 1)
    upper = (row > col).astype(BF16)
    past = col < row
    acc_ref[...] = jnp.zeros_like(acc_ref)
    carry_ref[...] = jnp.zeros_like(carry_ref)

    def tile(j, diag):
        off = pl.multiple_of(j * tq, tq)
        k = k_ref[0, pl.ds(off, tq), :].astype(BF16)
        v = v_ref[0, pl.ds(off, tq), :].astype(BF16)
        for hh in range(2):
            z = lax.dot_general(qh[hh], k, (((1,), (1,)), ((), ())), preferred_element_type=F32)
            sp = _softplus(z)
            log_rem = jnp.where(past, -sp, 0.0) if diag else -sp
            carry = carry_ref[hh]
            after = _dot_exact_rhs(log_rem, upper) + carry
            w = jnp.exp((z - sp) + after)
            if diag:
                w = jnp.where(past, w, 0.0)
            acc_ref[hh] += jnp.dot(w.astype(BF16), v, preferred_element_type=F32)
            carry_ref[hh] = carry + jnp.sum(log_rem, axis=1, keepdims=True)

    tile(i, True)

    def live(c):
        j, worst = c
        return jnp.logical_and(j >= 0, worst > F32_LOG_TINY)

    def body(c):
        j, _ = c
        tile(j, False)
        return j - 1, jnp.max(carry_ref[...])

    lax.while_loop(live, body, (i - 1, jnp.max(carry_ref[...])))
    o_ref[0] = acc_ref[0] * masks[0] + acc_ref[1] * masks[1]


def sb_attention(proj3, tq=256):
    B, S, _ = proj3.shape
    npairs = 4
    return pl.pallas_call(
        functools.partial(_sb_kernel, tq=tq),
        grid=(B, npairs, S // tq),
        in_specs=[pl.BlockSpec((1, tq, LANES), lambda b, p, i: (b, i, EV_SQ + p)),
                  pl.BlockSpec((1, S, LANES), lambda b, p, i: (b, 0, EV_SK + p)),
                  pl.BlockSpec((1, S, LANES), lambda b, p, i: (b, 0, EV_SV + p))],
        out_specs=pl.BlockSpec((1, tq, LANES), lambda b, p, i: (b, i, p)),
        out_shape=jax.ShapeDtypeStruct((B, S, npairs * LANES), F32),
        scratch_shapes=[pltpu.VMEM((2, tq, LANES), F32), pltpu.VMEM((2, tq, 1), F32)],
        compiler_params=_cparams(("parallel", "parallel", "arbitrary")),
        name="sb_attention",
    )(proj3, proj3, proj3)


OD_Q, OD_K, OD_V, OD_RX, OD_RG = 0, 4, 8, 12, 16
OD_W = 4 * LANES


def _moba_prep_kernel(q_ref, k_ref, v_ref, pos_ref, invf_ref, qn_ref, kn_ref, bd_ref,
                      qt_ref, ko_ref, vt_ref, km_ref):
    ang = pos_ref[0].astype(F32) * invf_ref[...]
    cos1, sin1 = jnp.cos(ang), jnp.sin(ang)
    lane = lax.broadcasted_iota(jnp.int32, (1, LANES), 1)
    first_half = (lane % HEAD_DIM) < (HEAD_DIM // 2)
    bd = bd_ref[...]

    def norm_rope(x, gain):
        ms = _dot_exact_rhs(x * x, bd) * (1.0 / HEAD_DIM)
        xn = x * lax.rsqrt(ms + EPS) * gain
        outs = []
        for t in range(OD_W // LANES):
            xb = xn[:, t * LANES:(t + 1) * LANES]
            up = pltpu.roll(xb, LANES - HEAD_DIM // 2, axis=1)
            dn = pltpu.roll(xb, HEAD_DIM // 2, axis=1)
            outs.append(xb * cos1 + jnp.where(first_half, -up, dn) * sin1)
        return jnp.concatenate(outs, axis=1)

    qt_ref[0] = norm_rope(q_ref[0], qn_ref[...]).T.astype(BF16)
    kr = norm_rope(k_ref[0], kn_ref[...])
    ko_ref[0] = kr.astype(BF16)
    km_ref[0, 0] = jnp.mean(kr, axis=0, keepdims=True)
    vt_ref[0] = v_ref[0].T.astype(BF16)


def moba_prep(proj3, pos3, inv_freq_tile, qn_tile, kn_tile, blockdiag):
    B, S, _ = proj3.shape
    tb = MOBA_BLOCK
    nb = S // tb
    full = lambda shape: pl.BlockSpec(shape, lambda b, i: (0,) * len(shape))
    return pl.pallas_call(
        _moba_prep_kernel,
        grid=(B, nb),
        in_specs=[pl.BlockSpec((1, tb, OD_W), lambda b, i: (b, i, OD_Q // 4)),
                  pl.BlockSpec((1, tb, OD_W), lambda b, i: (b, i, OD_K // 4)),
                  pl.BlockSpec((1, tb, OD_W), lambda b, i: (b, i, OD_V // 4)),
                  pl.BlockSpec((1, tb, 1), lambda b, i: (b, i, 0)),
                  full((1, LANES)), full((1, OD_W)), full((1, OD_W)), full((OD_W, OD_W))],
        out_specs=[pl.BlockSpec((1, OD_W, tb), lambda b, i: (b, 0, i)),
                   pl.BlockSpec((1, tb, OD_W), lambda b, i: (b, i, 0)),
                   pl.BlockSpec((1, OD_W, tb), lambda b, i: (b, 0, i)),
                   pl.BlockSpec((1, 1, 1, OD_W), lambda b, i: (b, i, 0, 0))],
        out_shape=[jax.ShapeDtypeStruct((B, OD_W, S), BF16),
                   jax.ShapeDtypeStruct((B, S, OD_W), BF16),
                   jax.ShapeDtypeStruct((B, OD_W, S), BF16),
                   jax.ShapeDtypeStruct((B, nb, 1, OD_W), F32)],
        compiler_params=_cparams(("parallel", "parallel")),
        name="moba_prep",
    )(proj3, proj3, proj3, pos3, inv_freq_tile, qn_tile, kn_tile, blockdiag)


def _moba_kernel(qt_ref, k_ref, vt_ref, km_ref, o_ref, acc_ref, m_ref, sel_ref):
    tb = MOBA_BLOCK
    nbp = km_ref.shape[1]
    i = pl.program_id(2)
    qt = qt_ref[0]
    km = km_ref[0].astype(BF16)
    chan = lax.broadcasted_iota(jnp.int32, (LANES, 1), 0)
    in_head = [(chan // HEAD_DIM) == hh for hh in range(2)]
    blk = lax.broadcasted_iota(jnp.int32, (nbp, tb), 0).astype(F32)
    own = i.astype(F32)
    key = lax.broadcasted_iota(jnp.int32, (tb, tb), 0)
    qry = lax.broadcasted_iota(jnp.int32, (tb, tb), 1)
    visible = key <= qry
    off_own = pl.multiple_of(i * tb, tb)
    k_own = k_ref[0, pl.ds(off_own, tb), :]
    vt_own = vt_ref[0, :, pl.ds(off_own, tb)]
    zero = jnp.zeros_like(qt)
    one = jnp.ones_like(qt)
    qs = []
    for hh in range(2):
        qm = jnp.where(in_head[hh], qt, zero)
        gate = jnp.dot(km, qm, preferred_element_type=F32)
        g = jnp.where(blk < own, gate, -jnp.inf)
        sel = jnp.zeros((nbp, tb), F32)
        for r in range(MOBA_TOPK):
            mx = jnp.max(g, axis=0, keepdims=True)
            idx = jnp.min(jnp.where(g == mx, blk, float(nbp)), axis=0, keepdims=True)
            hit = blk == idx
            keep = jnp.where(i > r, 1.0, 0.0)
            sel = sel + jnp.where(hit, keep, 0.0)
            g = jnp.where(hit, -jnp.inf, g)
        sel_ref[hh] = sel
        qb = qm * (HEAD_DIM ** -0.5)
        qs.append(qb)
        s = jnp.dot(k_own, qb, preferred_element_type=F32)
        s = jnp.where(visible, s, NEG)
        m0 = jnp.max(s, axis=0, keepdims=True)
        p = jnp.exp(s - m0)
        m_ref[hh] = m0
        acc_ref[hh] = jnp.dot(jnp.where(in_head[hh], vt_own, one), p.astype(BF16),
                              preferred_element_type=F32)

    def past_blocks(n, nblk):
        off = pl.multiple_of(n * tb, tb)
        k = k_ref[0, pl.ds(off, nblk * tb), :]
        vt = vt_ref[0, :, pl.ds(off, nblk * tb)]
        scores = [jnp.dot(k, qs[hh], preferred_element_type=F32) for hh in range(2)]
        probs, alphas = [], []
        for hh in range(2):
            parts = []
            for j in range(nblk):
                chosen = sel_ref[hh, pl.ds(n + j, 1), :]
                parts.append(jnp.where(chosen > 0.0, scores[hh][j * tb:(j + 1) * tb, :], NEG))
            m_old = m_ref[hh]
            m_new = m_old
            for part in parts:
                m_new = jnp.maximum(m_new, jnp.max(part, axis=0, keepdims=True))
            alphas.append(jnp.exp(m_old - m_new))
            probs.append(jnp.concatenate([jnp.exp(part - m_new).astype(BF16) for part in parts], axis=0))
            m_ref[hh] = m_new
        ones_v = jnp.ones_like(vt)
        pv = [jnp.dot(jnp.where(in_head[hh], vt, ones_v), probs[hh], preferred_element_type=F32)
              for hh in range(2)]
        for hh in range(2):
            acc_ref[hh] = acc_ref[hh] * alphas[hh] + pv[hh]

    def four_blocks(t, c):
        past_blocks(4 * t, 4)
        return c

    lax.fori_loop(0, i // 4, four_blocks, 0)
    rem = i % 4

    @pl.when(rem >= 2)
    def _():
        past_blocks(i - rem, 2)

    @pl.when(rem % 2 == 1)
    def _():
        past_blocks(i - 1, 1)

    a0, a1 = acc_ref[0], acc_ref[1]
    den0 = a0[HEAD_DIM:HEAD_DIM + 1, :]
    den1 = a1[0:1, :]
    o_ref[0] = jnp.where(in_head[0], a0 / den0, a1 / den1).T


def moba_attention(q_t, k_rot, v_t, kmean):
    B, S, _ = k_rot.shape
    tb = MOBA_BLOCK
    nb = S // tb
    npairs = OD_W // LANES
    nbp = -(-nb // BF16_ROWS) * BF16_ROWS
    kmean = jnp.pad(kmean, ((0, 0), (0, nbp - nb), (0, 0)))
    return pl.pallas_call(
        _moba_kernel,
        grid=(B, npairs, nb),
        in_specs=[pl.BlockSpec((1, LANES, tb), lambda b, p, i: (b, p, i)),
                  pl.BlockSpec((1, S, LANES), lambda b, p, i: (b, 0, p)),
                  pl.BlockSpec((1, LANES, S), lambda b, p, i: (b, p, 0)),
                  pl.BlockSpec((1, nbp, LANES), lambda b, p, i: (b, 0, p))],
        out_specs=pl.BlockSpec((1, tb, LANES), lambda b, p, i: (b, i, p)),
        out_shape=jax.ShapeDtypeStruct((B, S, OD_W), F32),
        scratch_shapes=[pltpu.VMEM((2, LANES, tb), F32), pltpu.VMEM((2, 1, tb), F32),
                        pltpu.VMEM((2, nbp, tb), F32)],
        compiler_params=_cparams(("parallel", "parallel", "arbitrary")),
        name="moba_attention",
    )(q_t, k_rot, v_t, kmean)


def _rglru_kernel(x_ref, gate_ref, cw_ref, cb_ref, wa_ref, ba_ref, wx_ref, bx_ref, lam_ref, o_ref,
                  buf_ref, xprev_ref, hprev_ref, *, ts):
    @pl.when(pl.program_id(1) == 0)
    def _():
        xprev_ref[...] = jnp.zeros_like(xprev_ref)
        hprev_ref[...] = jnp.zeros_like(hprev_ref)

    x = x_ref[0]
    buf_ref[0:8, :] = xprev_ref[...]
    buf_ref[8:8 + ts, :] = x
    xprev_ref[...] = x[ts - 8:ts, :]
    xc = cb_ref[...]
    for kk in range(RG_CONV):
        start = 8 - (RG_CONV - 1) + kk
        xc = xc + cw_ref[kk:kk + 1, :] * buf_ref[start:start + ts, :]
    r = jax.nn.sigmoid(_dot(xc, wa_ref[...]) + ba_ref[...])
    ig = jax.nn.sigmoid(_dot(xc, wx_ref[...]) + bx_ref[...])
    log_a = (-RG_C) * r * _softplus(-lam_ref[...])
    a = jnp.exp(log_a)
    u = jnp.sqrt(-jnp.tanh(log_a) * (a * a + 1.0)) * (ig * xc)
    t_idx = lax.broadcasted_iota(jnp.int32, (ts, 1), 0)
    d = 1
    while d < ts:
        valid = t_idx >= d
        a_sh = pltpu.roll(a, d, axis=0)
        u_sh = pltpu.roll(u, d, axis=0)
        u = jnp.where(valid, a * u_sh, 0.0) + u
        a = jnp.where(valid, a * a_sh, a)
        d *= 2
    h = a * hprev_ref[...] + u
    hprev_ref[...] = h[ts - 1:ts, :]
    o_ref[0] = h * jax.nn.gelu(gate_ref[0], approximate=True)


def rg_lru_mixer(proj3, conv_w, conv_b, wa_bd, ba, wx_bd, bx, lam, ts=256):
    B, S, _ = proj3.shape
    W = conv_w.shape[1]
    full = lambda shape: pl.BlockSpec(shape, lambda b, t: (0,) * len(shape))
    row = lambda v: v.reshape(1, W)
    return pl.pallas_call(
        functools.partial(_rglru_kernel, ts=ts),
        grid=(B, S // ts),
        in_specs=[pl.BlockSpec((1, ts, W), lambda b, t: (b, t, OD_RX // 4)),
                  pl.BlockSpec((1, ts, W), lambda b, t: (b, t, OD_RG // 4)),
                  full((RG_CONV, W)), full((1, W)), full((W, W)), full((1, W)), full((W, W)),
                  full((1, W)), full((1, W))],
        out_specs=pl.BlockSpec((1, ts, W), lambda b, t: (b, t, 0)),
        out_shape=jax.ShapeDtypeStruct((B, S, W), F32),
        scratch_shapes=[pltpu.VMEM((ts + 8, W), F32), pltpu.VMEM((8, W), F32), pltpu.VMEM((1, W), F32)],
        compiler_params=_cparams(("parallel", "arbitrary")),
        name="rg_lru",
    )(proj3, proj3, conv_w, row(conv_b), wa_bd, row(ba), wx_bd, row(bx), row(lam))


def _xattn_kernel(x_ref, kv_ref, g_ref, wq_ref, qn_ref, kn_ref, wo_ref, o_ref):
    x = x_ref[0]
    q = _dot(_rms(x, g_ref[...]), wq_ref[...])
    kv = kv_ref[0]
    hw = XA_HEADS * XA_DH
    outs = []
    for h in range(XA_HEADS):
        hl = slice(h * XA_DH, (h + 1) * XA_DH)
        qh = _rms(q[:, hl], qn_ref[...])
        kh = _rms(kv[:, hl], kn_ref[...])
        vh = kv[:, hw + h * XA_DH: hw + (h + 1) * XA_DH]
        s = _dot_nt(qh, kh) * (XA_DH ** -0.5)
        s = s - jnp.max(s, axis=1, keepdims=True)
        p = jnp.exp(s)
        p = p / jnp.sum(p, axis=1, keepdims=True)
        outs.append(_dot(p, vh))
    o_ref[0] = x + _dot(jnp.concatenate(outs, axis=1), wo_ref[...])


def cross_attention(x3, kv3, g, wq_bf16, qn, kn, wo_bf16, tq=256):
    B, S, D = x3.shape
    M = kv3.shape[1]
    hw = XA_HEADS * XA_DH
    full = lambda shape: pl.BlockSpec(shape, lambda b, i: (0,) * len(shape))
    return pl.pallas_call(
        _xattn_kernel,
        grid=(B, S // tq),
        in_specs=[pl.BlockSpec((1, tq, D), lambda b, i: (b, i, 0)),
                  pl.BlockSpec((1, M, 2 * hw), lambda b, i: (b, 0, 0)),
                  full((1, D)), full((D, hw)), full((1, XA_DH)), full((1, XA_DH)), full((hw, D))],
        out_specs=pl.BlockSpec((1, tq, D), lambda b, i: (b, i, 0)),
        out_shape=jax.ShapeDtypeStruct((B, S, D), F32),
        compiler_params=_cparams(("parallel", "parallel")),
        name="cross_attention",
    )(x3, kv3, g.reshape(1, D), wq_bf16, qn.reshape(1, -1), kn.reshape(1, -1), wo_bf16)


BF16_ROWS = 16


def _top_values(s, k, want_rank=False):
    rows = []
    rank = jnp.full(s.shape, float(k), F32) if want_rank else None
    for r in range(k):
        m = jnp.max(s, axis=0, keepdims=True)
        rows.append(m)
        hit = s == m
        if want_rank:
            rank = jnp.where(hit, float(r), rank)
        s = jnp.where(hit, -jnp.inf, s)
    return rows, rank


def _routing_stages(s1, s2, sub, finish, nstage):
    K = PEER_TOPK
    st = {"a1": s1, "a2": s2, "v1": [], "v2": [], "rank2": jnp.full(s2.shape, float(K), F32)}

    def extract(src, dst, rounds, want_rank):
        def run(nil):
            arr = st[src]
            gone = nil - jnp.inf
            for r in rounds:
                m = jnp.max(arr, axis=0, keepdims=True)
                st[dst].append(m)
                hit = arr == m
                if want_rank:
                    st["rank2"] = jnp.where(hit, float(r), st["rank2"])
                arr = jnp.where(hit, gone, arr)
            st[src] = arr
            st["last"] = m
        return run

    def candidates(nil):
        v1, v2 = st["v1"], st["v2"]
        v1m = jnp.concatenate(v1, axis=0)
        v2m = jnp.concatenate(v2, axis=0)
        groups = [(v1[0] + nil) + v2m]
        for a in range(1, 8):
            groups.append(jnp.where(sub < K // (a + 1), v1[a] + v2m[0:8], -jnp.inf))
        groups.append(v1m[8:16] + v2[0])
        st["cand"] = jnp.concatenate(groups, axis=0)
        st["top"] = []
        st["last"] = v2[0]

    def threshold(nil):
        top = st["top"]
        st["tau"] = top[K - 1]
        z = nil
        for tv in top:
            z = z + jnp.exp(tv - top[0])
        st["z"] = z
        st["last"] = z

    def counts(nil):
        v1, v2, tau = st["v1"], st["v2"], st["tau"]
        count = jnp.zeros(s1.shape, F32)
        for bb in range(K // 2):
            count = count + jnp.where(s1 + v2[bb] >= tau, 1.0, 0.0)
        best = nil
        for bb in range(K // 2, K):
            best = best + jnp.where(v1[0] + v2[bb] >= tau, 1.0, 0.0)
        count = count + jnp.where(s1 == v1[0], best, 0.0)
        finish(count, jnp.exp(s1 - v1[0]) / st["z"], st["rank2"].astype(BF16),
               jnp.exp(s2 - v2[0]).astype(BF16))
        st["last"] = best

    quarter = [range(q * 4, q * 4 + 4) for q in range(4)]
    work = ([extract("a1", "v1", r, False) for r in quarter]
            + [extract("a2", "v2", r, True) for r in quarter]
            + [candidates] + [extract("cand", "top", r, False) for r in quarter]
            + [threshold, counts])
    bounds = [round(j * len(work) / nstage) for j in range(nstage + 1)]

    def stage(j):
        def run(nil):
            for piece in work[bounds[j]:bounds[j + 1]]:
                piece(nil)
            return st["last"]
        return run

    return [stage(j) for j in range(nstage)]


def _peer_kernel(x_ref, xnext_ref, g_ref, wq_ref, sk_ref, u_ref, vt_ref, o_ref,
                 xn_ref, xnt_ref, c_ref, a_ref, r2_ref, e2_ref, acc_ref, *, ib, isub):
    t = pl.program_id(0)
    e = pl.program_id(1)
    nk = PEER_NKEYS
    tq = x_ref.shape[0]
    slot = t % 2
    nslot = 1 - slot
    sub = lax.broadcasted_iota(jnp.int32, (8, 1), 0)
    nsub = ib // isub

    def prepare(src_ref, s):
        xn32 = _rms(src_ref[...], g_ref[...])
        xn_ref[...] = xn32.astype(BF16)
        xnt_ref[s] = xn32.T.astype(BF16)

    def route_scores(h):
        q = jnp.dot(xn_ref[...], wq_ref[h], preferred_element_type=F32)
        return _dot_nt(sk_ref[2 * h], q[:, :nk]), _dot_nt(sk_ref[2 * h + 1], q[:, nk:])

    def table_writer(h, s):
        def finish(count, amp, rank2, e2):
            c_ref[s, h] = count
            a_ref[s, h] = amp
            r2_ref[s, h] = rank2
            e2_ref[s, h] = e2
        return finish

    @pl.when(jnp.logical_and(t == 0, e == 0))
    def _():
        prepare(x_ref, 0)

        def one(h, c):
            s1, s2 = route_scores(h)
            for run in _routing_stages(s1, s2, sub, table_writer(h, 0), 1):
                run(jnp.zeros((1, tq), F32))
            return c

        lax.fori_loop(0, PEER_HEADS, one, 0)

    @pl.when(e == 0)
    def _():
        acc_ref[...] = jnp.zeros_like(acc_ref)
        prepare(xnext_ref, nslot)

    ngrp = nk // BF16_ROWS
    xnt = xnt_ref[slot]
    first = pl.multiple_of(e * ib, ib)
    step_c = [c_ref[slot, h, pl.ds(first, ib), :] for h in range(PEER_HEADS)]
    step_a = [a_ref[slot, h, pl.ds(first, ib), :] for h in range(PEER_HEADS)]

    def expert_matmul(sc):
        lo = sc * isub * nk
        u_sub = pltpu.bitcast(u_ref[lo // 2:(lo + isub * nk) // 2, :], BF16)
        return jnp.dot(u_sub, xnt, preferred_element_type=F32)

    hid_next = expert_matmul(0)
    next_s1, next_s2 = route_scores(e)
    stages = _routing_stages(next_s1, next_s2, sub, table_writer(e, nslot), nsub)

    total = None
    after = jnp.zeros((1, tq), F32)
    for sc in range(nsub):
        zero = jnp.broadcast_to(after, (BF16_ROWS, tq)).astype(BF16)
        lo = sc * isub * nk
        hid_sub = hid_next
        if sc + 1 < nsub:
            hid_next = expert_matmul(sc + 1)
        pieces = []
        for ii in range(isub):
            il = sc * isub + ii
            hid = hid_sub[ii * nk:(ii + 1) * nk, :]
            act = (0.5 * hid * (1.0 + lax.erf(hid * np.float32(np.sqrt(0.5))))).astype(BF16)
            w = [zero] * ngrp
            for h in range(PEER_HEADS):
                cnt = jnp.broadcast_to(step_c[h][il:il + 1, :], (BF16_ROWS, tq)).astype(BF16)
                amp = jnp.broadcast_to(step_a[h][il:il + 1, :], (BF16_ROWS, tq)).astype(BF16)
                for gi in range(ngrp):
                    rows = slice(gi * BF16_ROWS, (gi + 1) * BF16_ROWS)
                    w[gi] = w[gi] + jnp.where(r2_ref[slot, h, rows, :] < cnt,
                                              e2_ref[slot, h, rows, :] * amp, zero)
            pieces += [w[gi] * act[gi * BF16_ROWS:(gi + 1) * BF16_ROWS, :] for gi in range(ngrp)]
        after = stages[sc](pieces[-1][0:1, :].astype(F32) * 0.0) * 0.0
        vt_sub = pltpu.bitcast(vt_ref[0, :, lo:lo + isub * nk], BF16)
        part = jnp.dot(vt_sub, jnp.concatenate(pieces, axis=0), preferred_element_type=F32)
        total = part if total is None else total + part
    acc_ref[...] += total

    @pl.when(e == pl.num_programs(1) - 1)
    def _():
        o_ref[...] = x_ref[...] + acc_ref[...].T


def _pair_rows_kernel(x_ref, o_ref, *, transpose):
    x = x_ref[...]
    if transpose:
        o_ref[0] = pltpu.bitcast(x.T.astype(BF16), jnp.uint32)
    else:
        o_ref[...] = pltpu.bitcast(x.astype(BF16), jnp.uint32)


def _pair_rows(w, transpose=False, tr=512):
    R, C = w.shape
    if transpose:
        per = PEER_STEP_EXPERTS // tr
        in_spec = pl.BlockSpec((tr, C), lambda i: (i, 0))
        out_spec = pl.BlockSpec((1, C // 2, tr), lambda i: (i // per, 0, i % per))
        out_shape = jax.ShapeDtypeStruct((R // PEER_STEP_EXPERTS, C // 2, PEER_STEP_EXPERTS), jnp.uint32)
    else:
        in_spec = pl.BlockSpec((tr, C), lambda i: (i, 0))
        out_spec = pl.BlockSpec((tr // 2, C), lambda i: (i, 0))
        out_shape = jax.ShapeDtypeStruct((R // 2, C), jnp.uint32)
    return pl.pallas_call(
        functools.partial(_pair_rows_kernel, transpose=transpose),
        grid=(R // tr,), in_specs=[in_spec], out_specs=out_spec, out_shape=out_shape,
        compiler_params=_cparams(("parallel",)),
        name="pair_rows",
    )(w)


def peer_ffn(x2d, g, wq3_bf16, sk_bf16, u_pairs, vt_pairs, tq=256, isub=2):
    T, D = x2d.shape
    H = PEER_HEADS
    ne = H
    ib = PEER_NKEYS // ne
    ec = ib * PEER_NKEYS
    nt = T // tq
    full = lambda shape: pl.BlockSpec(shape, lambda i, e: (0,) * len(shape))
    words = pltpu.VMEM((2, H, PEER_NKEYS, tq), F32)
    halves = pltpu.VMEM((2, H, PEER_NKEYS, tq), BF16)
    return pl.pallas_call(
        functools.partial(_peer_kernel, ib=ib, isub=isub),
        grid=(nt, ne),
        in_specs=[pl.BlockSpec((tq, D), lambda i, e: (i, 0)),
                  pl.BlockSpec((tq, D), lambda i, e: (jnp.minimum(i + 1, nt - 1), 0)),
                  full((1, D)), full((H, D, 2 * PEER_NKEYS)), full((2 * H, PEER_NKEYS, PEER_NKEYS)),
                  pl.BlockSpec((ec // 2, D), lambda i, e: (e, 0)),
                  pl.BlockSpec((1, D // 2, ec), lambda i, e: (e, 0, 0))],
        out_specs=pl.BlockSpec((tq, D), lambda i, e: (i, 0)),
        out_shape=jax.ShapeDtypeStruct((T, D), F32),
        scratch_shapes=[pltpu.VMEM((tq, D), BF16), pltpu.VMEM((2, D, tq), BF16),
                        words, words, halves, halves, pltpu.VMEM((D, tq), F32)],
        compiler_params=_cparams(("arbitrary", "arbitrary")),
        name="peer_ffn",
    )(x2d, x2d, g.reshape(1, D), wq3_bf16, sk_bf16, u_pairs, vt_pairs)


def _block_diag(w):
    G, n, _ = w.shape
    eye = jnp.eye(G, dtype=w.dtype)
    return (eye[:, None, :, None] * w[:, :, None, :]).reshape(G * n, G * n)


def even_layer(x2d, B, S, norm, w_in, gate_up, gate_b, out_norm, w_out):
    D = x2d.shape[1]
    gg0 = EV_GG * LANES
    wp = jnp.concatenate([w_in[:, :1536], w_in[:, 1536 + GLA_RANK:], w_in[:, 1536:1536 + GLA_RANK],
                          jnp.zeros((D, EV_COLS - gg0 - GLA_RANK), w_in.dtype)], axis=1).astype(BF16)
    proj = norm_proj(x2d, norm, wp).reshape(B, S, EV_COLS)
    gup = jnp.concatenate([gate_up, jnp.zeros((LANES - GLA_RANK, gate_up.shape[1]), gate_up.dtype)], axis=0)
    a_out = gla_mixer(proj, gup.astype(BF16), gate_b, out_norm)
    b_out = sb_attention(proj)
    return out_proj(x2d, a_out.reshape(B * S, -1), b_out.reshape(B * S, -1), w_out.astype(BF16))


def odd_layer(x2d, B, S, positions, norm, w_in, q_norm, k_norm, conv_w, conv_b, wa, ba, wx, bx, lam, w_out):
    proj = norm_proj(x2d, norm, w_in.astype(BF16)).reshape(B, S, -1)
    half = HEAD_DIM // 2
    inv_freq = ROPE_THETA ** (-jnp.arange(0, HEAD_DIM, 2, dtype=F32) / HEAD_DIM)
    inv_tile = jnp.tile(inv_freq, LANES // half).reshape(1, LANES)
    seg = np.arange(OD_W) // HEAD_DIM
    blockdiag = jnp.asarray(seg[:, None] == seg[None, :], dtype=BF16)
    q_t, k_rot, v_t, kmean = moba_prep(proj, positions.reshape(B, S, 1), inv_tile,
                                       jnp.tile(q_norm, OD_W // HEAD_DIM).reshape(1, OD_W),
                                       jnp.tile(k_norm, OD_W // HEAD_DIM).reshape(1, OD_W), blockdiag)
    c_out = moba_attention(q_t, k_rot, v_t, kmean.reshape(B, S // MOBA_BLOCK, OD_W))
    d_out = rg_lru_mixer(proj, conv_w, conv_b, _block_diag(wa).astype(BF16), ba,
                         _block_diag(wx).astype(BF16), bx, lam)
    return out_proj(x2d, c_out.reshape(B * S, -1), d_out.reshape(B * S, -1), w_out.astype(BF16))


def kernel(x, mem, positions, ev_norm, ev_w_in, ev_gla_gate_up, ev_gla_gate_b, ev_gla_out_norm, ev_w_out, od_norm, od_w_in, od_q_norm, od_k_norm, od_conv_w, od_conv_b, od_gate_a_w, od_gate_a_b, od_gate_x_w, od_gate_x_b, od_lambda, od_w_out, xa_norm, xa_mem_norm, xa_wq, xa_wkv, xa_q_norm, xa_k_norm, xa_wo, ffn_norm, peer_wq, peer_subkeys, peer_u, peer_v):
    B, S, D = x.shape
    M = mem.shape[1]
    depth = xa_norm.shape[0]
    x2d = x.reshape(B * S, D)
    mem2d = mem.reshape(B * M, D)
    for l in range(depth):
        if l % 2 == 0:
            e = l // 2
            x2d = even_layer(x2d, B, S, ev_norm[e], ev_w_in[e], ev_gla_gate_up[e], ev_gla_gate_b[e],
                             ev_gla_out_norm[e], ev_w_out[e])
        else:
            o = l // 2
            x2d = odd_layer(x2d, B, S, positions, od_norm[o], od_w_in[o], od_q_norm[o], od_k_norm[o],
                            od_conv_w[o], od_conv_b[o], od_gate_a_w[o], od_gate_a_b[o],
                            od_gate_x_w[o], od_gate_x_b[o], od_lambda[o], od_w_out[o])
        kv = norm_proj(mem2d, xa_mem_norm[l], xa_wkv[l].astype(BF16))
        x2d = cross_attention(x2d.reshape(B, S, D), kv.reshape(B, M, -1), xa_norm[l],
                              xa_wq[l].astype(BF16), xa_q_norm[l], xa_k_norm[l],
                              xa_wo[l].astype(BF16)).reshape(B * S, D)
        H = PEER_HEADS
        wq3 = peer_wq[l].reshape(D, H, 2 * PEER_NKEYS).transpose(1, 0, 2).astype(BF16)
        sk = peer_subkeys[l].reshape(2 * H, PEER_NKEYS, -1).astype(BF16)
        x2d = peer_ffn(x2d, ffn_norm[l], wq3, sk, _pair_rows(peer_u[l]),
                       _pair_rows(peer_v[l], transpose=True))
    return x2d.reshape(B, S, D)
```

```python
import functools

import jax
import jax.numpy as jnp
import numpy as np
from jax import lax
from jax.experimental import pallas as pl
from jax.experimental.pallas import tpu as pltpu

F32 = jnp.float32
BF16 = jnp.bfloat16

EPS = 1e-6
ROPE_THETA = 10000.0
LANES = 128
HEAD_DIM = 64
GLA_HEADS = 4
GLA_DV = 128
GLA_CHUNK = 64
GLA_TAU = 16.0
GLA_RANK = 16
MOBA_BLOCK = 256
MOBA_TOPK = 3
RG_C = 8.0
RG_CONV = 4
XA_HEADS = 4
XA_DH = 128
PEER_HEADS = 8
PEER_NKEYS = 128
PEER_TOPK = 16
NEG = -1e30
F32_LOG_TINY = -104.0
VMEM_LIMIT = 56 * 1024 * 1024


def _cparams(sem):
    return pltpu.CompilerParams(dimension_semantics=sem, vmem_limit_bytes=VMEM_LIMIT)


def _dot(a, b):
    return jnp.dot(a.astype(BF16), b.astype(BF16), preferred_element_type=F32)


def _dot_nt(a, b):
    return lax.dot_general(a.astype(BF16), b.astype(BF16), (((1,), (1,)), ((), ())),
                           preferred_element_type=F32)


def _dot_tn(a, b):
    return lax.dot_general(a.astype(BF16), b.astype(BF16), (((0,), (0,)), ((), ())),
                           preferred_element_type=F32)


def _split2(a):
    hi = a.astype(BF16)
    lo = (a - hi.astype(F32)).astype(BF16)
    return hi, lo


def _dot_exact_rhs(a, m_bf16):
    hi, lo = _split2(a)
    return (jnp.dot(hi, m_bf16, preferred_element_type=F32)
            + jnp.dot(lo, m_bf16, preferred_element_type=F32))


def _dot_exact_lhs(m_bf16, a):
    hi, lo = _split2(a)
    return (jnp.dot(m_bf16, hi, preferred_element_type=F32)
            + jnp.dot(m_bf16, lo, preferred_element_type=F32))


def _rms(x, g):
    return x * lax.rsqrt(jnp.mean(x * x, axis=-1, keepdims=True) + EPS) * g


def _softplus(z):
    return jnp.maximum(z, 0.0) + jnp.log1p(jnp.exp(-jnp.abs(z)))


def _head_mask(hh):
    lane = lax.broadcasted_iota(jnp.int32, (1, LANES), 1)
    return ((lane // HEAD_DIM) == hh).astype(F32)


def _norm_proj_kernel(x_ref, g_ref, w_ref, o_ref):
    xn = _rms(x_ref[...], g_ref[...])
    o_ref[...] = jnp.dot(xn.astype(BF16), w_ref[...], preferred_element_type=F32)


def norm_proj(x2d, g, w_bf16, tm=256):
    T, D = x2d.shape
    N = w_bf16.shape[1]
    return pl.pallas_call(
        _norm_proj_kernel,
        grid=(T // tm,),
        in_specs=[pl.BlockSpec((tm, D), lambda i: (i, 0)),
                  pl.BlockSpec((1, D), lambda i: (0, 0)),
                  pl.BlockSpec((D, N), lambda i: (0, 0))],
        out_specs=pl.BlockSpec((tm, N), lambda i: (i, 0)),
        out_shape=jax.ShapeDtypeStruct((T, N), F32),
        compiler_params=_cparams(("parallel",)),
        name="norm_proj",
    )(x2d, g.reshape(1, D), w_bf16)


def _out_proj_kernel(x_ref, a_ref, b_ref, w_ref, o_ref):
    ka = a_ref.shape[1]
    o_ref[...] = (x_ref[...] + _dot(a_ref[...], w_ref[0:ka, :])
                  + _dot(b_ref[...], w_ref[ka:, :]))


def out_proj(x2d, a2d, b2d, w_bf16, tm=512):
    T, D = x2d.shape
    ka, kb = a2d.shape[1], b2d.shape[1]
    return pl.pallas_call(
        _out_proj_kernel,
        grid=(T // tm,),
        in_specs=[pl.BlockSpec((tm, D), lambda i: (i, 0)),
                  pl.BlockSpec((tm, ka), lambda i: (i, 0)),
                  pl.BlockSpec((tm, kb), lambda i: (i, 0)),
                  pl.BlockSpec((ka + kb, D), lambda i: (0, 0))],
        out_specs=pl.BlockSpec((tm, D), lambda i: (i, 0)),
        out_shape=jax.ShapeDtypeStruct((T, D), F32),
        compiler_params=_cparams(("parallel",)),
        name="out_proj",
    )(x2d, a2d, b2d, w_bf16)


EV_Q, EV_K, EV_V, EV_R, EV_SQ, EV_SK, EV_SV, EV_GG = 0, 2, 4, 8, 12, 16, 20, 24
EV_COLS = 25 * LANES


def _gla_kernel(q_ref, k_ref, v_ref, r_ref, gg_ref, gup_ref, gb_ref, onorm_ref, o_ref, state_ref,
                *, ts):
    C = GLA_CHUNK

    @pl.when(pl.program_id(1) == 0)
    def _():
        state_ref[...] = jnp.zeros_like(state_ref)

    row = lax.broadcasted_iota(jnp.int32, (C, C), 0)
    col = lax.broadcasted_iota(jnp.int32, (C, C), 1)
    causal = row >= col
    tri = causal.astype(BF16)
    masks = [_head_mask(0), _head_mask(1)]
    scale = HEAD_DIM ** -0.5
    onorm = onorm_ref[...]

    for ci in range(ts // C):
        sl = slice(ci * C, (ci + 1) * C)
        pre = _dot(gg_ref[0, sl, :], gup_ref[...]) + gb_ref[...]
        g = (jnp.minimum(pre, 0.0) - jnp.log1p(jnp.exp(-jnp.abs(pre)))) * (1.0 / GLA_TAU)
        b = _dot_exact_lhs(tri, g)
        bmid = b[C // 2 - 1:C // 2, :]
        blast = b[C - 1:C, :]
        q = q_ref[0, sl, :] * scale
        k = k_ref[0, sl, :]
        qd = q * jnp.exp(b - bmid)
        kd = k * jnp.exp(bmid - b)
        qe = q * jnp.exp(b)
        kdec = k * jnp.exp(blast - b)
        eb_last = jnp.exp(blast)
        for p in range(GLA_HEADS // 2):
            lanes = slice(p * LANES, (p + 1) * LANES)
            st = state_ref[p]
            new_st = st * eb_last[:, lanes]
            for hh in range(2):
                h = 2 * p + hh
                m = masks[hh]
                hl = slice(h * GLA_DV, (h + 1) * GLA_DV)
                a = _dot_nt(qd[:, lanes] * m, kd[:, lanes])
                a = jnp.where(causal, a, 0.0)
                v_h = v_ref[0, sl, hl]
                o = _dot(a, v_h) + _dot_nt(qe[:, lanes] * m, st)
                o = _rms(o, onorm)
                r_h = r_ref[0, sl, hl]
                o_ref[0, sl, hl] = o * (r_h * jax.nn.sigmoid(r_h))
                new_st = new_st + _dot_tn(v_h, kdec[:, lanes] * m)
            state_ref[p] = new_st


def gla_mixer(proj3, gate_up_pad, gate_b, out_norm, ts=256):
    B, S, _ = proj3.shape
    nh = GLA_HEADS
    w = nh * GLA_DV

    def col(blk_w, tile):
        idx = tile * LANES // blk_w
        return pl.BlockSpec((1, ts, blk_w), lambda b, c: (b, c, idx))

    return pl.pallas_call(
        functools.partial(_gla_kernel, ts=ts),
        grid=(B, S // ts),
        in_specs=[col(nh * HEAD_DIM, EV_Q), col(nh * HEAD_DIM, EV_K), col(w, EV_V), col(w, EV_R),
                  col(LANES, EV_GG),
                  pl.BlockSpec((LANES, nh * HEAD_DIM), lambda b, c: (0, 0)),
                  pl.BlockSpec((1, nh * HEAD_DIM), lambda b, c: (0, 0)),
                  pl.BlockSpec((1, GLA_DV), lambda b, c: (0, 0))],
        out_specs=pl.BlockSpec((1, ts, w), lambda b, c: (b, c, 0)),
        out_shape=jax.ShapeDtypeStruct((B, S, w), F32),
        scratch_shapes=[pltpu.VMEM((nh // 2, GLA_DV, LANES), F32)],
        compiler_params=_cparams(("parallel", "arbitrary")),
        name="gla",
    )(proj3, proj3, proj3, proj3, proj3, gate_up_pad, gate_b.reshape(1, -1), out_norm.reshape(1, -1))


def _sb_kernel(q_ref, k_ref, v_ref, o_ref, acc_ref, carry_ref, *, tq):
    i = pl.program_id(2)
    q = q_ref[0] * (HEAD_DIM ** -0.5)
    masks = [_head_mask(0), _head_mask(1)]
    qh = [(q * m).astype(BF16) for m in masks]
    row = lax.broadcasted_iota(jnp.int32, (tq, tq), 0)
    col = lax.broadcasted_iota(jnp.int32, (tq, tq), 1)
    upper = (row > col).astype(BF16)
    past = col < row
    acc_ref[...] = jnp.zeros_like(acc_ref)
    carry_ref[...] = jnp.zeros_like(carry_ref)

    def tile(j, diag):
        off = pl.multiple_of(j * tq, tq)
        k = k_ref[0, pl.ds(off, tq), :].astype(BF16)
        v = v_ref[0, pl.ds(off, tq), :].astype(BF16)
        for hh in range(2):
            z = lax.dot_general(qh[hh], k, (((1,), (1,)), ((), ())), preferred_element_type=F32)
            sp = _softplus(z)
            log_rem = jnp.where(past, -sp, 0.0) if diag else -sp
            carry = carry_ref[hh]
            after = _dot_exact_rhs(log_rem, upper) + carry
            w = jnp.exp((z - sp) + after)
            if diag:
                w = jnp.where(past, w, 0.0)
            acc_ref[hh] += jnp.dot(w.astype(BF16), v, preferred_element_type=F32)
            carry_ref[hh] = carry + jnp.sum(log_rem, axis=1, keepdims=True)

    tile(i, True)

    def live(c):
        j, worst = c
        return jnp.logical_and(j >= 0, worst > F32_LOG_TINY)

    def body(c):
        j, _ = c
        tile(j, False)
        return j - 1, jnp.max(carry_ref[...])

    lax.while_loop(live, body, (i - 1, jnp.max(carry_ref[...])))
    o_ref[0] = acc_ref[0] * masks[0] + acc_ref[1] * masks[1]


def sb_attention(proj3, tq=256):
    B, S, _ = proj3.shape
    npairs = 4
    return pl.pallas_call(
        functools.partial(_sb_kernel, tq=tq),
        grid=(B, npairs, S // tq),
        in_specs=[pl.BlockSpec((1, tq, LANES), lambda b, p, i: (b, i, EV_SQ + p)),
                  pl.BlockSpec((1, S, LANES), lambda b, p, i: (b, 0, EV_SK + p)),
                  pl.BlockSpec((1, S, LANES), lambda b, p, i: (b, 0, EV_SV + p))],
        out_specs=pl.BlockSpec((1, tq, LANES), lambda b, p, i: (b, i, p)),
        out_shape=jax.ShapeDtypeStruct((B, S, npairs * LANES), F32),
        scratch_shapes=[pltpu.VMEM((2, tq, LANES), F32), pltpu.VMEM((2, tq, 1), F32)],
        compiler_params=_cparams(("parallel", "parallel", "arbitrary")),
        name="sb_attention",
    )(proj3, proj3, proj3)


OD_Q, OD_K, OD_V, OD_RX, OD_RG = 0, 4, 8, 12, 16
OD_W = 4 * LANES


def _moba_prep_kernel(q_ref, k_ref, v_ref, pos_ref, invf_ref, qn_ref, kn_ref, bd_ref,
                      qt_ref, ko_ref, vt_ref, km_ref):
    ang = pos_ref[0].astype(F32) * invf_ref[...]
    cos1, sin1 = jnp.cos(ang), jnp.sin(ang)
    lane = lax.broadcasted_iota(jnp.int32, (1, LANES), 1)
    first_half = (lane % HEAD_DIM) < (HEAD_DIM // 2)
    bd = bd_ref[...]

    def norm_rope(x, gain):
        ms = _dot_exact_rhs(x * x, bd) * (1.0 / HEAD_DIM)
        xn = x * lax.rsqrt(ms + EPS) * gain
        outs = []
        for t in range(OD_W // LANES):
            xb = xn[:, t * LANES:(t + 1) * LANES]
            up = pltpu.roll(xb, LANES - HEAD_DIM // 2, axis=1)
            dn = pltpu.roll(xb, HEAD_DIM // 2, axis=1)
            outs.append(xb * cos1 + jnp.where(first_half, -up, dn) * sin1)
        return jnp.concatenate(outs, axis=1)

    qt_ref[0] = norm_rope(q_ref[0], qn_ref[...]).T.astype(BF16)
    kr = norm_rope(k_ref[0], kn_ref[...])
    ko_ref[0] = kr.astype(BF16)
    km_ref[0, 0] = jnp.mean(kr, axis=0, keepdims=True)
    vt_ref[0] = v_ref[0].T.astype(BF16)


def moba_prep(proj3, pos3, inv_freq_tile, qn_tile, kn_tile, blockdiag):
    B, S, _ = proj3.shape
    tb = MOBA_BLOCK
    nb = S // tb
    full = lambda shape: pl.BlockSpec(shape, lambda b, i: (0,) * len(shape))
    return pl.pallas_call(
        _moba_prep_kernel,
        grid=(B, nb),
        in_specs=[pl.BlockSpec((1, tb, OD_W), lambda b, i: (b, i, OD_Q // 4)),
                  pl.BlockSpec((1, tb, OD_W), lambda b, i: (b, i, OD_K // 4)),
                  pl.BlockSpec((1, tb, OD_W), lambda b, i: (b, i, OD_V // 4)),
                  pl.BlockSpec((1, tb, 1), lambda b, i: (b, i, 0)),
                  full((1, LANES)), full((1, OD_W)), full((1, OD_W)), full((OD_W, OD_W))],
        out_specs=[pl.BlockSpec((1, OD_W, tb), lambda b, i: (b, 0, i)),
                   pl.BlockSpec((1, tb, OD_W), lambda b, i: (b, i, 0)),
                   pl.BlockSpec((1, OD_W, tb), lambda b, i: (b, 0, i)),
                   pl.BlockSpec((1, 1, 1, OD_W), lambda b, i: (b, i, 0, 0))],
        out_shape=[jax.ShapeDtypeStruct((B, OD_W, S), BF16),
                   jax.ShapeDtypeStruct((B, S, OD_W), BF16),
                   jax.ShapeDtypeStruct((B, OD_W, S), BF16),
                   jax.ShapeDtypeStruct((B, nb, 1, OD_W), F32)],
        compiler_params=_cparams(("parallel", "parallel")),
        name="moba_prep",
    )(proj3, proj3, proj3, pos3, inv_freq_tile, qn_tile, kn_tile, blockdiag)


def _moba_kernel(qt_ref, k_ref, vt_ref, km_ref, o_ref, acc_ref, m_ref, sel_ref):
    tb = MOBA_BLOCK
    nbp = km_ref.shape[1]
    i = pl.program_id(2)
    qt = qt_ref[0]
    km = km_ref[0].astype(BF16)
    chan = lax.broadcasted_iota(jnp.int32, (LANES, 1), 0)
    in_head = [(chan // HEAD_DIM) == hh for hh in range(2)]
    blk = lax.broadcasted_iota(jnp.int32, (nbp, tb), 0).astype(F32)
    own = i.astype(F32)
    key = lax.broadcasted_iota(jnp.int32, (tb, tb), 0)
    qry = lax.broadcasted_iota(jnp.int32, (tb, tb), 1)
    visible = key <= qry
    off_own = pl.multiple_of(i * tb, tb)
    k_own = k_ref[0, pl.ds(off_own, tb), :]
    vt_own = vt_ref[0, :, pl.ds(off_own, tb)]
    zero = jnp.zeros_like(qt)
    one = jnp.ones_like(qt)
    qs = []
    for hh in range(2):
        qm = jnp.where(in_head[hh], qt, zero)
        gate = jnp.dot(km, qm, preferred_element_type=F32)
        g = jnp.where(blk < own, gate, -jnp.inf)
        sel = jnp.zeros((nbp, tb), F32)
        for r in range(MOBA_TOPK):
            mx = jnp.max(g, axis=0, keepdims=True)
            idx = jnp.min(jnp.where(g == mx, blk, float(nbp)), axis=0, keepdims=True)
            hit = blk == idx
            keep = jnp.where(i > r, 1.0, 0.0)
            sel = sel + jnp.where(hit, keep, 0.0)
            g = jnp.where(hit, -jnp.inf, g)
        sel_ref[hh] = sel
        qb = qm * (HEAD_DIM ** -0.5)
        qs.append(qb)
        s = jnp.dot(k_own, qb, preferred_element_type=F32)
        s = jnp.where(visible, s, NEG)
        m0 = jnp.max(s, axis=0, keepdims=True)
        p = jnp.exp(s - m0)
        m_ref[hh] = m0
        acc_ref[hh] = jnp.dot(jnp.where(in_head[hh], vt_own, one), p.astype(BF16),
                              preferred_element_type=F32)

    def past_blocks(n, nblk):
        off = pl.multiple_of(n * tb, tb)
        k = k_ref[0, pl.ds(off, nblk * tb), :]
        vt = vt_ref[0, :, pl.ds(off, nblk * tb)]
        scores = [jnp.dot(k, qs[hh], preferred_element_type=F32) for hh in range(2)]
        probs, alphas = [], []
        for hh in range(2):
            parts = []
            for j in range(nblk):
                chosen = sel_ref[hh, pl.ds(n + j, 1), :]
                parts.append(jnp.where(chosen > 0.0, scores[hh][j * tb:(j + 1) * tb, :], NEG))
            m_old = m_ref[hh]
            m_new = m_old
            for part in parts:
                m_new = jnp.maximum(m_new, jnp.max(part, axis=0, keepdims=True))
            alphas.append(jnp.exp(m_old - m_new))
            probs.append(jnp.concatenate([jnp.exp(part - m_new).astype(BF16) for part in parts], axis=0))
            m_ref[hh] = m_new
        ones_v = jnp.ones_like(vt)
        pv = [jnp.dot(jnp.where(in_head[hh], vt, ones_v), probs[hh], preferred_element_type=F32)
              for hh in range(2)]
        for hh in range(2):
            acc_ref[hh] = acc_ref[hh] * alphas[hh] + pv[hh]

    def four_blocks(t, c):
        past_blocks(4 * t, 4)
        return c

    lax.fori_loop(0, i // 4, four_blocks, 0)
    rem = i % 4

    @pl.when(rem >= 2)
    def _():
        past_blocks(i - rem, 2)

    @pl.when(rem % 2 == 1)
    def _():
        past_blocks(i - 1, 1)

    a0, a1 = acc_ref[0], acc_ref[1]
    den0 = a0[HEAD_DIM:HEAD_DIM + 1, :]
    den1 = a1[0:1, :]
    o_ref[0] = jnp.where(in_head[0], a0 / den0, a1 / den1).T


def moba_attention(q_t, k_rot, v_t, kmean):
    B, S, _ = k_rot.shape
    tb = MOBA_BLOCK
    nb = S // tb
    npairs = OD_W // LANES
    nbp = -(-nb // BF16_ROWS) * BF16_ROWS
    kmean = jnp.pad(kmean, ((0, 0), (0, nbp - nb), (0, 0)))
    return pl.pallas_call(
        _moba_kernel,
        grid=(B, npairs, nb),
        in_specs=[pl.BlockSpec((1, LANES, tb), lambda b, p, i: (b, p, i)),
                  pl.BlockSpec((1, S, LANES), lambda b, p, i: (b, 0, p)),
                  pl.BlockSpec((1, LANES, S), lambda b, p, i: (b, p, 0)),
                  pl.BlockSpec((1, nbp, LANES), lambda b, p, i: (b, 0, p))],
        out_specs=pl.BlockSpec((1, tb, LANES), lambda b, p, i: (b, i, p)),
        out_shape=jax.ShapeDtypeStruct((B, S, OD_W), F32),
        scratch_shapes=[pltpu.VMEM((2, LANES, tb), F32), pltpu.VMEM((2, 1, tb), F32),
                        pltpu.VMEM((2, nbp, tb), F32)],
        compiler_params=_cparams(("parallel", "parallel", "arbitrary")),
        name="moba_attention",
    )(q_t, k_rot, v_t, kmean)


def _rglru_kernel(x_ref, gate_ref, cw_ref, cb_ref, wa_ref, ba_ref, wx_ref, bx_ref, lam_ref, o_ref,
                  buf_ref, xprev_ref, hprev_ref, *, ts):
    @pl.when(pl.program_id(1) == 0)
    def _():
        xprev_ref[...] = jnp.zeros_like(xprev_ref)
        hprev_ref[...] = jnp.zeros_like(hprev_ref)

    x = x_ref[0]
    buf_ref[0:8, :] = xprev_ref[...]
    buf_ref[8:8 + ts, :] = x
    xprev_ref[...] = x[ts - 8:ts, :]
    xc = cb_ref[...]
    for kk in range(RG_CONV):
        start = 8 - (RG_CONV - 1) + kk
        xc = xc + cw_ref[kk:kk + 1, :] * buf_ref[start:start + ts, :]
    r = jax.nn.sigmoid(_dot(xc, wa_ref[...]) + ba_ref[...])
    ig = jax.nn.sigmoid(_dot(xc, wx_ref[...]) + bx_ref[...])
    log_a = (-RG_C) * r * _softplus(-lam_ref[...])
    a = jnp.exp(log_a)
    u = jnp.sqrt(-jnp.tanh(log_a) * (a * a + 1.0)) * (ig * xc)
    t_idx = lax.broadcasted_iota(jnp.int32, (ts, 1), 0)
    d = 1
    while d < ts:
        valid = t_idx >= d
        a_sh = pltpu.roll(a, d, axis=0)
        u_sh = pltpu.roll(u, d, axis=0)
        u = jnp.where(valid, a * u_sh, 0.0) + u
        a = jnp.where(valid, a * a_sh, a)
        d *= 2
    h = a * hprev_ref[...] + u
    hprev_ref[...] = h[ts - 1:ts, :]
    o_ref[0] = h * jax.nn.gelu(gate_ref[0], approximate=True)


def rg_lru_mixer(proj3, conv_w, conv_b, wa_bd, ba, wx_bd, bx, lam, ts=256):
    B, S, _ = proj3.shape
    W = conv_w.shape[1]
    full = lambda shape: pl.BlockSpec(shape, lambda b, t: (0,) * len(shape))
    row = lambda v: v.reshape(1, W)
    return pl.pallas_call(
        functools.partial(_rglru_kernel, ts=ts),
        grid=(B, S // ts),
        in_specs=[pl.BlockSpec((1, ts, W), lambda b, t: (b, t, OD_RX // 4)),
                  pl.BlockSpec((1, ts, W), lambda b, t: (b, t, OD_RG // 4)),
                  full((RG_CONV, W)), full((1, W)), full((W, W)), full((1, W)), full((W, W)),
                  full((1, W)), full((1, W))],
        out_specs=pl.BlockSpec((1, ts, W), lambda b, t: (b, t, 0)),
        out_shape=jax.ShapeDtypeStruct((B, S, W), F32),
        scratch_shapes=[pltpu.VMEM((ts + 8, W), F32), pltpu.VMEM((8, W), F32), pltpu.VMEM((1, W), F32)],
        compiler_params=_cparams(("parallel", "arbitrary")),
        name="rg_lru",
    )(proj3, proj3, conv_w, row(conv_b), wa_bd, row(ba), wx_bd, row(bx), row(lam))


def _xattn_kernel(x_ref, kv_ref, g_ref, wq_ref, qn_ref, kn_ref, wo_ref, o_ref):
    x = x_ref[0]
    q = _dot(_rms(x, g_ref[...]), wq_ref[...])
    kv = kv_ref[0]
    hw = XA_HEADS * XA_DH
    outs = []
    for h in range(XA_HEADS):
        hl = slice(h * XA_DH, (h + 1) * XA_DH)
        qh = _rms(q[:, hl], qn_ref[...])
        kh = _rms(kv[:, hl], kn_ref[...])
        vh = kv[:, hw + h * XA_DH: hw + (h + 1) * XA_DH]
        s = _dot_nt(qh, kh) * (XA_DH ** -0.5)
        s = s - jnp.max(s, axis=1, keepdims=True)
        p = jnp.exp(s)
        p = p / jnp.sum(p, axis=1, keepdims=True)
        outs.append(_dot(p, vh))
    o_ref[0] = x + _dot(jnp.concatenate(outs, axis=1), wo_ref[...])


def cross_attention(x3, kv3, g, wq_bf16, qn, kn, wo_bf16, tq=256):
    B, S, D = x3.shape
    M = kv3.shape[1]
    hw = XA_HEADS * XA_DH
    full = lambda shape: pl.BlockSpec(shape, lambda b, i: (0,) * len(shape))
    return pl.pallas_call(
        _xattn_kernel,
        grid=(B, S // tq),
        in_specs=[pl.BlockSpec((1, tq, D), lambda b, i: (b, i, 0)),
                  pl.BlockSpec((1, M, 2 * hw), lambda b, i: (b, 0, 0)),
                  full((1, D)), full((D, hw)), full((1, XA_DH)), full((1, XA_DH)), full((hw, D))],
        out_specs=pl.BlockSpec((1, tq, D), lambda b, i: (b, i, 0)),
        out_shape=jax.ShapeDtypeStruct((B, S, D), F32),
        compiler_params=_cparams(("parallel", "parallel")),
        name="cross_attention",
    )(x3, kv3, g.reshape(1, D), wq_bf16, qn.reshape(1, -1), kn.reshape(1, -1), wo_bf16)


BF16_ROWS = 16


def _top_values(s, k, want_rank=False):
    rows = []
    rank = jnp.full(s.shape, float(k), F32) if want_rank else None
    for r in range(k):
        m = jnp.max(s, axis=0, keepdims=True)
        rows.append(m)
        hit = s == m
        if want_rank:
            rank = jnp.where(hit, float(r), rank)
        s = jnp.where(hit, -jnp.inf, s)
    return rows, rank


def _routing_stages(s1, s2, sub, finish, nstage):
    K = PEER_TOPK
    st = {"a1": s1, "a2": s2, "v1": [], "v2": [], "rank2": jnp.full(s2.shape, float(K), F32)}

    def extract(src, dst, rounds, want_rank):
        def run(nil):
            arr = st[src]
            gone = nil - jnp.inf
            for r in rounds:
                m = jnp.max(arr, axis=0, keepdims=True)
                st[dst].append(m)
                hit = arr == m
                if want_rank:
                    st["rank2"] = jnp.where(hit, float(r), st["rank2"])
                arr = jnp.where(hit, gone, arr)
            st[src] = arr
            st["last"] = m
        return run

    def candidates(nil):
        v1, v2 = st["v1"], st["v2"]
        v1m = jnp.concatenate(v1, axis=0)
        v2m = jnp.concatenate(v2, axis=0)
        groups = [(v1[0] + nil) + v2m]
        for a in range(1, 8):
            groups.append(jnp.where(sub < K // (a + 1), v1[a] + v2m[0:8], -jnp.inf))
        groups.append(v1m[8:16] + v2[0])
        st["cand"] = jnp.concatenate(groups, axis=0)
        st["top"] = []
        st["last"] = v2[0]

    def threshold(nil):
        top = st["top"]
        st["tau"] = top[K - 1]
        z = nil
        for tv in top:
            z = z + jnp.exp(tv - top[0])
        st["z"] = z
        st["last"] = z

    def counts(nil):
        v1, v2, tau = st["v1"], st["v2"], st["tau"]
        count = jnp.zeros(s1.shape, F32)
        for bb in range(K // 2):
            count = count + jnp.where(s1 + v2[bb] >= tau, 1.0, 0.0)
        best = nil
        for bb in range(K // 2, K):
            best = best + jnp.where(v1[0] + v2[bb] >= tau, 1.0, 0.0)
        count = count + jnp.where(s1 == v1[0], best, 0.0)
        finish(count, jnp.exp(s1 - v1[0]) / st["z"], st["rank2"].astype(BF16),
               jnp.exp(s2 - v2[0]).astype(BF16))
        st["last"] = best

    quarter = [range(q * 4, q * 4 + 4) for q in range(4)]
    work = ([extract("a1", "v1", r, False) for r in quarter]
            + [extract("a2", "v2", r, True) for r in quarter]
            + [candidates] + [extract("cand", "top", r, False) for r in quarter]
            + [threshold, counts])
    bounds = [round(j * len(work) / nstage) for j in range(nstage + 1)]

    def stage(j):
        def run(nil):
            for piece in work[bounds[j]:bounds[j + 1]]:
                piece(nil)
            return st["last"]
        return run

    return [stage(j) for j in range(nstage)]


def _peer_kernel(x_ref, xnext_ref, g_ref, wq_ref, sk_ref, u_ref, vt_ref, o_ref,
                 xn_ref, xnt_ref, c_ref, a_ref, r2_ref, e2_ref, acc_ref, s_ref, *, ib, isub):
    t = pl.program_id(0)
    e = pl.program_id(1)
    nk = PEER_NKEYS
    tq = x_ref.shape[0]
    slot = t % 2
    nslot = 1 - slot
    sub = lax.broadcasted_iota(jnp.int32, (8, 1), 0)
    nsub = ib // isub

    def prepare(src_ref, s):
        xn32 = _rms(src_ref[...], g_ref[...])
        xn_ref[...] = xn32.astype(BF16)
        xnt_ref[s] = xn32.T.astype(BF16)

    def route_scores(h):
        q = jnp.dot(xn_ref[...], wq_ref[h], preferred_element_type=F32)
        return _dot_nt(sk_ref[2 * h], q[:, :nk]), _dot_nt(sk_ref[2 * h + 1], q[:, nk:])

    def table_writer(h, s):
        def finish(count, amp, rank2, e2):
            c_ref[s, h] = count
            a_ref[s, h] = amp
            r2_ref[s, h] = rank2
            e2_ref[s, h] = e2
        return finish

    @pl.when(jnp.logical_and(t == 0, e == 0))
    def _():
        prepare(x_ref, 0)

        def one(h, c):
            s1, s2 = route_scores(h)
            for run in _routing_stages(s1, s2, sub, table_writer(h, 0), 1):
                run(jnp.zeros((1, tq), F32))
            return c

        lax.fori_loop(0, PEER_HEADS, one, 0)

    @pl.when(e == 0)
    def _():
        acc_ref[...] = jnp.zeros_like(acc_ref)
        prepare(xnext_ref, nslot)
        s_ref[0], s_ref[1] = route_scores(0)

    ngrp = nk // BF16_ROWS
    xnt = xnt_ref[slot]
    first = pl.multiple_of(e * ib, ib)
    step_c = [c_ref[slot, h, pl.ds(first, ib), :] for h in range(PEER_HEADS)]
    step_a = [a_ref[slot, h, pl.ds(first, ib), :] for h in range(PEER_HEADS)]

    def expert_matmul(sc):
        lo = sc * isub * nk
        u_sub = pltpu.bitcast(u_ref[lo // 2:(lo + isub * nk) // 2, :], BF16)
        return jnp.dot(u_sub, xnt, preferred_element_type=F32)

    stages = _routing_stages(s_ref[0], s_ref[1], sub, table_writer(e, nslot), nsub)
    hid_next = expert_matmul(0)
    ahead_s1, ahead_s2 = route_scores(jnp.minimum(e + 1, PEER_HEADS - 1))

    total = None
    after = jnp.zeros((1, tq), F32)
    for sc in range(nsub):
        zero = jnp.broadcast_to(stages[sc](after) * 0.0, (BF16_ROWS, tq)).astype(BF16)
        lo = sc * isub * nk
        hid_sub = hid_next
        if sc + 1 < nsub:
            hid_next = expert_matmul(sc + 1)
        pieces = []
        for ii in range(isub):
            il = sc * isub + ii
            hid = hid_sub[ii * nk:(ii + 1) * nk, :]
            act = (0.5 * hid * (1.0 + lax.erf(hid * np.float32(np.sqrt(0.5))))).astype(BF16)
            w = [zero] * ngrp
            for h in range(PEER_HEADS):
                cnt = jnp.broadcast_to(step_c[h][il:il + 1, :], (BF16_ROWS, tq)).astype(BF16)
                amp = jnp.broadcast_to(step_a[h][il:il + 1, :], (BF16_ROWS, tq)).astype(BF16)
                for gi in range(ngrp):
                    rows = slice(gi * BF16_ROWS, (gi + 1) * BF16_ROWS)
                    w[gi] = w[gi] + jnp.where(r2_ref[slot, h, rows, :] < cnt,
                                              e2_ref[slot, h, rows, :] * amp, zero)
            pieces += [w[gi] * act[gi * BF16_ROWS:(gi + 1) * BF16_ROWS, :] for gi in range(ngrp)]
        after = pieces[-1][0:1, :].astype(F32) * 0.0
        vt_sub = pltpu.bitcast(vt_ref[:, lo:lo + isub * nk], BF16)
        part = jnp.dot(vt_sub, jnp.concatenate(pieces, axis=0), preferred_element_type=F32)
        total = part if total is None else total + part
    acc_ref[...] += total
    s_ref[0] = ahead_s1
    s_ref[1] = ahead_s2

    @pl.when(e == pl.num_programs(1) - 1)
    def _():
        o_ref[...] = x_ref[...] + acc_ref[...].T


def _pair_rows_kernel(x_ref, o_ref, *, transpose):
    x = x_ref[...]
    if transpose:
        x = x.T
    o_ref[...] = pltpu.bitcast(x.astype(BF16), jnp.uint32)


def _pair_rows(w, transpose=False, tr=512):
    R, C = w.shape
    if transpose:
        in_spec = pl.BlockSpec((tr, C), lambda i: (i, 0))
        out_spec = pl.BlockSpec((C // 2, tr), lambda i: (0, i))
        out_shape = jax.ShapeDtypeStruct((C // 2, R), jnp.uint32)
    else:
        in_spec = pl.BlockSpec((tr, C), lambda i: (i, 0))
        out_spec = pl.BlockSpec((tr // 2, C), lambda i: (i, 0))
        out_shape = jax.ShapeDtypeStruct((R // 2, C), jnp.uint32)
    return pl.pallas_call(
        functools.partial(_pair_rows_kernel, transpose=transpose),
        grid=(R // tr,), in_specs=[in_spec], out_specs=out_spec, out_shape=out_shape,
        compiler_params=_cparams(("parallel",)),
        name="pair_rows",
    )(w)


def peer_ffn(x2d, g, wq3_bf16, sk_bf16, u_pairs, vt_pairs, tq=256, isub=2):
    T, D = x2d.shape
    H = PEER_HEADS
    ne = H
    ib = PEER_NKEYS // ne
    ec = ib * PEER_NKEYS
    nt = T // tq
    full = lambda shape: pl.BlockSpec(shape, lambda i, e: (0,) * len(shape))
    words = pltpu.VMEM((2, H, PEER_NKEYS, tq), F32)
    halves = pltpu.VMEM((2, H, PEER_NKEYS, tq), BF16)
    return pl.pallas_call(
        functools.partial(_peer_kernel, ib=ib, isub=isub),
        grid=(nt, ne),
        in_specs=[pl.BlockSpec((tq, D), lambda i, e: (i, 0)),
                  pl.BlockSpec((tq, D), lambda i, e: (jnp.minimum(i + 1, nt - 1), 0)),
                  full((1, D)), full((H, D, 2 * PEER_NKEYS)), full((2 * H, PEER_NKEYS, PEER_NKEYS)),
                  pl.BlockSpec((ec // 2, D), lambda i, e: (e, 0)),
                  pl.BlockSpec((D // 2, ec), lambda i, e: (0, e))],
        out_specs=pl.BlockSpec((tq, D), lambda i, e: (i, 0)),
        out_shape=jax.ShapeDtypeStruct((T, D), F32),
        scratch_shapes=[pltpu.VMEM((tq, D), BF16), pltpu.VMEM((2, D, tq), BF16),
                        words, words, halves, halves, pltpu.VMEM((D, tq), F32),
                        pltpu.VMEM((2, PEER_NKEYS, tq), F32)],
        compiler_params=_cparams(("arbitrary", "arbitrary")),
        name="peer_ffn",
    )(x2d, x2d, g.reshape(1, D), wq3_bf16, sk_bf16, u_pairs, vt_pairs)


def _block_diag(w):
    G, n, _ = w.shape
    eye = jnp.eye(G, dtype=w.dtype)
    return (eye[:, None, :, None] * w[:, :, None, :]).reshape(G * n, G * n)


def even_layer(x2d, B, S, norm, w_in, gate_up, gate_b, out_norm, w_out):
    D = x2d.shape[1]
    gg0 = EV_GG * LANES
    wp = jnp.concatenate([w_in[:, :1536], w_in[:, 1536 + GLA_RANK:], w_in[:, 1536:1536 + GLA_RANK],
                          jnp.zeros((D, EV_COLS - gg0 - GLA_RANK), w_in.dtype)], axis=1).astype(BF16)
    proj = norm_proj(x2d, norm, wp).reshape(B, S, EV_COLS)
    gup = jnp.concatenate([gate_up, jnp.zeros((LANES - GLA_RANK, gate_up.shape[1]), gate_up.dtype)], axis=0)
    a_out = gla_mixer(proj, gup.astype(BF16), gate_b, out_norm)
    b_out = sb_attention(proj)
    return out_proj(x2d, a_out.reshape(B * S, -1), b_out.reshape(B * S, -1), w_out.astype(BF16))


def odd_layer(x2d, B, S, positions, norm, w_in, q_norm, k_norm, conv_w, conv_b, wa, ba, wx, bx, lam, w_out):
    proj = norm_proj(x2d, norm, w_in.astype(BF16)).reshape(B, S, -1)
    half = HEAD_DIM // 2
    inv_freq = ROPE_THETA ** (-jnp.arange(0, HEAD_DIM, 2, dtype=F32) / HEAD_DIM)
    inv_tile = jnp.tile(inv_freq, LANES // half).reshape(1, LANES)
    seg = np.arange(OD_W) // HEAD_DIM
    blockdiag = jnp.asarray(seg[:, None] == seg[None, :], dtype=BF16)
    q_t, k_rot, v_t, kmean = moba_prep(proj, positions.reshape(B, S, 1), inv_tile,
                                       jnp.tile(q_norm, OD_W // HEAD_DIM).reshape(1, OD_W),
                                       jnp.tile(k_norm, OD_W // HEAD_DIM).reshape(1, OD_W), blockdiag)
    c_out = moba_attention(q_t, k_rot, v_t, kmean.reshape(B, S // MOBA_BLOCK, OD_W))
    d_out = rg_lru_mixer(proj, conv_w, conv_b, _block_diag(wa).astype(BF16), ba,
                         _block_diag(wx).astype(BF16), bx, lam)
    return out_proj(x2d, c_out.reshape(B * S, -1), d_out.reshape(B * S, -1), w_out.astype(BF16))


def kernel(x, mem, positions, ev_norm, ev_w_in, ev_gla_gate_up, ev_gla_gate_b, ev_gla_out_norm, ev_w_out, od_norm, od_w_in, od_q_norm, od_k_norm, od_conv_w, od_conv_b, od_gate_a_w, od_gate_a_b, od_gate_x_w, od_gate_x_b, od_lambda, od_w_out, xa_norm, xa_mem_norm, xa_wq, xa_wkv, xa_q_norm, xa_k_norm, xa_wo, ffn_norm, peer_wq, peer_subkeys, peer_u, peer_v):
    B, S, D = x.shape
    M = mem.shape[1]
    depth = xa_norm.shape[0]
    x2d = x.reshape(B * S, D)
    mem2d = mem.reshape(B * M, D)
    for l in range(depth):
        if l % 2 == 0:
            e = l // 2
            x2d = even_layer(x2d, B, S, ev_norm[e], ev_w_in[e], ev_gla_gate_up[e], ev_gla_gate_b[e],
                             ev_gla_out_norm[e], ev_w_out[e])
        else:
            o = l // 2
            x2d = odd_layer(x2d, B, S, positions, od_norm[o], od_w_in[o], od_q_norm[o], od_k_norm[o],
                            od_conv_w[o], od_conv_b[o], od_gate_a_w[o], od_gate_a_b[o],
                            od_gate_x_w[o], od_gate_x_b[o], od_lambda[o], od_w_out[o])
        kv = norm_proj(mem2d, xa_mem_norm[l], xa_wkv[l].astype(BF16))
        x2d = cross_attention(x2d.reshape(B, S, D), kv.reshape(B, M, -1), xa_norm[l],
                              xa_wq[l].astype(BF16), xa_q_norm[l], xa_k_norm[l],
                              xa_wo[l].astype(BF16)).reshape(B * S, D)
        H = PEER_HEADS
        wq3 = peer_wq[l].reshape(D, H, 2 * PEER_NKEYS).transpose(1, 0, 2).astype(BF16)
        sk = peer_subkeys[l].reshape(2 * H, PEER_NKEYS, -1).astype(BF16)
        x2d = peer_ffn(x2d, ffn_norm[l], wq3, sk, _pair_rows(peer_u[l]),
                       _pair_rows(peer_v[l], transpose=True))
    return x2d.reshape(B, S, D)
```

```python
import functools

import jax
import jax.numpy as jnp
import numpy as np
from jax import lax
from jax.experimental import pallas as pl
from jax.experimental.pallas import tpu as pltpu

F32 = jnp.float32
BF16 = jnp.bfloat16

EPS = 1e-6
ROPE_THETA = 10000.0
LANES = 128
HEAD_DIM = 64
GLA_HEADS = 4
GLA_DV = 128
GLA_CHUNK = 64
GLA_TAU = 16.0
GLA_RANK = 16
MOBA_BLOCK = 256
MOBA_TOPK = 3
RG_C = 8.0
RG_CONV = 4
XA_HEADS = 4
XA_DH = 128
PEER_HEADS = 8
PEER_NKEYS = 128
PEER_TOPK = 16
NEG = -1e30
F32_LOG_TINY = -104.0
VMEM_LIMIT = 56 * 1024 * 1024


def _cparams(sem):
    return pltpu.CompilerParams(dimension_semantics=sem, vmem_limit_bytes=VMEM_LIMIT)


def _dot(a, b):
    return jnp.dot(a.astype(BF16), b.astype(BF16), preferred_element_type=F32)


def _dot_nt(a, b):
    return lax.dot_general(a.astype(BF16), b.astype(BF16), (((1,), (1,)), ((), ())),
                           preferred_element_type=F32)


def _dot_tn(a, b):
    return lax.dot_general(a.astype(BF16), b.astype(BF16), (((0,), (0,)), ((), ())),
                           preferred_element_type=F32)


def _split2(a):
    hi = a.astype(BF16)
    lo = (a - hi.astype(F32)).astype(BF16)
    return hi, lo


def _dot_exact_rhs(a, m_bf16):
    hi, lo = _split2(a)
    return (jnp.dot(hi, m_bf16, preferred_element_type=F32)
            + jnp.dot(lo, m_bf16, preferred_element_type=F32))


def _dot_exact_lhs(m_bf16, a):
    hi, lo = _split2(a)
    return (jnp.dot(m_bf16, hi, preferred_element_type=F32)
            + jnp.dot(m_bf16, lo, preferred_element_type=F32))


def _rms(x, g):
    return x * lax.rsqrt(jnp.mean(x * x, axis=-1, keepdims=True) + EPS) * g


def _softplus(z):
    return jnp.maximum(z, 0.0) + jnp.log1p(jnp.exp(-jnp.abs(z)))


def _head_mask(hh):
    lane = lax.broadcasted_iota(jnp.int32, (1, LANES), 1)
    return ((lane // HEAD_DIM) == hh).astype(F32)


def _norm_proj_kernel(x_ref, g_ref, w_ref, o_ref):
    xn = _rms(x_ref[...], g_ref[...])
    o_ref[...] = jnp.dot(xn.astype(BF16), w_ref[...], preferred_element_type=F32)


def norm_proj(x2d, g, w_bf16, tm=256):
    T, D = x2d.shape
    N = w_bf16.shape[1]
    return pl.pallas_call(
        _norm_proj_kernel,
        grid=(T // tm,),
        in_specs=[pl.BlockSpec((tm, D), lambda i: (i, 0)),
                  pl.BlockSpec((1, D), lambda i: (0, 0)),
                  pl.BlockSpec((D, N), lambda i: (0, 0))],
        out_specs=pl.BlockSpec((tm, N), lambda i: (i, 0)),
        out_shape=jax.ShapeDtypeStruct((T, N), F32),
        compiler_params=_cparams(("parallel",)),
        name="norm_proj",
    )(x2d, g.reshape(1, D), w_bf16)


def _out_proj_kernel(x_ref, a_ref, b_ref, w_ref, o_ref):
    ka = a_ref.shape[1]
    o_ref[...] = (x_ref[...] + _dot(a_ref[...], w_ref[0:ka, :])
                  + _dot(b_ref[...], w_ref[ka:, :]))


def out_proj(x2d, a2d, b2d, w_bf16, tm=512):
    T, D = x2d.shape
    ka, kb = a2d.shape[1], b2d.shape[1]
    return pl.pallas_call(
        _out_proj_kernel,
        grid=(T // tm,),
        in_specs=[pl.BlockSpec((tm, D), lambda i: (i, 0)),
                  pl.BlockSpec((tm, ka), lambda i: (i, 0)),
                  pl.BlockSpec((tm, kb), lambda i: (i, 0)),
                  pl.BlockSpec((ka + kb, D), lambda i: (0, 0))],
        out_specs=pl.BlockSpec((tm, D), lambda i: (i, 0)),
        out_shape=jax.ShapeDtypeStruct((T, D), F32),
        compiler_params=_cparams(("parallel",)),
        name="out_proj",
    )(x2d, a2d, b2d, w_bf16)


EV_Q, EV_K, EV_V, EV_R, EV_SQ, EV_SK, EV_SV, EV_GG = 0, 2, 4, 8, 12, 16, 20, 24
EV_COLS = 25 * LANES


def _gla_kernel(q_ref, k_ref, v_ref, r_ref, gg_ref, gup_ref, gb_ref, onorm_ref, o_ref, state_ref,
                *, ts):
    C = GLA_CHUNK

    @pl.when(pl.program_id(1) == 0)
    def _():
        state_ref[...] = jnp.zeros_like(state_ref)

    row = lax.broadcasted_iota(jnp.int32, (C, C), 0)
    col = lax.broadcasted_iota(jnp.int32, (C, C), 1)
    causal = row >= col
    tri = causal.astype(BF16)
    masks = [_head_mask(0), _head_mask(1)]
    scale = HEAD_DIM ** -0.5
    onorm = onorm_ref[...]

    for ci in range(ts // C):
        sl = slice(ci * C, (ci + 1) * C)
        pre = _dot(gg_ref[0, sl, :], gup_ref[...]) + gb_ref[...]
        g = (jnp.minimum(pre, 0.0) - jnp.log1p(jnp.exp(-jnp.abs(pre)))) * (1.0 / GLA_TAU)
        b = _dot_exact_lhs(tri, g)
        bmid = b[C // 2 - 1:C // 2, :]
        blast = b[C - 1:C, :]
        q = q_ref[0, sl, :] * scale
        k = k_ref[0, sl, :]
        qd = q * jnp.exp(b - bmid)
        kd = k * jnp.exp(bmid - b)
        qe = q * jnp.exp(b)
        kdec = k * jnp.exp(blast - b)
        eb_last = jnp.exp(blast)
        for p in range(GLA_HEADS // 2):
            lanes = slice(p * LANES, (p + 1) * LANES)
            st = state_ref[p]
            new_st = st * eb_last[:, lanes]
            for hh in range(2):
                h = 2 * p + hh
                m = masks[hh]
                hl = slice(h * GLA_DV, (h + 1) * GLA_DV)
                a = _dot_nt(qd[:, lanes] * m, kd[:, lanes])
                a = jnp.where(causal, a, 0.0)
                v_h = v_ref[0, sl, hl]
                o = _dot(a, v_h) + _dot_nt(qe[:, lanes] * m, st)
                o = _rms(o, onorm)
                r_h = r_ref[0, sl, hl]
                o_ref[0, sl, hl] = o * (r_h * jax.nn.sigmoid(r_h))
                new_st = new_st + _dot_tn(v_h, kdec[:, lanes] * m)
            state_ref[p] = new_st


def gla_mixer(proj3, gate_up_pad, gate_b, out_norm, ts=256):
    B, S, _ = proj3.shape
    nh = GLA_HEADS
    w = nh * GLA_DV

    def col(blk_w, tile):
        idx = tile * LANES // blk_w
        return pl.BlockSpec((1, ts, blk_w), lambda b, c: (b, c, idx))

    return pl.pallas_call(
        functools.partial(_gla_kernel, ts=ts),
        grid=(B, S // ts),
        in_specs=[col(nh * HEAD_DIM, EV_Q), col(nh * HEAD_DIM, EV_K), col(w, EV_V), col(w, EV_R),
                  col(LANES, EV_GG),
                  pl.BlockSpec((LANES, nh * HEAD_DIM), lambda b, c: (0, 0)),
                  pl.BlockSpec((1, nh * HEAD_DIM), lambda b, c: (0, 0)),
                  pl.BlockSpec((1, GLA_DV), lambda b, c: (0, 0))],
        out_specs=pl.BlockSpec((1, ts, w), lambda b, c: (b, c, 0)),
        out_shape=jax.ShapeDtypeStruct((B, S, w), F32),
        scratch_shapes=[pltpu.VMEM((nh // 2, GLA_DV, LANES), F32)],
        compiler_params=_cparams(("parallel", "arbitrary")),
        name="gla",
    )(proj3, proj3, proj3, proj3, proj3, gate_up_pad, gate_b.reshape(1, -1), out_norm.reshape(1, -1))


def _sb_kernel(q_ref, k_ref, v_ref, o_ref, acc_ref, carry_ref, *, tq):
    i = pl.program_id(2)
    q = q_ref[0] * (HEAD_DIM ** -0.5)
    masks = [_head_mask(0), _head_mask(1)]
    qh = [(q * m).astype(BF16) for m in masks]
    row = lax.broadcasted_iota(jnp.int32, (tq, tq), 0)
    col = lax.broadcasted_iota(jnp.int32, (tq, tq), 1)
    upper = (row > col).astype(BF16)
    past = col < row
    acc_ref[...] = jnp.zeros_like(acc_ref)
    carry_ref[...] = jnp.zeros_like(carry_ref)

    def tile(j, diag):
        off = pl.multiple_of(j * tq, tq)
        k = k_ref[0, pl.ds(off, tq), :].astype(BF16)
        v = v_ref[0, pl.ds(off, tq), :].astype(BF16)
        for hh in range(2):
            z = lax.dot_general(qh[hh], k, (((1,), (1,)), ((), ())), preferred_element_type=F32)
            sp = _softplus(z)
            log_rem = jnp.where(past, -sp, 0.0) if diag else -sp
            carry = carry_ref[hh]
            after = _dot_exact_rhs(log_rem, upper) + carry
            w = jnp.exp((z - sp) + after)
            if diag:
                w = jnp.where(past, w, 0.0)
            acc_ref[hh] += jnp.dot(w.astype(BF16), v, preferred_element_type=F32)
            carry_ref[hh] = carry + jnp.sum(log_rem, axis=1, keepdims=True)

    def first_two_tiles():
        off = pl.multiple_of((i - 1) * tq, tq)
        k = k_ref[0, pl.ds(off, 2 * tq), :].astype(BF16)
        v = v_ref[0, pl.ds(off, 2 * tq), :].astype(BF16)
        for hh in range(2):
            z = lax.dot_general(qh[hh], k, (((1,), (1,)), ((), ())), preferred_element_type=F32)
            sp = _softplus(z)
            z_l, z_r, sp_l, sp_r = z[:, :tq], z[:, tq:], sp[:, :tq], sp[:, tq:]
            rem_r = jnp.where(past, -sp_r, 0.0)
            after_r = _dot_exact_rhs(rem_r, upper)
            tot_r = jnp.sum(rem_r, axis=1, keepdims=True)
            after_l = _dot_exact_rhs(-sp_l, upper) + tot_r
            w_r = jnp.where(past, jnp.exp((z_r - sp_r) + after_r), 0.0)
            w_l = jnp.exp((z_l - sp_l) + after_l)
            w = jnp.concatenate([w_l, w_r], axis=1).astype(BF16)
            acc_ref[hh] = jnp.dot(w, v, preferred_element_type=F32)
            carry_ref[hh] = tot_r - jnp.sum(sp_l, axis=1, keepdims=True)

    @pl.when(i == 0)
    def _():
        tile(i, True)

    @pl.when(i > 0)
    def _():
        first_two_tiles()

    def live(c):
        j, worst = c
        return jnp.logical_and(j >= 0, worst > F32_LOG_TINY)

    def body(c):
        j, _ = c
        tile(j, False)
        return j - 1, jnp.max(carry_ref[...])

    lax.while_loop(live, body, (i - 2, jnp.max(carry_ref[...])))
    o_ref[0] = acc_ref[0] * masks[0] + acc_ref[1] * masks[1]


def sb_attention(proj3, tq=256):
    B, S, _ = proj3.shape
    npairs = 4
    return pl.pallas_call(
        functools.partial(_sb_kernel, tq=tq),
        grid=(B, npairs, S // tq),
        in_specs=[pl.BlockSpec((1, tq, LANES), lambda b, p, i: (b, i, EV_SQ + p)),
                  pl.BlockSpec((1, S, LANES), lambda b, p, i: (b, 0, EV_SK + p)),
                  pl.BlockSpec((1, S, LANES), lambda b, p, i: (b, 0, EV_SV + p))],
        out_specs=pl.BlockSpec((1, tq, LANES), lambda b, p, i: (b, i, p)),
        out_shape=jax.ShapeDtypeStruct((B, S, npairs * LANES), F32),
        scratch_shapes=[pltpu.VMEM((2, tq, LANES), F32), pltpu.VMEM((2, tq, 1), F32)],
        compiler_params=_cparams(("parallel", "parallel", "arbitrary")),
        name="sb_attention",
    )(proj3, proj3, proj3)


OD_Q, OD_K, OD_V, OD_RX, OD_RG = 0, 4, 8, 12, 16
OD_W = 4 * LANES


def _moba_prep_kernel(q_ref, k_ref, v_ref, pos_ref, invf_ref, qn_ref, kn_ref, bd_ref,
                      qt_ref, ko_ref, vt_ref, km_ref):
    ang = pos_ref[0].astype(F32) * invf_ref[...]
    cos1, sin1 = jnp.cos(ang), jnp.sin(ang)
    lane = lax.broadcasted_iota(jnp.int32, (1, LANES), 1)
    first_half = (lane % HEAD_DIM) < (HEAD_DIM // 2)
    bd = bd_ref[...]

    def norm_rope(x, gain):
        ms = _dot_exact_rhs(x * x, bd) * (1.0 / HEAD_DIM)
        xn = x * lax.rsqrt(ms + EPS) * gain
        outs = []
        for t in range(OD_W // LANES):
            xb = xn[:, t * LANES:(t + 1) * LANES]
            up = pltpu.roll(xb, LANES - HEAD_DIM // 2, axis=1)
            dn = pltpu.roll(xb, HEAD_DIM // 2, axis=1)
            outs.append(xb * cos1 + jnp.where(first_half, -up, dn) * sin1)
        return jnp.concatenate(outs, axis=1)

    qt_ref[0] = norm_rope(q_ref[0], qn_ref[...]).T.astype(BF16)
    kr = norm_rope(k_ref[0], kn_ref[...])
    ko_ref[0] = kr.astype(BF16)
    km_ref[0, 0] = jnp.mean(kr, axis=0, keepdims=True)
    vt_ref[0] = v_ref[0].T.astype(BF16)


def moba_prep(proj3, pos3, inv_freq_tile, qn_tile, kn_tile, blockdiag):
    B, S, _ = proj3.shape
    tb = MOBA_BLOCK
    nb = S // tb
    full = lambda shape: pl.BlockSpec(shape, lambda b, i: (0,) * len(shape))
    return pl.pallas_call(
        _moba_prep_kernel,
        grid=(B, nb),
        in_specs=[pl.BlockSpec((1, tb, OD_W), lambda b, i: (b, i, OD_Q // 4)),
                  pl.BlockSpec((1, tb, OD_W), lambda b, i: (b, i, OD_K // 4)),
                  pl.BlockSpec((1, tb, OD_W), lambda b, i: (b, i, OD_V // 4)),
                  pl.BlockSpec((1, tb, 1), lambda b, i: (b, i, 0)),
                  full((1, LANES)), full((1, OD_W)), full((1, OD_W)), full((OD_W, OD_W))],
        out_specs=[pl.BlockSpec((1, OD_W, tb), lambda b, i: (b, 0, i)),
                   pl.BlockSpec((1, tb, OD_W), lambda b, i: (b, i, 0)),
                   pl.BlockSpec((1, OD_W, tb), lambda b, i: (b, 0, i)),
                   pl.BlockSpec((1, 1, 1, OD_W), lambda b, i: (b, i, 0, 0))],
        out_shape=[jax.ShapeDtypeStruct((B, OD_W, S), BF16),
                   jax.ShapeDtypeStruct((B, S, OD_W), BF16),
                   jax.ShapeDtypeStruct((B, OD_W, S), BF16),
                   jax.ShapeDtypeStruct((B, nb, 1, OD_W), F32)],
        compiler_params=_cparams(("parallel", "parallel")),
        name="moba_prep",
    )(proj3, proj3, proj3, pos3, inv_freq_tile, qn_tile, kn_tile, blockdiag)


def _moba_kernel(qt_ref, k_ref, vt_ref, km_ref, o_ref, acc_ref, m_ref, sel_ref):
    tb = MOBA_BLOCK
    nbp = km_ref.shape[1]
    i = pl.program_id(2)
    qt = qt_ref[0]
    km = km_ref[0].astype(BF16)
    chan = lax.broadcasted_iota(jnp.int32, (LANES, 1), 0)
    in_head = [(chan // HEAD_DIM) == hh for hh in range(2)]
    blk = lax.broadcasted_iota(jnp.int32, (nbp, tb), 0).astype(F32)
    own = i.astype(F32)
    key = lax.broadcasted_iota(jnp.int32, (tb, tb), 0)
    qry = lax.broadcasted_iota(jnp.int32, (tb, tb), 1)
    visible = key <= qry
    off_own = pl.multiple_of(i * tb, tb)
    k_own = k_ref[0, pl.ds(off_own, tb), :]
    vt_own = vt_ref[0, :, pl.ds(off_own, tb)]
    zero = jnp.zeros_like(qt)
    one = jnp.ones_like(qt)
    qs = []
    for hh in range(2):
        qm = jnp.where(in_head[hh], qt, zero)
        gate = jnp.dot(km, qm, preferred_element_type=F32)
        g = jnp.where(blk < own, gate, -jnp.inf)
        sel = jnp.zeros((nbp, tb), F32)
        for r in range(MOBA_TOPK):
            mx = jnp.max(g, axis=0, keepdims=True)
            idx = jnp.min(jnp.where(g == mx, blk, float(nbp)), axis=0, keepdims=True)
            hit = blk == idx
            keep = jnp.where(i > r, 1.0, 0.0)
            sel = sel + jnp.where(hit, keep, 0.0)
            g = jnp.where(hit, -jnp.inf, g)
        sel_ref[hh] = sel
        qb = qm * (HEAD_DIM ** -0.5)
        qs.append(qb)
        s = jnp.dot(k_own, qb, preferred_element_type=F32)
        s = jnp.where(visible, s, NEG)
        m0 = jnp.max(s, axis=0, keepdims=True)
        p = jnp.exp(s - m0)
        m_ref[hh] = m0
        acc_ref[hh] = jnp.dot(jnp.where(in_head[hh], vt_own, one), p.astype(BF16),
                              preferred_element_type=F32)

    def past_blocks(n, nblk):
        off = pl.multiple_of(n * tb, tb)
        k = k_ref[0, pl.ds(off, nblk * tb), :]
        vt = vt_ref[0, :, pl.ds(off, nblk * tb)]
        scores = [jnp.dot(k, qs[hh], preferred_element_type=F32) for hh in range(2)]
        probs, alphas = [], []
        for hh in range(2):
            parts = []
            for j in range(nblk):
                chosen = sel_ref[hh, pl.ds(n + j, 1), :]
                parts.append(jnp.where(chosen > 0.0, scores[hh][j * tb:(j + 1) * tb, :], NEG))
            m_old = m_ref[hh]
            m_new = m_old
            for part in parts:
                m_new = jnp.maximum(m_new, jnp.max(part, axis=0, keepdims=True))
            alphas.append(jnp.exp(m_old - m_new))
            probs.append(jnp.concatenate([jnp.exp(part - m_new).astype(BF16) for part in parts], axis=0))
            m_ref[hh] = m_new
        ones_v = jnp.ones_like(vt)
        pv = [jnp.dot(jnp.where(in_head[hh], vt, ones_v), probs[hh], preferred_element_type=F32)
              for hh in range(2)]
        for hh in range(2):
            acc_ref[hh] = acc_ref[hh] * alphas[hh] + pv[hh]

    def four_blocks(t, c):
        past_blocks(4 * t, 4)
        return c

    lax.fori_loop(0, i // 4, four_blocks, 0)
    rem = i % 4

    @pl.when(rem >= 2)
    def _():
        past_blocks(i - rem, 2)

    @pl.when(rem % 2 == 1)
    def _():
        past_blocks(i - 1, 1)

    a0, a1 = acc_ref[0], acc_ref[1]
    den0 = a0[HEAD_DIM:HEAD_DIM + 1, :]
    den1 = a1[0:1, :]
    o_ref[0] = jnp.where(in_head[0], a0 / den0, a1 / den1).T


def moba_attention(q_t, k_rot, v_t, kmean):
    B, S, _ = k_rot.shape
    tb = MOBA_BLOCK
    nb = S // tb
    npairs = OD_W // LANES
    nbp = -(-nb // BF16_ROWS) * BF16_ROWS
    kmean = jnp.pad(kmean, ((0, 0), (0, nbp - nb), (0, 0)))
    return pl.pallas_call(
        _moba_kernel,
        grid=(B, npairs, nb),
        in_specs=[pl.BlockSpec((1, LANES, tb), lambda b, p, i: (b, p, i)),
                  pl.BlockSpec((1, S, LANES), lambda b, p, i: (b, 0, p)),
                  pl.BlockSpec((1, LANES, S), lambda b, p, i: (b, p, 0)),
                  pl.BlockSpec((1, nbp, LANES), lambda b, p, i: (b, 0, p))],
        out_specs=pl.BlockSpec((1, tb, LANES), lambda b, p, i: (b, i, p)),
        out_shape=jax.ShapeDtypeStruct((B, S, OD_W), F32),
        scratch_shapes=[pltpu.VMEM((2, LANES, tb), F32), pltpu.VMEM((2, 1, tb), F32),
                        pltpu.VMEM((2, nbp, tb), F32)],
        compiler_params=_cparams(("parallel", "parallel", "arbitrary")),
        name="moba_attention",
    )(q_t, k_rot, v_t, kmean)


def _rglru_kernel(x_ref, gate_ref, cw_ref, cb_ref, wa_ref, ba_ref, wx_ref, bx_ref, lam_ref, o_ref,
                  buf_ref, xprev_ref, hprev_ref, *, ts):
    @pl.when(pl.program_id(1) == 0)
    def _():
        xprev_ref[...] = jnp.zeros_like(xprev_ref)
        hprev_ref[...] = jnp.zeros_like(hprev_ref)

    x = x_ref[0]
    buf_ref[0:8, :] = xprev_ref[...]
    buf_ref[8:8 + ts, :] = x
    xprev_ref[...] = x[ts - 8:ts, :]
    xc = cb_ref[...]
    for kk in range(RG_CONV):
        start = 8 - (RG_CONV - 1) + kk
        xc = xc + cw_ref[kk:kk + 1, :] * buf_ref[start:start + ts, :]
    r = jax.nn.sigmoid(_dot(xc, wa_ref[...]) + ba_ref[...])
    ig = jax.nn.sigmoid(_dot(xc, wx_ref[...]) + bx_ref[...])
    log_a = (-RG_C) * r * _softplus(-lam_ref[...])
    a = jnp.exp(log_a)
    u = jnp.sqrt(-jnp.tanh(log_a) * (a * a + 1.0)) * (ig * xc)
    t_idx = lax.broadcasted_iota(jnp.int32, (ts, 1), 0)
    d = 1
    while d < ts:
        valid = t_idx >= d
        a_sh = pltpu.roll(a, d, axis=0)
        u_sh = pltpu.roll(u, d, axis=0)
        u = jnp.where(valid, a * u_sh, 0.0) + u
        a = jnp.where(valid, a * a_sh, a)
        d *= 2
    h = a * hprev_ref[...] + u
    hprev_ref[...] = h[ts - 1:ts, :]
    o_ref[0] = h * jax.nn.gelu(gate_ref[0], approximate=True)


def rg_lru_mixer(proj3, conv_w, conv_b, wa_bd, ba, wx_bd, bx, lam, ts=256):
    B, S, _ = proj3.shape
    W = conv_w.shape[1]
    full = lambda shape: pl.BlockSpec(shape, lambda b, t: (0,) * len(shape))
    row = lambda v: v.reshape(1, W)
    return pl.pallas_call(
        functools.partial(_rglru_kernel, ts=ts),
        grid=(B, S // ts),
        in_specs=[pl.BlockSpec((1, ts, W), lambda b, t: (b, t, OD_RX // 4)),
                  pl.BlockSpec((1, ts, W), lambda b, t: (b, t, OD_RG // 4)),
                  full((RG_CONV, W)), full((1, W)), full((W, W)), full((1, W)), full((W, W)),
                  full((1, W)), full((1, W))],
        out_specs=pl.BlockSpec((1, ts, W), lambda b, t: (b, t, 0)),
        out_shape=jax.ShapeDtypeStruct((B, S, W), F32),
        scratch_shapes=[pltpu.VMEM((ts + 8, W), F32), pltpu.VMEM((8, W), F32), pltpu.VMEM((1, W), F32)],
        compiler_params=_cparams(("parallel", "arbitrary")),
        name="rg_lru",
    )(proj3, proj3, conv_w, row(conv_b), wa_bd, row(ba), wx_bd, row(bx), row(lam))


def _xattn_kernel(x_ref, kv_ref, g_ref, wq_ref, qn_ref, kn_ref, wo_ref, o_ref):
    x = x_ref[0]
    q = _dot(_rms(x, g_ref[...]), wq_ref[...])
    kv = kv_ref[0]
    hw = XA_HEADS * XA_DH
    outs = []
    for h in range(XA_HEADS):
        hl = slice(h * XA_DH, (h + 1) * XA_DH)
        qh = _rms(q[:, hl], qn_ref[...])
        kh = _rms(kv[:, hl], kn_ref[...])
        vh = kv[:, hw + h * XA_DH: hw + (h + 1) * XA_DH]
        s = _dot_nt(qh, kh) * (XA_DH ** -0.5)
        s = s - jnp.max(s, axis=1, keepdims=True)
        p = jnp.exp(s)
        p = p / jnp.sum(p, axis=1, keepdims=True)
        outs.append(_dot(p, vh))
    o_ref[0] = x + _dot(jnp.concatenate(outs, axis=1), wo_ref[...])


def cross_attention(x3, kv3, g, wq_bf16, qn, kn, wo_bf16, tq=256):
    B, S, D = x3.shape
    M = kv3.shape[1]
    hw = XA_HEADS * XA_DH
    full = lambda shape: pl.BlockSpec(shape, lambda b, i: (0,) * len(shape))
    return pl.pallas_call(
        _xattn_kernel,
        grid=(B, S // tq),
        in_specs=[pl.BlockSpec((1, tq, D), lambda b, i: (b, i, 0)),
                  pl.BlockSpec((1, M, 2 * hw), lambda b, i: (b, 0, 0)),
                  full((1, D)), full((D, hw)), full((1, XA_DH)), full((1, XA_DH)), full((hw, D))],
        out_specs=pl.BlockSpec((1, tq, D), lambda b, i: (b, i, 0)),
        out_shape=jax.ShapeDtypeStruct((B, S, D), F32),
        compiler_params=_cparams(("parallel", "parallel")),
        name="cross_attention",
    )(x3, kv3, g.reshape(1, D), wq_bf16, qn.reshape(1, -1), kn.reshape(1, -1), wo_bf16)


BF16_ROWS = 16


def _top_values(s, k, want_rank=False):
    rows = []
    rank = jnp.full(s.shape, float(k), F32) if want_rank else None
    for r in range(k):
        m = jnp.max(s, axis=0, keepdims=True)
        rows.append(m)
        hit = s == m
        if want_rank:
            rank = jnp.where(hit, float(r), rank)
        s = jnp.where(hit, -jnp.inf, s)
    return rows, rank


def _routing_stages(s1, s2, sub, finish, nstage):
    K = PEER_TOPK
    st = {"a1": s1, "a2": s2, "v1": [], "v2": [], "rank2": jnp.full(s2.shape, float(K), F32)}

    def extract(src, dst, rounds, want_rank):
        def run(nil):
            arr = st[src]
            gone = nil - jnp.inf
            for r in rounds:
                m = jnp.max(arr, axis=0, keepdims=True)
                st[dst].append(m)
                hit = arr == m
                if want_rank:
                    st["rank2"] = jnp.where(hit, float(r), st["rank2"])
                arr = jnp.where(hit, gone, arr)
            st[src] = arr
            st["last"] = m
        return run

    def candidates(nil):
        v1, v2 = st["v1"], st["v2"]
        v1m = jnp.concatenate(v1, axis=0)
        v2m = jnp.concatenate(v2, axis=0)
        groups = [(v1[0] + nil) + v2m]
        for a in range(1, 8):
            groups.append(jnp.where(sub < K // (a + 1), v1[a] + v2m[0:8], -jnp.inf))
        groups.append(v1m[8:16] + v2[0])
        st["cand"] = jnp.concatenate(groups, axis=0)
        st["top"] = []
        st["last"] = v2[0]

    def threshold(nil):
        top = st["top"]
        st["tau"] = top[K - 1]
        z = nil
        for tv in top:
            z = z + jnp.exp(tv - top[0])
        st["z"] = z
        st["last"] = z

    def counts(nil):
        v1, v2, tau = st["v1"], st["v2"], st["tau"]
        count = jnp.zeros(s1.shape, F32)
        for bb in range(K // 2):
            count = count + jnp.where(s1 + v2[bb] >= tau, 1.0, 0.0)
        best = nil
        for bb in range(K // 2, K):
            best = best + jnp.where(v1[0] + v2[bb] >= tau, 1.0, 0.0)
        count = count + jnp.where(s1 == v1[0], best, 0.0)
        finish(count, jnp.exp(s1 - v1[0]) / st["z"], st["rank2"].astype(BF16),
               jnp.exp(s2 - v2[0]).astype(BF16))
        st["last"] = best

    quarter = [range(q * 4, q * 4 + 4) for q in range(4)]
    work = ([extract("a1", "v1", r, False) for r in quarter]
            + [extract("a2", "v2", r, True) for r in quarter]
            + [candidates] + [extract("cand", "top", r, False) for r in quarter]
            + [threshold, counts])
    bounds = [round(j * len(work) / nstage) for j in range(nstage + 1)]

    def stage(j):
        def run(nil):
            for piece in work[bounds[j]:bounds[j + 1]]:
                piece(nil)
            return st["last"]
        return run

    return [stage(j) for j in range(nstage)]


def _peer_kernel(x_ref, xnext_ref, g_ref, wq_ref, sk_ref, u_ref, vt_ref, o_ref,
                 xn_ref, xnt_ref, c_ref, a_ref, r2_ref, e2_ref, acc_ref, s_ref, *, ib, isub):
    t = pl.program_id(0)
    e = pl.program_id(1)
    nk = PEER_NKEYS
    tq = x_ref.shape[0]
    slot = t % 2
    nslot = 1 - slot
    sub = lax.broadcasted_iota(jnp.int32, (8, 1), 0)
    nsub = ib // isub

    def prepare(src_ref, s):
        xn32 = _rms(src_ref[...], g_ref[...])
        xn_ref[...] = xn32.astype(BF16)
        xnt_ref[s] = xn32.T.astype(BF16)

    def route_scores(h):
        q = jnp.dot(xn_ref[...], wq_ref[h], preferred_element_type=F32)
        return _dot_nt(sk_ref[2 * h], q[:, :nk]), _dot_nt(sk_ref[2 * h + 1], q[:, nk:])

    def table_writer(h, s):
        def finish(count, amp, rank2, e2):
            c_ref[s, h] = count
            a_ref[s, h] = amp
            r2_ref[s, h] = rank2
            e2_ref[s, h] = e2
        return finish

    @pl.when(jnp.logical_and(t == 0, e == 0))
    def _():
        prepare(x_ref, 0)

        def one(h, c):
            s1, s2 = route_scores(h)
            for run in _routing_stages(s1, s2, sub, table_writer(h, 0), 1):
                run(jnp.zeros((1, tq), F32))
            return c

        lax.fori_loop(0, PEER_HEADS, one, 0)

    @pl.when(e == 0)
    def _():
        acc_ref[...] = jnp.zeros_like(acc_ref)
        prepare(xnext_ref, nslot)
        s_ref[0], s_ref[1] = route_scores(0)

    ngrp = nk // BF16_ROWS
    xnt = xnt_ref[slot]
    first = pl.multiple_of(e * ib, ib)
    step_c = [c_ref[slot, h, pl.ds(first, ib), :] for h in range(PEER_HEADS)]
    step_a = [a_ref[slot, h, pl.ds(first, ib), :] for h in range(PEER_HEADS)]

    def expert_matmul(sc):
        lo = sc * isub * nk
        u_sub = pltpu.bitcast(u_ref[lo // 2:(lo + isub * nk) // 2, :], BF16)
        return jnp.dot(u_sub, xnt, preferred_element_type=F32)

    stages = _routing_stages(s_ref[0], s_ref[1], sub, table_writer(e, nslot), nsub)
    hid_next = expert_matmul(0)
    ahead_s1, ahead_s2 = route_scores(jnp.minimum(e + 1, PEER_HEADS - 1))

    total = None
    after = jnp.zeros((1, tq), F32)
    for sc in range(nsub):
        zero = jnp.broadcast_to(stages[sc](after) * 0.0, (BF16_ROWS, tq)).astype(BF16)
        lo = sc * isub * nk
        hid_sub = hid_next
        if sc + 1 < nsub:
            hid_next = expert_matmul(sc + 1)
        pieces = []
        for ii in range(isub):
            il = sc * isub + ii
            hid = hid_sub[ii * nk:(ii + 1) * nk, :]
            act = (0.5 * hid * (1.0 + lax.erf(hid * np.float32(np.sqrt(0.5))))).astype(BF16)
            w = [zero] * ngrp
            for h in range(PEER_HEADS):
                cnt = jnp.broadcast_to(step_c[h][il:il + 1, :], (BF16_ROWS, tq)).astype(BF16)
                amp = jnp.broadcast_to(step_a[h][il:il + 1, :], (BF16_ROWS, tq)).astype(BF16)
                for gi in range(ngrp):
                    rows = slice(gi * BF16_ROWS, (gi + 1) * BF16_ROWS)
                    w[gi] = w[gi] + jnp.where(r2_ref[slot, h, rows, :] < cnt,
                                              e2_ref[slot, h, rows, :] * amp, zero)
            pieces += [w[gi] * act[gi * BF16_ROWS:(gi + 1) * BF16_ROWS, :] for gi in range(ngrp)]
        after = pieces[-1][0:1, :].astype(F32) * 0.0
        vt_sub = pltpu.bitcast(vt_ref[:, lo:lo + isub * nk], BF16)
        part = jnp.dot(vt_sub, jnp.concatenate(pieces, axis=0), preferred_element_type=F32)
        total = part if total is None else total + part
    acc_ref[...] += total
    s_ref[0] = ahead_s1
    s_ref[1] = ahead_s2

    @pl.when(e == pl.num_programs(1) - 1)
    def _():
        o_ref[...] = x_ref[...] + acc_ref[...].T


def _pair_rows_kernel(x_ref, o_ref, *, transpose):
    x = x_ref[...]
    if transpose:
        x = x.T
    o_ref[...] = pltpu.bitcast(x.astype(BF16), jnp.uint32)


def _pair_rows(w, transpose=False, tr=512):
    R, C = w.shape
    if transpose:
        in_spec = pl.BlockSpec((tr, C), lambda i: (i, 0))
        out_spec = pl.BlockSpec((C // 2, tr), lambda i: (0, i))
        out_shape = jax.ShapeDtypeStruct((C // 2, R), jnp.uint32)
    else:
        in_spec = pl.BlockSpec((tr, C), lambda i: (i, 0))
        out_spec = pl.BlockSpec((tr // 2, C), lambda i: (i, 0))
        out_shape = jax.ShapeDtypeStruct((R // 2, C), jnp.uint32)
    return pl.pallas_call(
        functools.partial(_pair_rows_kernel, transpose=transpose),
        grid=(R // tr,), in_specs=[in_spec], out_specs=out_spec, out_shape=out_shape,
        compiler_params=_cparams(("parallel",)),
        name="pair_rows",
    )(w)


def peer_ffn(x2d, g, wq3_bf16, sk_bf16, u_pairs, vt_pairs, tq=256, isub=2):
    T, D = x2d.shape
    H = PEER_HEADS
    ne = H
    ib = PEER_NKEYS // ne
    ec = ib * PEER_NKEYS
    nt = T // tq
    full = lambda shape: pl.BlockSpec(shape, lambda i, e: (0,) * len(shape))
    words = pltpu.VMEM((2, H, PEER_NKEYS, tq), F32)
    halves = pltpu.VMEM((2, H, PEER_NKEYS, tq), BF16)
    return pl.pallas_call(
        functools.partial(_peer_kernel, ib=ib, isub=isub),
        grid=(nt, ne),
        in_specs=[pl.BlockSpec((tq, D), lambda i, e: (i, 0)),
                  pl.BlockSpec((tq, D), lambda i, e: (jnp.minimum(i + 1, nt - 1), 0)),
                  full((1, D)), full((H, D, 2 * PEER_NKEYS)), full((2 * H, PEER_NKEYS, PEER_NKEYS)),
                  pl.BlockSpec((ec // 2, D), lambda i, e: (e, 0)),
                  pl.BlockSpec((D // 2, ec), lambda i, e: (0, e))],
        out_specs=pl.BlockSpec((tq, D), lambda i, e: (i, 0)),
        out_shape=jax.ShapeDtypeStruct((T, D), F32),
        scratch_shapes=[pltpu.VMEM((tq, D), BF16), pltpu.VMEM((2, D, tq), BF16),
                        words, words, halves, halves, pltpu.VMEM((D, tq), F32),
                        pltpu.VMEM((2, PEER_NKEYS, tq), F32)],
        compiler_params=_cparams(("arbitrary", "arbitrary")),
        name="peer_ffn",
    )(x2d, x2d, g.reshape(1, D), wq3_bf16, sk_bf16, u_pairs, vt_pairs)


def _block_diag(w):
    G, n, _ = w.shape
    eye = jnp.eye(G, dtype=w.dtype)
    return (eye[:, None, :, None] * w[:, :, None, :]).reshape(G * n, G * n)


def even_layer(x2d, B, S, norm, w_in, gate_up, gate_b, out_norm, w_out):
    D = x2d.shape[1]
    gg0 = EV_GG * LANES
    wp = jnp.concatenate([w_in[:, :1536], w_in[:, 1536 + GLA_RANK:], w_in[:, 1536:1536 + GLA_RANK],
                          jnp.zeros((D, EV_COLS - gg0 - GLA_RANK), w_in.dtype)], axis=1).astype(BF16)
    proj = norm_proj(x2d, norm, wp).reshape(B, S, EV_COLS)
    gup = jnp.concatenate([gate_up, jnp.zeros((LANES - GLA_RANK, gate_up.shape[1]), gate_up.dtype)], axis=0)
    a_out = gla_mixer(proj, gup.astype(BF16), gate_b, out_norm)
    b_out = sb_attention(proj)
    return out_proj(x2d, a_out.reshape(B * S, -1), b_out.reshape(B * S, -1), w_out.astype(BF16))


def odd_layer(x2d, B, S, positions, norm, w_in, q_norm, k_norm, conv_w, conv_b, wa, ba, wx, bx, lam, w_out):
    proj = norm_proj(x2d, norm, w_in.astype(BF16)).reshape(B, S, -1)
    half = HEAD_DIM // 2
    inv_freq = ROPE_THETA ** (-jnp.arange(0, HEAD_DIM, 2, dtype=F32) / HEAD_DIM)
    inv_tile = jnp.tile(inv_freq, LANES // half).reshape(1, LANES)
    seg = np.arange(OD_W) // HEAD_DIM
    blockdiag = jnp.asarray(seg[:, None] == seg[None, :], dtype=BF16)
    q_t, k_rot, v_t, kmean = moba_prep(proj, positions.reshape(B, S, 1), inv_tile,
                                       jnp.tile(q_norm, OD_W // HEAD_DIM).reshape(1, OD_W),
                                       jnp.tile(k_norm, OD_W // HEAD_DIM).reshape(1, OD_W), blockdiag)
    c_out = moba_attention(q_t, k_rot, v_t, kmean.reshape(B, S // MOBA_BLOCK, OD_W))
    d_out = rg_lru_mixer(proj, conv_w, conv_b, _block_diag(wa).astype(BF16), ba,
                         _block_diag(wx).astype(BF16), bx, lam)
    return out_proj(x2d, c_out.reshape(B * S, -1), d_out.reshape(B * S, -1), w_out.astype(BF16))


def kernel(x, mem, positions, ev_norm, ev_w_in, ev_gla_gate_up, ev_gla_gate_b, ev_gla_out_norm, ev_w_out, od_norm, od_w_in, od_q_norm, od_k_norm, od_conv_w, od_conv_b, od_gate_a_w, od_gate_a_b, od_gate_x_w, od_gate_x_b, od_lambda, od_w_out, xa_norm, xa_mem_norm, xa_wq, xa_wkv, xa_q_norm, xa_k_norm, xa_wo, ffn_norm, peer_wq, peer_subkeys, peer_u, peer_v):
    B, S, D = x.shape
    M = mem.shape[1]
    depth = xa_norm.shape[0]
    x2d = x.reshape(B * S, D)
    mem2d = mem.reshape(B * M, D)
    for l in range(depth):
        if l % 2 == 0:
            e = l // 2
            x2d = even_layer(x2d, B, S, ev_norm[e], ev_w_in[e], ev_gla_gate_up[e], ev_gla_gate_b[e],
                             ev_gla_out_norm[e], ev_w_out[e])
        else:
            o = l // 2
            x2d = odd_layer(x2d, B, S, positions, od_norm[o], od_w_in[o], od_q_norm[o], od_k_norm[o],
                            od_conv_w[o], od_conv_b[o], od_gate_a_w[o], od_gate_a_b[o],
                            od_gate_x_w[o], od_gate_x_b[o], od_lambda[o], od_w_out[o])
        kv = norm_proj(mem2d, xa_mem_norm[l], xa_wkv[l].astype(BF16))
        x2d = cross_attention(x2d.reshape(B, S, D), kv.reshape(B, M, -1), xa_norm[l],
                              xa_wq[l].astype(BF16), xa_q_norm[l], xa_k_norm[l],
                              xa_wo[l].astype(BF16)).reshape(B * S, D)
        H = PEER_HEADS
        wq3 = peer_wq[l].reshape(D, H, 2 * PEER_NKEYS).transpose(1, 0, 2).astype(BF16)
        sk = peer_subkeys[l].reshape(2 * H, PEER_NKEYS, -1).astype(BF16)
        x2d = peer_ffn(x2d, ffn_norm[l], wq3, sk, _pair_rows(peer_u[l]),
                       _pair_rows(peer_v[l], transpose=True))
    return x2d.reshape(B, S, D)
```

```python
import functools

import jax
import jax.numpy as jnp
import numpy as np
from jax import lax
from jax.experimental import pallas as pl
from jax.experimental.pallas import tpu as pltpu

F32 = jnp.float32
BF16 = jnp.bfloat16

EPS = 1e-6
ROPE_THETA = 10000.0
LANES = 128
HEAD_DIM = 64
GLA_HEADS = 4
GLA_DV = 128
GLA_CHUNK = 64
GLA_TAU = 16.0
GLA_RANK = 16
MOBA_BLOCK = 256
MOBA_TOPK = 3
RG_C = 8.0
RG_CONV = 4
XA_HEADS = 4
XA_DH = 128
PEER_HEADS = 8
PEER_NKEYS = 128
PEER_TOPK = 16
NEG = -1e30
F32_LOG_TINY = -104.0
VMEM_LIMIT = 56 * 1024 * 1024


def _cparams(sem):
    return pltpu.CompilerParams(dimension_semantics=sem, vmem_limit_bytes=VMEM_LIMIT)


def _dot(a, b):
    return jnp.dot(a.astype(BF16), b.astype(BF16), preferred_element_type=F32)


def _dot_nt(a, b):
    return lax.dot_general(a.astype(BF16), b.astype(BF16), (((1,), (1,)), ((), ())),
                           preferred_element_type=F32)


def _dot_tn(a, b):
    return lax.dot_general(a.astype(BF16), b.astype(BF16), (((0,), (0,)), ((), ())),
                           preferred_element_type=F32)


def _split2(a):
    hi = a.astype(BF16)
    lo = (a - hi.astype(F32)).astype(BF16)
    return hi, lo


def _dot_exact_rhs(a, m_bf16):
    hi, lo = _split2(a)
    return (jnp.dot(hi, m_bf16, preferred_element_type=F32)
            + jnp.dot(lo, m_bf16, preferred_element_type=F32))


def _dot_exact_lhs(m_bf16, a):
    hi, lo = _split2(a)
    return (jnp.dot(m_bf16, hi, preferred_element_type=F32)
            + jnp.dot(m_bf16, lo, preferred_element_type=F32))


def _rms(x, g):
    return x * lax.rsqrt(jnp.mean(x * x, axis=-1, keepdims=True) + EPS) * g


def _softplus(z):
    return jnp.maximum(z, 0.0) + jnp.log1p(jnp.exp(-jnp.abs(z)))


def _head_mask(hh):
    lane = lax.broadcasted_iota(jnp.int32, (1, LANES), 1)
    return ((lane // HEAD_DIM) == hh).astype(F32)


def _norm_proj_kernel(x_ref, g_ref, w_ref, o_ref):
    xn = _rms(x_ref[...], g_ref[...])
    o_ref[...] = jnp.dot(xn.astype(BF16), w_ref[...], preferred_element_type=F32)


def norm_proj(x2d, g, w_bf16, tm=256):
    T, D = x2d.shape
    N = w_bf16.shape[1]
    return pl.pallas_call(
        _norm_proj_kernel,
        grid=(T // tm,),
        in_specs=[pl.BlockSpec((tm, D), lambda i: (i, 0)),
                  pl.BlockSpec((1, D), lambda i: (0, 0)),
                  pl.BlockSpec((D, N), lambda i: (0, 0))],
        out_specs=pl.BlockSpec((tm, N), lambda i: (i, 0)),
        out_shape=jax.ShapeDtypeStruct((T, N), F32),
        compiler_params=_cparams(("parallel",)),
        name="norm_proj",
    )(x2d, g.reshape(1, D), w_bf16)


def _out_proj_kernel(x_ref, a_ref, b_ref, w_ref, o_ref):
    ka = a_ref.shape[1]
    o_ref[...] = (x_ref[...] + _dot(a_ref[...], w_ref[0:ka, :])
                  + _dot(b_ref[...], w_ref[ka:, :]))


def out_proj(x2d, a2d, b2d, w_bf16, tm=512):
    T, D = x2d.shape
    ka, kb = a2d.shape[1], b2d.shape[1]
    return pl.pallas_call(
        _out_proj_kernel,
        grid=(T // tm,),
        in_specs=[pl.BlockSpec((tm, D), lambda i: (i, 0)),
                  pl.BlockSpec((tm, ka), lambda i: (i, 0)),
                  pl.BlockSpec((tm, kb), lambda i: (i, 0)),
                  pl.BlockSpec((ka + kb, D), lambda i: (0, 0))],
        out_specs=pl.BlockSpec((tm, D), lambda i: (i, 0)),
        out_shape=jax.ShapeDtypeStruct((T, D), F32),
        compiler_params=_cparams(("parallel",)),
        name="out_proj",
    )(x2d, a2d, b2d, w_bf16)


EV_Q, EV_K, EV_V, EV_R, EV_SQ, EV_SK, EV_SV, EV_GG = 0, 2, 4, 8, 12, 16, 20, 24
EV_COLS = 25 * LANES


def _gla_kernel(q_ref, k_ref, v_ref, r_ref, gg_ref, gup_ref, gb_ref, onorm_ref, o_ref, state_ref,
                *, ts):
    C = GLA_CHUNK

    @pl.when(pl.program_id(1) == 0)
    def _():
        state_ref[...] = jnp.zeros_like(state_ref)

    row = lax.broadcasted_iota(jnp.int32, (C, C), 0)
    col = lax.broadcasted_iota(jnp.int32, (C, C), 1)
    causal = row >= col
    tri = causal.astype(BF16)
    masks = [_head_mask(0), _head_mask(1)]
    scale = HEAD_DIM ** -0.5
    onorm = onorm_ref[...]

    for ci in range(ts // C):
        sl = slice(ci * C, (ci + 1) * C)
        pre = _dot(gg_ref[0, sl, :], gup_ref[...]) + gb_ref[...]
        g = (jnp.minimum(pre, 0.0) - jnp.log1p(jnp.exp(-jnp.abs(pre)))) * (1.0 / GLA_TAU)
        b = _dot_exact_lhs(tri, g)
        bmid = b[C // 2 - 1:C // 2, :]
        blast = b[C - 1:C, :]
        q = q_ref[0, sl, :] * scale
        k = k_ref[0, sl, :]
        qd = q * jnp.exp(b - bmid)
        kd = k * jnp.exp(bmid - b)
        qe = q * jnp.exp(b)
        kdec = k * jnp.exp(blast - b)
        eb_last = jnp.exp(blast)
        for p in range(GLA_HEADS // 2):
            lanes = slice(p * LANES, (p + 1) * LANES)
            st = state_ref[p]
            new_st = st * eb_last[:, lanes]
            for hh in range(2):
                h = 2 * p + hh
                m = masks[hh]
                hl = slice(h * GLA_DV, (h + 1) * GLA_DV)
                a = _dot_nt(qd[:, lanes] * m, kd[:, lanes])
                a = jnp.where(causal, a, 0.0)
                v_h = v_ref[0, sl, hl]
                o = _dot(a, v_h) + _dot_nt(qe[:, lanes] * m, st)
                o = _rms(o, onorm)
                r_h = r_ref[0, sl, hl]
                o_ref[0, sl, hl] = o * (r_h * jax.nn.sigmoid(r_h))
                new_st = new_st + _dot_tn(v_h, kdec[:, lanes] * m)
            state_ref[p] = new_st


def gla_mixer(proj3, gate_up_pad, gate_b, out_norm, ts=256):
    B, S, _ = proj3.shape
    nh = GLA_HEADS
    w = nh * GLA_DV

    def col(blk_w, tile):
        idx = tile * LANES // blk_w
        return pl.BlockSpec((1, ts, blk_w), lambda b, c: (b, c, idx))

    return pl.pallas_call(
        functools.partial(_gla_kernel, ts=ts),
        grid=(B, S // ts),
        in_specs=[col(nh * HEAD_DIM, EV_Q), col(nh * HEAD_DIM, EV_K), col(w, EV_V), col(w, EV_R),
                  col(LANES, EV_GG),
                  pl.BlockSpec((LANES, nh * HEAD_DIM), lambda b, c: (0, 0)),
                  pl.BlockSpec((1, nh * HEAD_DIM), lambda b, c: (0, 0)),
                  pl.BlockSpec((1, GLA_DV), lambda b, c: (0, 0))],
        out_specs=pl.BlockSpec((1, ts, w), lambda b, c: (b, c, 0)),
        out_shape=jax.ShapeDtypeStruct((B, S, w), F32),
        scratch_shapes=[pltpu.VMEM((nh // 2, GLA_DV, LANES), F32)],
        compiler_params=_cparams(("parallel", "arbitrary")),
        name="gla",
    )(proj3, proj3, proj3, proj3, proj3, gate_up_pad, gate_b.reshape(1, -1), out_norm.reshape(1, -1))


def _sb_kernel(q_ref, k_ref, v_ref, o_ref, acc_ref, carry_ref, *, tq):
    i = pl.program_id(2)
    q = q_ref[0] * (HEAD_DIM ** -0.5)
    masks = [_head_mask(0), _head_mask(1)]
    qh = [(q * m).astype(BF16) for m in masks]
    row = lax.broadcasted_iota(jnp.int32, (tq, tq), 0)
    col = lax.broadcasted_iota(jnp.int32, (tq, tq), 1)
    upper = (row > col).astype(BF16)
    past = col < row
    acc_ref[...] = jnp.zeros_like(acc_ref)
    carry_ref[...] = jnp.zeros_like(carry_ref)

    def tile(j, diag):
        off = pl.multiple_of(j * tq, tq)
        k = k_ref[0, pl.ds(off, tq), :].astype(BF16)
        v = v_ref[0, pl.ds(off, tq), :].astype(BF16)
        for hh in range(2):
            z = lax.dot_general(qh[hh], k, (((1,), (1,)), ((), ())), preferred_element_type=F32)
            sp = _softplus(z)
            log_rem = jnp.where(past, -sp, 0.0) if diag else -sp
            carry = carry_ref[hh]
            after = _dot_exact_rhs(log_rem, upper) + carry
            w = jnp.exp((z - sp) + after)
            if diag:
                w = jnp.where(past, w, 0.0)
            acc_ref[hh] += jnp.dot(w.astype(BF16), v, preferred_element_type=F32)
            carry_ref[hh] = carry + jnp.sum(log_rem, axis=1, keepdims=True)

    def first_two_tiles():
        off = pl.multiple_of((i - 1) * tq, tq)
        k = k_ref[0, pl.ds(off, 2 * tq), :].astype(BF16)
        v = v_ref[0, pl.ds(off, 2 * tq), :].astype(BF16)
        for hh in range(2):
            z = lax.dot_general(qh[hh], k, (((1,), (1,)), ((), ())), preferred_element_type=F32)
            sp = _softplus(z)
            z_l, z_r, sp_l, sp_r = z[:, :tq], z[:, tq:], sp[:, :tq], sp[:, tq:]
            rem_r = jnp.where(past, -sp_r, 0.0)
            after_r = _dot_exact_rhs(rem_r, upper)
            tot_r = jnp.sum(rem_r, axis=1, keepdims=True)
            after_l = _dot_exact_rhs(-sp_l, upper) + tot_r
            w_r = jnp.where(past, jnp.exp((z_r - sp_r) + after_r), 0.0)
            w_l = jnp.exp((z_l - sp_l) + after_l)
            w = jnp.concatenate([w_l, w_r], axis=1).astype(BF16)
            acc_ref[hh] = jnp.dot(w, v, preferred_element_type=F32)
            carry_ref[hh] = tot_r - jnp.sum(sp_l, axis=1, keepdims=True)

    @pl.when(i == 0)
    def _():
        tile(i, True)

    @pl.when(i > 0)
    def _():
        first_two_tiles()

    def live(c):
        j, worst = c
        return jnp.logical_and(j >= 0, worst > F32_LOG_TINY)

    def body(c):
        j, _ = c
        tile(j, False)
        return j - 1, jnp.max(carry_ref[...])

    lax.while_loop(live, body, (i - 2, jnp.max(carry_ref[...])))
    o_ref[0] = acc_ref[0] * masks[0] + acc_ref[1] * masks[1]


def sb_attention(proj3, tq=256):
    B, S, _ = proj3.shape
    npairs = 4
    return pl.pallas_call(
        functools.partial(_sb_kernel, tq=tq),
        grid=(B, npairs, S // tq),
        in_specs=[pl.BlockSpec((1, tq, LANES), lambda b, p, i: (b, i, EV_SQ + p)),
                  pl.BlockSpec((1, S, LANES), lambda b, p, i: (b, 0, EV_SK + p)),
                  pl.BlockSpec((1, S, LANES), lambda b, p, i: (b, 0, EV_SV + p))],
        out_specs=pl.BlockSpec((1, tq, LANES), lambda b, p, i: (b, i, p)),
        out_shape=jax.ShapeDtypeStruct((B, S, npairs * LANES), F32),
        scratch_shapes=[pltpu.VMEM((2, tq, LANES), F32), pltpu.VMEM((2, tq, 1), F32)],
        compiler_params=_cparams(("parallel", "parallel", "arbitrary")),
        name="sb_attention",
    )(proj3, proj3, proj3)


OD_Q, OD_K, OD_V, OD_RX, OD_RG = 0, 4, 8, 12, 16
OD_W = 4 * LANES


def _moba_prep_kernel(q_ref, k_ref, v_ref, pos_ref, invf_ref, qn_ref, kn_ref, bd_ref,
                      qt_ref, ko_ref, vt_ref, km_ref):
    ang = pos_ref[0].astype(F32) * invf_ref[...]
    cos1, sin1 = jnp.cos(ang), jnp.sin(ang)
    lane = lax.broadcasted_iota(jnp.int32, (1, LANES), 1)
    first_half = (lane % HEAD_DIM) < (HEAD_DIM // 2)
    bd = bd_ref[...]

    def norm_rope(x, gain):
        ms = _dot_exact_rhs(x * x, bd) * (1.0 / HEAD_DIM)
        xn = x * lax.rsqrt(ms + EPS) * gain
        outs = []
        for t in range(OD_W // LANES):
            xb = xn[:, t * LANES:(t + 1) * LANES]
            up = pltpu.roll(xb, LANES - HEAD_DIM // 2, axis=1)
            dn = pltpu.roll(xb, HEAD_DIM // 2, axis=1)
            outs.append(xb * cos1 + jnp.where(first_half, -up, dn) * sin1)
        return jnp.concatenate(outs, axis=1)

    qt_ref[0] = norm_rope(q_ref[0], qn_ref[...]).T.astype(BF16)
    kr = norm_rope(k_ref[0], kn_ref[...])
    ko_ref[0] = kr.astype(BF16)
    km_ref[0, 0] = jnp.mean(kr, axis=0, keepdims=True)
    vt_ref[0] = v_ref[0].T.astype(BF16)


def moba_prep(proj3, pos3, inv_freq_tile, qn_tile, kn_tile, blockdiag):
    B, S, _ = proj3.shape
    tb = MOBA_BLOCK
    nb = S // tb
    full = lambda shape: pl.BlockSpec(shape, lambda b, i: (0,) * len(shape))
    return pl.pallas_call(
        _moba_prep_kernel,
        grid=(B, nb),
        in_specs=[pl.BlockSpec((1, tb, OD_W), lambda b, i: (b, i, OD_Q // 4)),
                  pl.BlockSpec((1, tb, OD_W), lambda b, i: (b, i, OD_K // 4)),
                  pl.BlockSpec((1, tb, OD_W), lambda b, i: (b, i, OD_V // 4)),
                  pl.BlockSpec((1, tb, 1), lambda b, i: (b, i, 0)),
                  full((1, LANES)), full((1, OD_W)), full((1, OD_W)), full((OD_W, OD_W))],
        out_specs=[pl.BlockSpec((1, OD_W, tb), lambda b, i: (b, 0, i)),
                   pl.BlockSpec((1, tb, OD_W), lambda b, i: (b, i, 0)),
                   pl.BlockSpec((1, OD_W, tb), lambda b, i: (b, 0, i)),
                   pl.BlockSpec((1, 1, 1, OD_W), lambda b, i: (b, i, 0, 0))],
        out_shape=[jax.ShapeDtypeStruct((B, OD_W, S), BF16),
                   jax.ShapeDtypeStruct((B, S, OD_W), BF16),
                   jax.ShapeDtypeStruct((B, OD_W, S), BF16),
                   jax.ShapeDtypeStruct((B, nb, 1, OD_W), F32)],
        compiler_params=_cparams(("parallel", "parallel")),
        name="moba_prep",
    )(proj3, proj3, proj3, pos3, inv_freq_tile, qn_tile, kn_tile, blockdiag)


def _moba_kernel(qt_ref, k_ref, vt_ref, km_ref, o_ref, acc_ref, m_ref, sel_ref):
    tb = MOBA_BLOCK
    nbp = km_ref.shape[1]
    i = pl.program_id(2)
    qt = qt_ref[0]
    km = km_ref[0].astype(BF16)
    chan = lax.broadcasted_iota(jnp.int32, (LANES, 1), 0)
    in_head = [(chan // HEAD_DIM) == hh for hh in range(2)]
    blk = lax.broadcasted_iota(jnp.int32, (nbp, tb), 0).astype(F32)
    own = i.astype(F32)
    key = lax.broadcasted_iota(jnp.int32, (tb, tb), 0)
    qry = lax.broadcasted_iota(jnp.int32, (tb, tb), 1)
    visible = key <= qry
    off_own = pl.multiple_of(i * tb, tb)
    k_own = k_ref[0, pl.ds(off_own, tb), :]
    vt_own = vt_ref[0, :, pl.ds(off_own, tb)]
    zero = jnp.zeros_like(qt)
    one = jnp.ones_like(qt)
    qs = []
    for hh in range(2):
        qm = jnp.where(in_head[hh], qt, zero)
        gate = jnp.dot(km, qm, preferred_element_type=F32)
        g = jnp.where(blk < own, gate, -jnp.inf)
        sel = jnp.zeros((nbp, tb), F32)
        for r in range(MOBA_TOPK):
            mx = jnp.max(g, axis=0, keepdims=True)
            idx = jnp.min(jnp.where(g == mx, blk, float(nbp)), axis=0, keepdims=True)
            hit = blk == idx
            keep = jnp.where(i > r, 1.0, 0.0)
            sel = sel + jnp.where(hit, keep, 0.0)
            g = jnp.where(hit, -jnp.inf, g)
        sel_ref[hh] = sel
        qb = qm * (HEAD_DIM ** -0.5)
        qs.append(qb)
        s = jnp.dot(k_own, qb, preferred_element_type=F32)
        s = jnp.where(visible, s, NEG)
        m0 = jnp.max(s, axis=0, keepdims=True)
        p = jnp.exp(s - m0)
        m_ref[hh] = m0
        acc_ref[hh] = jnp.dot(jnp.where(in_head[hh], vt_own, one), p.astype(BF16),
                              preferred_element_type=F32)

    def past_blocks(n, nblk):
        off = pl.multiple_of(n * tb, tb)
        k = k_ref[0, pl.ds(off, nblk * tb), :]
        vt = vt_ref[0, :, pl.ds(off, nblk * tb)]
        scores = [jnp.dot(k, qs[hh], preferred_element_type=F32) for hh in range(2)]
        probs, alphas = [], []
        for hh in range(2):
            parts = []
            for j in range(nblk):
                chosen = sel_ref[hh, pl.ds(n + j, 1), :]
                parts.append(jnp.where(chosen > 0.0, scores[hh][j * tb:(j + 1) * tb, :], NEG))
            m_old = m_ref[hh]
            m_new = m_old
            for part in parts:
                m_new = jnp.maximum(m_new, jnp.max(part, axis=0, keepdims=True))
            alphas.append(jnp.exp(m_old - m_new))
            probs.append(jnp.concatenate([jnp.exp(part - m_new).astype(BF16) for part in parts], axis=0))
            m_ref[hh] = m_new
        ones_v = jnp.ones_like(vt)
        pv = [jnp.dot(jnp.where(in_head[hh], vt, ones_v), probs[hh], preferred_element_type=F32)
              for hh in range(2)]
        for hh in range(2):
            acc_ref[hh] = acc_ref[hh] * alphas[hh] + pv[hh]

    def four_blocks(t, c):
        past_blocks(4 * t, 4)
        return c

    lax.fori_loop(0, i // 4, four_blocks, 0)
    rem = i % 4

    @pl.when(rem >= 2)
    def _():
        past_blocks(i - rem, 2)

    @pl.when(rem % 2 == 1)
    def _():
        past_blocks(i - 1, 1)

    a0, a1 = acc_ref[0], acc_ref[1]
    den0 = a0[HEAD_DIM:HEAD_DIM + 1, :]
    den1 = a1[0:1, :]
    o_ref[0] = jnp.where(in_head[0], a0 / den0, a1 / den1).T


def moba_attention(q_t, k_rot, v_t, kmean):
    B, S, _ = k_rot.shape
    tb = MOBA_BLOCK
    nb = S // tb
    npairs = OD_W // LANES
    nbp = -(-nb // BF16_ROWS) * BF16_ROWS
    kmean = jnp.pad(kmean, ((0, 0), (0, nbp - nb), (0, 0)))
    return pl.pallas_call(
        _moba_kernel,
        grid=(B, npairs, nb),
        in_specs=[pl.BlockSpec((1, LANES, tb), lambda b, p, i: (b, p, i)),
                  pl.BlockSpec((1, S, LANES), lambda b, p, i: (b, 0, p)),
                  pl.BlockSpec((1, LANES, S), lambda b, p, i: (b, p, 0)),
                  pl.BlockSpec((1, nbp, LANES), lambda b, p, i: (b, 0, p))],
        out_specs=pl.BlockSpec((1, tb, LANES), lambda b, p, i: (b, i, p)),
        out_shape=jax.ShapeDtypeStruct((B, S, OD_W), F32),
        scratch_shapes=[pltpu.VMEM((2, LANES, tb), F32), pltpu.VMEM((2, 1, tb), F32),
                        pltpu.VMEM((2, nbp, tb), F32)],
        compiler_params=_cparams(("parallel", "parallel", "arbitrary")),
        name="moba_attention",
    )(q_t, k_rot, v_t, kmean)


def _rglru_kernel(x_ref, gate_ref, cw_ref, cb_ref, wa_ref, ba_ref, wx_ref, bx_ref, lam_ref, o_ref,
                  buf_ref, xprev_ref, hprev_ref, *, ts):
    @pl.when(pl.program_id(1) == 0)
    def _():
        xprev_ref[...] = jnp.zeros_like(xprev_ref)
        hprev_ref[...] = jnp.zeros_like(hprev_ref)

    x = x_ref[0]
    buf_ref[0:8, :] = xprev_ref[...]
    buf_ref[8:8 + ts, :] = x
    xprev_ref[...] = x[ts - 8:ts, :]
    xc = cb_ref[...]
    for kk in range(RG_CONV):
        start = 8 - (RG_CONV - 1) + kk
        xc = xc + cw_ref[kk:kk + 1, :] * buf_ref[start:start + ts, :]
    r = jax.nn.sigmoid(_dot(xc, wa_ref[...]) + ba_ref[...])
    ig = jax.nn.sigmoid(_dot(xc, wx_ref[...]) + bx_ref[...])
    log_a = (-RG_C) * r * _softplus(-lam_ref[...])
    a = jnp.exp(log_a)
    u = jnp.sqrt(-jnp.tanh(log_a) * (a * a + 1.0)) * (ig * xc)
    t_idx = lax.broadcasted_iota(jnp.int32, (ts, 1), 0)
    d = 1
    while d < ts:
        valid = t_idx >= d
        a_sh = pltpu.roll(a, d, axis=0)
        u_sh = pltpu.roll(u, d, axis=0)
        u = jnp.where(valid, a * u_sh, 0.0) + u
        a = jnp.where(valid, a * a_sh, a)
        d *= 2
    h = a * hprev_ref[...] + u
    hprev_ref[...] = h[ts - 1:ts, :]
    o_ref[0] = h * jax.nn.gelu(gate_ref[0], approximate=True)


def rg_lru_mixer(proj3, conv_w, conv_b, wa_bd, ba, wx_bd, bx, lam, ts=256):
    B, S, _ = proj3.shape
    W = conv_w.shape[1]
    full = lambda shape: pl.BlockSpec(shape, lambda b, t: (0,) * len(shape))
    row = lambda v: v.reshape(1, W)
    return pl.pallas_call(
        functools.partial(_rglru_kernel, ts=ts),
        grid=(B, S // ts),
        in_specs=[pl.BlockSpec((1, ts, W), lambda b, t: (b, t, OD_RX // 4)),
                  pl.BlockSpec((1, ts, W), lambda b, t: (b, t, OD_RG // 4)),
                  full((RG_CONV, W)), full((1, W)), full((W, W)), full((1, W)), full((W, W)),
                  full((1, W)), full((1, W))],
        out_specs=pl.BlockSpec((1, ts, W), lambda b, t: (b, t, 0)),
        out_shape=jax.ShapeDtypeStruct((B, S, W), F32),
        scratch_shapes=[pltpu.VMEM((ts + 8, W), F32), pltpu.VMEM((8, W), F32), pltpu.VMEM((1, W), F32)],
        compiler_params=_cparams(("parallel", "arbitrary")),
        name="rg_lru",
    )(proj3, proj3, conv_w, row(conv_b), wa_bd, row(ba), wx_bd, row(bx), row(lam))


def _xattn_kernel(x_ref, kv_ref, g_ref, wq_ref, qn_ref, kn_ref, wo_ref, o_ref):
    x = x_ref[0]
    q = _dot(_rms(x, g_ref[...]), wq_ref[...])
    kv = kv_ref[0]
    hw = XA_HEADS * XA_DH
    outs = []
    for h in range(XA_HEADS):
        hl = slice(h * XA_DH, (h + 1) * XA_DH)
        qh = _rms(q[:, hl], qn_ref[...])
        kh = _rms(kv[:, hl], kn_ref[...])
        vh = kv[:, hw + h * XA_DH: hw + (h + 1) * XA_DH]
        s = _dot_nt(qh, kh) * (XA_DH ** -0.5)
        s = s - jnp.max(s, axis=1, keepdims=True)
        p = jnp.exp(s)
        p = p / jnp.sum(p, axis=1, keepdims=True)
        outs.append(_dot(p, vh))
    o_ref[0] = x + _dot(jnp.concatenate(outs, axis=1), wo_ref[...])


def cross_attention(x3, kv3, g, wq_bf16, qn, kn, wo_bf16, tq=256):
    B, S, D = x3.shape
    M = kv3.shape[1]
    hw = XA_HEADS * XA_DH
    full = lambda shape: pl.BlockSpec(shape, lambda b, i: (0,) * len(shape))
    return pl.pallas_call(
        _xattn_kernel,
        grid=(B, S // tq),
        in_specs=[pl.BlockSpec((1, tq, D), lambda b, i: (b, i, 0)),
                  pl.BlockSpec((1, M, 2 * hw), lambda b, i: (b, 0, 0)),
                  full((1, D)), full((D, hw)), full((1, XA_DH)), full((1, XA_DH)), full((hw, D))],
        out_specs=pl.BlockSpec((1, tq, D), lambda b, i: (b, i, 0)),
        out_shape=jax.ShapeDtypeStruct((B, S, D), F32),
        compiler_params=_cparams(("parallel", "parallel")),
        name="cross_attention",
    )(x3, kv3, g.reshape(1, D), wq_bf16, qn.reshape(1, -1), kn.reshape(1, -1), wo_bf16)


BF16_ROWS = 16


def _top_values(s, k, want_rank=False):
    rows = []
    rank = jnp.full(s.shape, float(k), F32) if want_rank else None
    for r in range(k):
        m = jnp.max(s, axis=0, keepdims=True)
        rows.append(m)
        hit = s == m
        if want_rank:
            rank = jnp.where(hit, float(r), rank)
        s = jnp.where(hit, -jnp.inf, s)
    return rows, rank


def _routing_stages(s1, s2, sub, finish, nstage):
    K = PEER_TOPK
    st = {"a1": s1, "a2": s2, "v1": [], "v2": [], "rank2": jnp.full(s2.shape, float(K), F32)}

    def extract(src, dst, rounds, want_rank):
        def run(nil):
            arr = st[src]
            gone = nil - jnp.inf
            for r in rounds:
                m = jnp.max(arr, axis=0, keepdims=True)
                st[dst].append(m)
                hit = arr == m
                if want_rank:
                    st["rank2"] = jnp.where(hit, float(r), st["rank2"])
                arr = jnp.where(hit, gone, arr)
            st[src] = arr
            st["last"] = m
        return run

    def candidates(nil):
        v1, v2 = st["v1"], st["v2"]
        v1m = jnp.concatenate(v1, axis=0)
        v2m = jnp.concatenate(v2, axis=0)
        groups = [(v1[0] + nil) + v2m]
        for a in range(1, 8):
            groups.append(jnp.where(sub < K // (a + 1), v1[a] + v2m[0:8], -jnp.inf))
        groups.append(v1m[8:16] + v2[0])
        st["cand"] = jnp.concatenate(groups, axis=0)
        st["top"] = []
        st["last"] = v2[0]

    def threshold(nil):
        top = st["top"]
        st["tau"] = top[K - 1]
        z = nil
        for tv in top:
            z = z + jnp.exp(tv - top[0])
        st["z"] = z
        st["last"] = z

    def counts(nil):
        v1, v2, tau = st["v1"], st["v2"], st["tau"]
        count = jnp.zeros(s1.shape, F32)
        for bb in range(K // 2):
            count = count + jnp.where(s1 + v2[bb] >= tau, 1.0, 0.0)
        best = nil
        for bb in range(K // 2, K):
            best = best + jnp.where(v1[0] + v2[bb] >= tau, 1.0, 0.0)
        count = count + jnp.where(s1 == v1[0], best, 0.0)
        finish(count, jnp.exp(s1 - v1[0]) / st["z"], st["rank2"].astype(BF16),
               jnp.exp(s2 - v2[0]).astype(BF16))
        st["last"] = best

    quarter = [range(q * 4, q * 4 + 4) for q in range(4)]
    work = ([extract("a1", "v1", r, False) for r in quarter]
            + [extract("a2", "v2", r, True) for r in quarter]
            + [candidates] + [extract("cand", "top", r, False) for r in quarter]
            + [threshold, counts])
    bounds = [round(j * len(work) / nstage) for j in range(nstage + 1)]

    def stage(j):
        def run(nil):
            for piece in work[bounds[j]:bounds[j + 1]]:
                piece(nil)
            return st["last"]
        return run

    return [stage(j) for j in range(nstage)]


def _peer_kernel(x_ref, xnext_ref, g_ref, wq_ref, sk_ref, u_ref, vt_ref, o_ref,
                 xn_ref, xnt_ref, c_ref, a_ref, r2_ref, e2_ref, acc_ref, s_ref, *, ib, isub):
    t = pl.program_id(0)
    e = pl.program_id(1)
    nk = PEER_NKEYS
    tq = x_ref.shape[0]
    slot = t % 2
    nslot = 1 - slot
    sub = lax.broadcasted_iota(jnp.int32, (8, 1), 0)
    nsub = ib // isub

    def prepare(src_ref, s):
        xn32 = _rms(src_ref[...], g_ref[...])
        xn_ref[...] = xn32.astype(BF16)
        xnt_ref[s] = xn32.T.astype(BF16)

    def route_scores(h):
        q = jnp.dot(xn_ref[...], wq_ref[h], preferred_element_type=F32)
        return _dot_nt(sk_ref[2 * h], q[:, :nk]), _dot_nt(sk_ref[2 * h + 1], q[:, nk:])

    def table_writer(h, s):
        def finish(count, amp, rank2, e2):
            c_ref[s, h] = count
            a_ref[s, h] = amp
            r2_ref[s, h] = rank2
            e2_ref[s, h] = e2
        return finish

    @pl.when(jnp.logical_and(t == 0, e == 0))
    def _():
        prepare(x_ref, 0)

        def one(h, c):
            s1, s2 = route_scores(h)
            for run in _routing_stages(s1, s2, sub, table_writer(h, 0), 1):
                run(jnp.zeros((1, tq), F32))
            return c

        lax.fori_loop(0, PEER_HEADS, one, 0)

    @pl.when(e == 0)
    def _():
        acc_ref[...] = jnp.zeros_like(acc_ref)
        prepare(xnext_ref, nslot)
        s_ref[0], s_ref[1] = route_scores(0)

    ngrp = nk // BF16_ROWS
    xnt = xnt_ref[slot]
    first = pl.multiple_of(e * ib, ib)
    step_c = [c_ref[slot, h, pl.ds(first, ib), :] for h in range(PEER_HEADS)]
    step_a = [a_ref[slot, h, pl.ds(first, ib), :] for h in range(PEER_HEADS)]

    def expert_matmul(sc):
        lo = sc * isub * nk
        u_sub = pltpu.bitcast(u_ref[lo // 2:(lo + isub * nk) // 2, :], BF16)
        return jnp.dot(u_sub, xnt, preferred_element_type=F32)

    stages = _routing_stages(s_ref[0], s_ref[1], sub, table_writer(e, nslot), nsub)
    hid_next = expert_matmul(0)
    ahead_s1, ahead_s2 = route_scores(jnp.minimum(e + 1, PEER_HEADS - 1))

    total = None
    after = jnp.zeros((1, tq), F32)
    for sc in range(nsub):
        zero = jnp.broadcast_to(stages[sc](after) * 0.0, (BF16_ROWS, tq)).astype(BF16)
        lo = sc * isub * nk
        hid_sub = hid_next
        if sc + 1 < nsub:
            hid_next = expert_matmul(sc + 1)
        pieces = []
        for ii in range(isub):
            il = sc * isub + ii
            hid = hid_sub[ii * nk:(ii + 1) * nk, :]
            act = (0.5 * hid * (1.0 + lax.erf(hid * np.float32(np.sqrt(0.5))))).astype(BF16)
            w = [zero] * ngrp
            for h in range(PEER_HEADS):
                cnt = jnp.broadcast_to(step_c[h][il:il + 1, :], (BF16_ROWS, tq)).astype(BF16)
                amp = jnp.broadcast_to(step_a[h][il:il + 1, :], (BF16_ROWS, tq)).astype(BF16)
                for gi in range(ngrp):
                    rows = slice(gi * BF16_ROWS, (gi + 1) * BF16_ROWS)
                    w[gi] = w[gi] + jnp.where(r2_ref[slot, h, rows, :] < cnt,
                                              e2_ref[slot, h, rows, :] * amp, zero)
            pieces += [w[gi] * act[gi * BF16_ROWS:(gi + 1) * BF16_ROWS, :] for gi in range(ngrp)]
        after = pieces[-1][0:1, :].astype(F32) * 0.0
        vt_sub = pltpu.bitcast(vt_ref[:, lo:lo + isub * nk], BF16)
        part = jnp.dot(vt_sub, jnp.concatenate(pieces, axis=0), preferred_element_type=F32)
        total = part if total is None else total + part
    acc_ref[...] += total
    s_ref[0] = ahead_s1
    s_ref[1] = ahead_s2

    @pl.when(e == pl.num_programs(1) - 1)
    def _():
        o_ref[...] = x_ref[...] + acc_ref[...].T


def _pair_rows_kernel(x_ref, o_ref, *, transpose):
    x = x_ref[0]
    if transpose:
        x = x.T
    o_ref[...] = pltpu.bitcast(x.astype(BF16), jnp.uint32)


def _pair_rows(w3, layer, transpose=False, tr=512):
    _, R, C = w3.shape
    in_spec = pl.BlockSpec((1, tr, C), lambda i: (layer, i, 0))
    if transpose:
        out_spec = pl.BlockSpec((C // 2, tr), lambda i: (0, i))
        out_shape = jax.ShapeDtypeStruct((C // 2, R), jnp.uint32)
    else:
        out_spec = pl.BlockSpec((tr // 2, C), lambda i: (i, 0))
        out_shape = jax.ShapeDtypeStruct((R // 2, C), jnp.uint32)
    return pl.pallas_call(
        functools.partial(_pair_rows_kernel, transpose=transpose),
        grid=(R // tr,), in_specs=[in_spec], out_specs=out_spec, out_shape=out_shape,
        compiler_params=_cparams(("parallel",)),
        name="pair_rows",
    )(w3)


def peer_ffn(x2d, g, wq3_bf16, sk_bf16, u_pairs, vt_pairs, tq=256, isub=2):
    T, D = x2d.shape
    H = PEER_HEADS
    ne = H
    ib = PEER_NKEYS // ne
    ec = ib * PEER_NKEYS
    nt = T // tq
    full = lambda shape: pl.BlockSpec(shape, lambda i, e: (0,) * len(shape))
    words = pltpu.VMEM((2, H, PEER_NKEYS, tq), F32)
    halves = pltpu.VMEM((2, H, PEER_NKEYS, tq), BF16)
    return pl.pallas_call(
        functools.partial(_peer_kernel, ib=ib, isub=isub),
        grid=(nt, ne),
        in_specs=[pl.BlockSpec((tq, D), lambda i, e: (i, 0)),
                  pl.BlockSpec((tq, D), lambda i, e: (jnp.minimum(i + 1, nt - 1), 0)),
                  full((1, D)), full((H, D, 2 * PEER_NKEYS)), full((2 * H, PEER_NKEYS, PEER_NKEYS)),
                  pl.BlockSpec((ec // 2, D), lambda i, e: (e, 0)),
                  pl.BlockSpec((D // 2, ec), lambda i, e: (0, e))],
        out_specs=pl.BlockSpec((tq, D), lambda i, e: (i, 0)),
        out_shape=jax.ShapeDtypeStruct((T, D), F32),
        scratch_shapes=[pltpu.VMEM((tq, D), BF16), pltpu.VMEM((2, D, tq), BF16),
                        words, words, halves, halves, pltpu.VMEM((D, tq), F32),
                        pltpu.VMEM((2, PEER_NKEYS, tq), F32)],
        compiler_params=_cparams(("arbitrary", "arbitrary")),
        name="peer_ffn",
    )(x2d, x2d, g.reshape(1, D), wq3_bf16, sk_bf16, u_pairs, vt_pairs)


def _block_diag(w):
    G, n, _ = w.shape
    eye = jnp.eye(G, dtype=w.dtype)
    return (eye[:, None, :, None] * w[:, :, None, :]).reshape(G * n, G * n)


def even_layer(x2d, B, S, norm, w_in, gate_up, gate_b, out_norm, w_out):
    D = x2d.shape[1]
    gg0 = EV_GG * LANES
    wp = jnp.concatenate([w_in[:, :1536], w_in[:, 1536 + GLA_RANK:], w_in[:, 1536:1536 + GLA_RANK],
                          jnp.zeros((D, EV_COLS - gg0 - GLA_RANK), w_in.dtype)], axis=1).astype(BF16)
    proj = norm_proj(x2d, norm, wp).reshape(B, S, EV_COLS)
    gup = jnp.concatenate([gate_up, jnp.zeros((LANES - GLA_RANK, gate_up.shape[1]), gate_up.dtype)], axis=0)
    a_out = gla_mixer(proj, gup.astype(BF16), gate_b, out_norm)
    b_out = sb_attention(proj)
    return out_proj(x2d, a_out.reshape(B * S, -1), b_out.reshape(B * S, -1), w_out.astype(BF16))


def odd_layer(x2d, B, S, positions, norm, w_in, q_norm, k_norm, conv_w, conv_b, wa, ba, wx, bx, lam, w_out):
    proj = norm_proj(x2d, norm, w_in.astype(BF16)).reshape(B, S, -1)
    half = HEAD_DIM // 2
    inv_freq = ROPE_THETA ** (-jnp.arange(0, HEAD_DIM, 2, dtype=F32) / HEAD_DIM)
    inv_tile = jnp.tile(inv_freq, LANES // half).reshape(1, LANES)
    seg = np.arange(OD_W) // HEAD_DIM
    blockdiag = jnp.asarray(seg[:, None] == seg[None, :], dtype=BF16)
    q_t, k_rot, v_t, kmean = moba_prep(proj, positions.reshape(B, S, 1), inv_tile,
                                       jnp.tile(q_norm, OD_W // HEAD_DIM).reshape(1, OD_W),
                                       jnp.tile(k_norm, OD_W // HEAD_DIM).reshape(1, OD_W), blockdiag)
    c_out = moba_attention(q_t, k_rot, v_t, kmean.reshape(B, S // MOBA_BLOCK, OD_W))
    d_out = rg_lru_mixer(proj, conv_w, conv_b, _block_diag(wa).astype(BF16), ba,
                         _block_diag(wx).astype(BF16), bx, lam)
    return out_proj(x2d, c_out.reshape(B * S, -1), d_out.reshape(B * S, -1), w_out.astype(BF16))


def kernel(x, mem, positions, ev_norm, ev_w_in, ev_gla_gate_up, ev_gla_gate_b, ev_gla_out_norm, ev_w_out, od_norm, od_w_in, od_q_norm, od_k_norm, od_conv_w, od_conv_b, od_gate_a_w, od_gate_a_b, od_gate_x_w, od_gate_x_b, od_lambda, od_w_out, xa_norm, xa_mem_norm, xa_wq, xa_wkv, xa_q_norm, xa_k_norm, xa_wo, ffn_norm, peer_wq, peer_subkeys, peer_u, peer_v):
    B, S, D = x.shape
    M = mem.shape[1]
    depth = xa_norm.shape[0]
    x2d = x.reshape(B * S, D)
    mem2d = mem.reshape(B * M, D)
    for l in range(depth):
        if l % 2 == 0:
            e = l // 2
            x2d = even_layer(x2d, B, S, ev_norm[e], ev_w_in[e], ev_gla_gate_up[e], ev_gla_gate_b[e],
                             ev_gla_out_norm[e], ev_w_out[e])
        else:
            o = l // 2
            x2d = odd_layer(x2d, B, S, positions, od_norm[o], od_w_in[o], od_q_norm[o], od_k_norm[o],
                            od_conv_w[o], od_conv_b[o], od_gate_a_w[o], od_gate_a_b[o],
                            od_gate_x_w[o], od_gate_x_b[o], od_lambda[o], od_w_out[o])
        kv = norm_proj(mem2d, xa_mem_norm[l], xa_wkv[l].astype(BF16))
        x2d = cross_attention(x2d.reshape(B, S, D), kv.reshape(B, M, -1), xa_norm[l],
                              xa_wq[l].astype(BF16), xa_q_norm[l], xa_k_norm[l],
                              xa_wo[l].astype(BF16)).reshape(B * S, D)
        H = PEER_HEADS
        wq3 = peer_wq[l].reshape(D, H, 2 * PEER_NKEYS).transpose(1, 0, 2).astype(BF16)
        sk = peer_subkeys[l].reshape(2 * H, PEER_NKEYS, -1).astype(BF16)
        x2d = peer_ffn(x2d, ffn_norm[l], wq3, sk, _pair_rows(peer_u, l),
                       _pair_rows(peer_v, l, transpose=True))
    return x2d.reshape(B, S, D)
```

```python
import functools

import jax
import jax.numpy as jnp
import numpy as np
from jax import lax
from jax.experimental import pallas as pl
from jax.experimental.pallas import tpu as pltpu

F32 = jnp.float32
BF16 = jnp.bfloat16

EPS = 1e-6
ROPE_THETA = 10000.0
LANES = 128
HEAD_DIM = 64
GLA_HEADS = 4
GLA_DV = 128
GLA_CHUNK = 64
GLA_TAU = 16.0
GLA_RANK = 16
MOBA_BLOCK = 256
MOBA_TOPK = 3
RG_C = 8.0
RG_CONV = 4
XA_HEADS = 4
XA_DH = 128
PEER_HEADS = 8
PEER_NKEYS = 128
PEER_TOPK = 16
NEG = -1e30
F32_LOG_TINY = -104.0
VMEM_LIMIT = 56 * 1024 * 1024


def _cparams(sem):
    return pltpu.CompilerParams(dimension_semantics=sem, vmem_limit_bytes=VMEM_LIMIT)


def _dot(a, b):
    return jnp.dot(a.astype(BF16), b.astype(BF16), preferred_element_type=F32)


def _dot_nt(a, b):
    return lax.dot_general(a.astype(BF16), b.astype(BF16), (((1,), (1,)), ((), ())),
                           preferred_element_type=F32)


def _dot_tn(a, b):
    return lax.dot_general(a.astype(BF16), b.astype(BF16), (((0,), (0,)), ((), ())),
                           preferred_element_type=F32)


def _split2(a):
    hi = a.astype(BF16)
    lo = (a - hi.astype(F32)).astype(BF16)
    return hi, lo


def _dot_exact_rhs(a, m_bf16):
    hi, lo = _split2(a)
    return (jnp.dot(hi, m_bf16, preferred_element_type=F32)
            + jnp.dot(lo, m_bf16, preferred_element_type=F32))


def _dot_exact_lhs(m_bf16, a):
    hi, lo = _split2(a)
    return (jnp.dot(m_bf16, hi, preferred_element_type=F32)
            + jnp.dot(m_bf16, lo, preferred_element_type=F32))


def _rms(x, g):
    return x * lax.rsqrt(jnp.mean(x * x, axis=-1, keepdims=True) + EPS) * g


def _softplus(z):
    return jnp.maximum(z, 0.0) + jnp.log1p(jnp.exp(-jnp.abs(z)))


def _head_mask(hh):
    lane = lax.broadcasted_iota(jnp.int32, (1, LANES), 1)
    return ((lane // HEAD_DIM) == hh).astype(F32)


def _norm_proj_kernel(x_ref, g_ref, w_ref, o_ref):
    xn = _rms(x_ref[...], g_ref[...])
    o_ref[...] = jnp.dot(xn.astype(BF16), w_ref[...], preferred_element_type=F32)


def norm_proj(x2d, g, w_bf16, tm=256):
    T, D = x2d.shape
    N = w_bf16.shape[1]
    return pl.pallas_call(
        _norm_proj_kernel,
        grid=(T // tm,),
        in_specs=[pl.BlockSpec((tm, D), lambda i: (i, 0)),
                  pl.BlockSpec((1, D), lambda i: (0, 0)),
                  pl.BlockSpec((D, N), lambda i: (0, 0))],
        out_specs=pl.BlockSpec((tm, N), lambda i: (i, 0)),
        out_shape=jax.ShapeDtypeStruct((T, N), F32),
        compiler_params=_cparams(("parallel",)),
        name="norm_proj",
    )(x2d, g.reshape(1, D), w_bf16)


EV_Q, EV_K, EV_V, EV_R, EV_SQ, EV_SK, EV_SV, EV_GG = 0, 2, 4, 8, 12, 16, 20, 24
EV_COLS = 25 * LANES


def _gla_kernel(q_ref, k_ref, v_ref, r_ref, gg_ref, gup_ref, gb_ref, onorm_ref, o_ref, state_ref,
                *, ts):
    C = GLA_CHUNK

    @pl.when(pl.program_id(1) == 0)
    def _():
        state_ref[...] = jnp.zeros_like(state_ref)

    row = lax.broadcasted_iota(jnp.int32, (C, C), 0)
    col = lax.broadcasted_iota(jnp.int32, (C, C), 1)
    causal = row >= col
    tri = causal.astype(BF16)
    masks = [_head_mask(0), _head_mask(1)]
    scale = HEAD_DIM ** -0.5
    onorm = onorm_ref[...]

    for ci in range(ts // C):
        sl = slice(ci * C, (ci + 1) * C)
        pre = _dot(gg_ref[0, sl, :], gup_ref[...]) + gb_ref[...]
        g = (jnp.minimum(pre, 0.0) - jnp.log1p(jnp.exp(-jnp.abs(pre)))) * (1.0 / GLA_TAU)
        b = _dot_exact_lhs(tri, g)
        bmid = b[C // 2 - 1:C // 2, :]
        blast = b[C - 1:C, :]
        q = q_ref[0, sl, :] * scale
        k = k_ref[0, sl, :]
        qd = q * jnp.exp(b - bmid)
        kd = k * jnp.exp(bmid - b)
        qe = q * jnp.exp(b)
        kdec = k * jnp.exp(blast - b)
        eb_last = jnp.exp(blast)
        for p in range(GLA_HEADS // 2):
            lanes = slice(p * LANES, (p + 1) * LANES)
            st = state_ref[p]
            new_st = st * eb_last[:, lanes]
            for hh in range(2):
                h = 2 * p + hh
                m = masks[hh]
                hl = slice(h * GLA_DV, (h + 1) * GLA_DV)
                a = _dot_nt(qd[:, lanes] * m, kd[:, lanes])
                a = jnp.where(causal, a, 0.0)
                v_h = v_ref[0, sl, hl]
                o = _dot(a, v_h) + _dot_nt(qe[:, lanes] * m, st)
                o = _rms(o, onorm)
                r_h = r_ref[0, sl, hl]
                o_ref[0, sl, hl] = o * (r_h * jax.nn.sigmoid(r_h))
                new_st = new_st + _dot_tn(v_h, kdec[:, lanes] * m)
            state_ref[p] = new_st


def gla_mixer(proj3, gate_up_pad, gate_b, out_norm, ts=512):
    B, S, _ = proj3.shape
    nh = GLA_HEADS
    w = nh * GLA_DV

    def col(blk_w, tile):
        idx = tile * LANES // blk_w
        return pl.BlockSpec((1, ts, blk_w), lambda b, c: (b, c, idx))

    return pl.pallas_call(
        functools.partial(_gla_kernel, ts=ts),
        grid=(B, S // ts),
        in_specs=[col(nh * HEAD_DIM, EV_Q), col(nh * HEAD_DIM, EV_K), col(w, EV_V), col(w, EV_R),
                  col(LANES, EV_GG),
                  pl.BlockSpec((LANES, nh * HEAD_DIM), lambda b, c: (0, 0)),
                  pl.BlockSpec((1, nh * HEAD_DIM), lambda b, c: (0, 0)),
                  pl.BlockSpec((1, GLA_DV), lambda b, c: (0, 0))],
        out_specs=pl.BlockSpec((1, ts, w), lambda b, c: (b, c, 0)),
        out_shape=jax.ShapeDtypeStruct((B, S, w), F32),
        scratch_shapes=[pltpu.VMEM((nh // 2, GLA_DV, LANES), F32)],
        compiler_params=_cparams(("parallel", "arbitrary")),
        name="gla",
    )(proj3, proj3, proj3, proj3, proj3, gate_up_pad, gate_b.reshape(1, -1), out_norm.reshape(1, -1))


def _sb_kernel(q_ref, k_ref, v_ref, o_ref, acc_ref, carry_ref, *, tq):
    i = pl.program_id(2)
    q = q_ref[0] * (HEAD_DIM ** -0.5)
    masks = [_head_mask(0), _head_mask(1)]
    qh = [(q * m).astype(BF16) for m in masks]
    row = lax.broadcasted_iota(jnp.int32, (tq, tq), 0)
    col = lax.broadcasted_iota(jnp.int32, (tq, tq), 1)
    upper = (row > col).astype(BF16)
    past = col < row
    acc_ref[...] = jnp.zeros_like(acc_ref)
    carry_ref[...] = jnp.zeros_like(carry_ref)

    def tile(j, diag):
        off = pl.multiple_of(j * tq, tq)
        k = k_ref[0, pl.ds(off, tq), :].astype(BF16)
        v = v_ref[0, pl.ds(off, tq), :].astype(BF16)
        for hh in range(2):
            z = lax.dot_general(qh[hh], k, (((1,), (1,)), ((), ())), preferred_element_type=F32)
            sp = _softplus(z)
            log_rem = jnp.where(past, -sp, 0.0) if diag else -sp
            carry = carry_ref[hh]
            after = _dot_exact_rhs(log_rem, upper) + carry
            w = jnp.exp((z - sp) + after)
            if diag:
                w = jnp.where(past, w, 0.0)
            acc_ref[hh] += jnp.dot(w.astype(BF16), v, preferred_element_type=F32)
            carry_ref[hh] = carry + jnp.sum(log_rem, axis=1, keepdims=True)

    def first_two_tiles():
        off = pl.multiple_of((i - 1) * tq, tq)
        k = k_ref[0, pl.ds(off, 2 * tq), :].astype(BF16)
        v = v_ref[0, pl.ds(off, 2 * tq), :].astype(BF16)
        for hh in range(2):
            z = lax.dot_general(qh[hh], k, (((1,), (1,)), ((), ())), preferred_element_type=F32)
            sp = _softplus(z)
            z_l, z_r, sp_l, sp_r = z[:, :tq], z[:, tq:], sp[:, :tq], sp[:, tq:]
            rem_r = jnp.where(past, -sp_r, 0.0)
            after_r = _dot_exact_rhs(rem_r, upper)
            tot_r = jnp.sum(rem_r, axis=1, keepdims=True)
            after_l = _dot_exact_rhs(-sp_l, upper) + tot_r
            w_r = jnp.where(past, jnp.exp((z_r - sp_r) + after_r), 0.0)
            w_l = jnp.exp((z_l - sp_l) + after_l)
            w = jnp.concatenate([w_l, w_r], axis=1).astype(BF16)
            acc_ref[hh] = jnp.dot(w, v, preferred_element_type=F32)
            carry_ref[hh] = tot_r - jnp.sum(sp_l, axis=1, keepdims=True)

    @pl.when(i == 0)
    def _():
        tile(i, True)

    @pl.when(i > 0)
    def _():
        first_two_tiles()

    def live(c):
        j, worst = c
        return jnp.logical_and(j >= 0, worst > F32_LOG_TINY)

    def body(c):
        j, _ = c
        tile(j, False)
        return j - 1, jnp.max(carry_ref[...])

    lax.while_loop(live, body, (i - 2, jnp.max(carry_ref[...])))
    o_ref[0] = acc_ref[0] * masks[0] + acc_ref[1] * masks[1]


def sb_attention(proj3, tq=256):
    B, S, _ = proj3.shape
    npairs = 4
    return pl.pallas_call(
        functools.partial(_sb_kernel, tq=tq),
        grid=(B, npairs, S // tq),
        in_specs=[pl.BlockSpec((1, tq, LANES), lambda b, p, i: (b, i, EV_SQ + p)),
                  pl.BlockSpec((1, S, LANES), lambda b, p, i: (b, 0, EV_SK + p)),
                  pl.BlockSpec((1, S, LANES), lambda b, p, i: (b, 0, EV_SV + p))],
        out_specs=pl.BlockSpec((1, tq, LANES), lambda b, p, i: (b, i, p)),
        out_shape=jax.ShapeDtypeStruct((B, S, npairs * LANES), F32),
        scratch_shapes=[pltpu.VMEM((2, tq, LANES), F32), pltpu.VMEM((2, tq, 1), F32)],
        compiler_params=_cparams(("parallel", "parallel", "arbitrary")),
        name="sb_attention",
    )(proj3, proj3, proj3)


OD_Q, OD_K, OD_V, OD_RX, OD_RG = 0, 4, 8, 12, 16
OD_W = 4 * LANES


def _moba_prep_kernel(q_ref, k_ref, v_ref, pos_ref, invf_ref, qn_ref, kn_ref, bd_ref,
                      qt_ref, ko_ref, vt_ref, km_ref):
    ang = pos_ref[0].astype(F32) * invf_ref[...]
    cos1, sin1 = jnp.cos(ang), jnp.sin(ang)
    lane = lax.broadcasted_iota(jnp.int32, (1, LANES), 1)
    first_half = (lane % HEAD_DIM) < (HEAD_DIM // 2)
    bd = bd_ref[...]

    def norm_rope(x, gain):
        ms = _dot_exact_rhs(x * x, bd) * (1.0 / HEAD_DIM)
        xn = x * lax.rsqrt(ms + EPS) * gain
        outs = []
        for t in range(OD_W // LANES):
            xb = xn[:, t * LANES:(t + 1) * LANES]
            up = pltpu.roll(xb, LANES - HEAD_DIM // 2, axis=1)
            dn = pltpu.roll(xb, HEAD_DIM // 2, axis=1)
            outs.append(xb * cos1 + jnp.where(first_half, -up, dn) * sin1)
        return jnp.concatenate(outs, axis=1)

    qt_ref[0] = norm_rope(q_ref[0], qn_ref[...]).T.astype(BF16)
    kr = norm_rope(k_ref[0], kn_ref[...])
    ko_ref[0] = kr.astype(BF16)
    km_ref[0, 0] = jnp.mean(kr, axis=0, keepdims=True)
    vt_ref[0] = v_ref[0].T.astype(BF16)


def moba_prep(proj3, pos3, inv_freq_tile, qn_tile, kn_tile, blockdiag):
    B, S, _ = proj3.shape
    tb = MOBA_BLOCK
    nb = S // tb
    full = lambda shape: pl.BlockSpec(shape, lambda b, i: (0,) * len(shape))
    return pl.pallas_call(
        _moba_prep_kernel,
        grid=(B, nb),
        in_specs=[pl.BlockSpec((1, tb, OD_W), lambda b, i: (b, i, OD_Q // 4)),
                  pl.BlockSpec((1, tb, OD_W), lambda b, i: (b, i, OD_K // 4)),
                  pl.BlockSpec((1, tb, OD_W), lambda b, i: (b, i, OD_V // 4)),
                  pl.BlockSpec((1, tb, 1), lambda b, i: (b, i, 0)),
                  full((1, LANES)), full((1, OD_W)), full((1, OD_W)), full((OD_W, OD_W))],
        out_specs=[pl.BlockSpec((1, OD_W, tb), lambda b, i: (b, 0, i)),
                   pl.BlockSpec((1, tb, OD_W), lambda b, i: (b, i, 0)),
                   pl.BlockSpec((1, OD_W, tb), lambda b, i: (b, 0, i)),
                   pl.BlockSpec((1, 1, 1, OD_W), lambda b, i: (b, i, 0, 0))],
        out_shape=[jax.ShapeDtypeStruct((B, OD_W, S), BF16),
                   jax.ShapeDtypeStruct((B, S, OD_W), BF16),
                   jax.ShapeDtypeStruct((B, OD_W, S), BF16),
                   jax.ShapeDtypeStruct((B, nb, 1, OD_W), F32)],
        compiler_params=_cparams(("parallel", "parallel")),
        name="moba_prep",
    )(proj3, proj3, proj3, pos3, inv_freq_tile, qn_tile, kn_tile, blockdiag)


def _moba_kernel(qt_ref, k_ref, vt_ref, km_ref, o_ref, acc_ref, m_ref, sel_ref):
    tb = MOBA_BLOCK
    nbp = km_ref.shape[1]
    i = pl.program_id(2)
    qt = qt_ref[0]
    km = km_ref[0].astype(BF16)
    chan = lax.broadcasted_iota(jnp.int32, (LANES, 1), 0)
    in_head = [(chan // HEAD_DIM) == hh for hh in range(2)]
    blk = lax.broadcasted_iota(jnp.int32, (nbp, tb), 0).astype(F32)
    own = i.astype(F32)
    key = lax.broadcasted_iota(jnp.int32, (tb, tb), 0)
    qry = lax.broadcasted_iota(jnp.int32, (tb, tb), 1)
    visible = key <= qry
    off_own = pl.multiple_of(i * tb, tb)
    k_own = k_ref[0, pl.ds(off_own, tb), :]
    vt_own = vt_ref[0, :, pl.ds(off_own, tb)]
    zero = jnp.zeros_like(qt)
    one = jnp.ones_like(qt)
    qs = []
    for hh in range(2):
        qm = jnp.where(in_head[hh], qt, zero)
        gate = jnp.dot(km, qm, preferred_element_type=F32)
        g = jnp.where(blk < own, gate, -jnp.inf)
        sel = jnp.zeros((nbp, tb), F32)
        for r in range(MOBA_TOPK):
            mx = jnp.max(g, axis=0, keepdims=True)
            idx = jnp.min(jnp.where(g == mx, blk, float(nbp)), axis=0, keepdims=True)
            hit = blk == idx
            keep = jnp.where(i > r, 1.0, 0.0)
            sel = sel + jnp.where(hit, keep, 0.0)
            g = jnp.where(hit, -jnp.inf, g)
        sel_ref[hh] = sel
        qb = qm * (HEAD_DIM ** -0.5)
        qs.append(qb)
        s = jnp.dot(k_own, qb, preferred_element_type=F32)
        s = jnp.where(visible, s, NEG)
        m0 = jnp.max(s, axis=0, keepdims=True)
        p = jnp.exp(s - m0)
        m_ref[hh] = m0
        acc_ref[hh] = jnp.dot(jnp.where(in_head[hh], vt_own, one), p.astype(BF16),
                              preferred_element_type=F32)

    def past_blocks(n, nblk):
        off = pl.multiple_of(n * tb, tb)
        k = k_ref[0, pl.ds(off, nblk * tb), :]
        vt = vt_ref[0, :, pl.ds(off, nblk * tb)]
        scores = [jnp.dot(k, qs[hh], preferred_element_type=F32) for hh in range(2)]
        probs, alphas = [], []
        for hh in range(2):
            parts = []
            for j in range(nblk):
                chosen = sel_ref[hh, pl.ds(n + j, 1), :]
                parts.append(jnp.where(chosen > 0.0, scores[hh][j * tb:(j + 1) * tb, :], NEG))
            m_old = m_ref[hh]
            m_new = m_old
            for part in parts:
                m_new = jnp.maximum(m_new, jnp.max(part, axis=0, keepdims=True))
            alphas.append(jnp.exp(m_old - m_new))
            probs.append(jnp.concatenate([jnp.exp(part - m_new).astype(BF16) for part in parts], axis=0))
            m_ref[hh] = m_new
        ones_v = jnp.ones_like(vt)
        pv = [jnp.dot(jnp.where(in_head[hh], vt, ones_v), probs[hh], preferred_element_type=F32)
              for hh in range(2)]
        for hh in range(2):
            acc_ref[hh] = acc_ref[hh] * alphas[hh] + pv[hh]

    def four_blocks(t, c):
        past_blocks(4 * t, 4)
        return c

    lax.fori_loop(0, i // 4, four_blocks, 0)
    rem = i % 4

    @pl.when(rem >= 2)
    def _():
        past_blocks(i - rem, 2)

    @pl.when(rem % 2 == 1)
    def _():
        past_blocks(i - 1, 1)

    a0, a1 = acc_ref[0], acc_ref[1]
    den0 = a0[HEAD_DIM:HEAD_DIM + 1, :]
    den1 = a1[0:1, :]
    o_ref[0] = jnp.where(in_head[0], a0 / den0, a1 / den1).T


def moba_attention(q_t, k_rot, v_t, kmean):
    B, S, _ = k_rot.shape
    tb = MOBA_BLOCK
    nb = S // tb
    npairs = OD_W // LANES
    nbp = -(-nb // BF16_ROWS) * BF16_ROWS
    kmean = jnp.pad(kmean, ((0, 0), (0, nbp - nb), (0, 0)))
    return pl.pallas_call(
        _moba_kernel,
        grid=(B, npairs, nb),
        in_specs=[pl.BlockSpec((1, LANES, tb), lambda b, p, i: (b, p, i)),
                  pl.BlockSpec((1, S, LANES), lambda b, p, i: (b, 0, p)),
                  pl.BlockSpec((1, LANES, S), lambda b, p, i: (b, p, 0)),
                  pl.BlockSpec((1, nbp, LANES), lambda b, p, i: (b, 0, p))],
        out_specs=pl.BlockSpec((1, tb, LANES), lambda b, p, i: (b, i, p)),
        out_shape=jax.ShapeDtypeStruct((B, S, OD_W), F32),
        scratch_shapes=[pltpu.VMEM((2, LANES, tb), F32), pltpu.VMEM((2, 1, tb), F32),
                        pltpu.VMEM((2, nbp, tb), F32)],
        compiler_params=_cparams(("parallel", "parallel", "arbitrary")),
        name="moba_attention",
    )(q_t, k_rot, v_t, kmean)


def _rglru_kernel(x_ref, gate_ref, cw_ref, cb_ref, wa_ref, ba_ref, wx_ref, bx_ref, lam_ref, o_ref,
                  buf_ref, xprev_ref, hprev_ref, *, ts):
    @pl.when(pl.program_id(1) == 0)
    def _():
        xprev_ref[...] = jnp.zeros_like(xprev_ref)
        hprev_ref[...] = jnp.zeros_like(hprev_ref)

    x = x_ref[0]
    buf_ref[0:8, :] = xprev_ref[...]
    buf_ref[8:8 + ts, :] = x
    xprev_ref[...] = x[ts - 8:ts, :]
    xc = cb_ref[...]
    for kk in range(RG_CONV):
        start = 8 - (RG_CONV - 1) + kk
        xc = xc + cw_ref[kk:kk + 1, :] * buf_ref[start:start + ts, :]
    r = jax.nn.sigmoid(_dot(xc, wa_ref[...]) + ba_ref[...])
    ig = jax.nn.sigmoid(_dot(xc, wx_ref[...]) + bx_ref[...])
    log_a = (-RG_C) * r * _softplus(-lam_ref[...])
    a = jnp.exp(log_a)
    u = jnp.sqrt(-jnp.tanh(log_a) * (a * a + 1.0)) * (ig * xc)
    t_idx = lax.broadcasted_iota(jnp.int32, (ts, 1), 0)
    d = 1
    while d < ts:
        valid = t_idx >= d
        a_sh = pltpu.roll(a, d, axis=0)
        u_sh = pltpu.roll(u, d, axis=0)
        u = jnp.where(valid, a * u_sh, 0.0) + u
        a = jnp.where(valid, a * a_sh, a)
        d *= 2
    h = a * hprev_ref[...] + u
    hprev_ref[...] = h[ts - 1:ts, :]
    o_ref[0] = h * jax.nn.gelu(gate_ref[0], approximate=True)


def rg_lru_mixer(proj3, conv_w, conv_b, wa_bd, ba, wx_bd, bx, lam, ts=256):
    B, S, _ = proj3.shape
    W = conv_w.shape[1]
    full = lambda shape: pl.BlockSpec(shape, lambda b, t: (0,) * len(shape))
    row = lambda v: v.reshape(1, W)
    return pl.pallas_call(
        functools.partial(_rglru_kernel, ts=ts),
        grid=(B, S // ts),
        in_specs=[pl.BlockSpec((1, ts, W), lambda b, t: (b, t, OD_RX // 4)),
                  pl.BlockSpec((1, ts, W), lambda b, t: (b, t, OD_RG // 4)),
                  full((RG_CONV, W)), full((1, W)), full((W, W)), full((1, W)), full((W, W)),
                  full((1, W)), full((1, W))],
        out_specs=pl.BlockSpec((1, ts, W), lambda b, t: (b, t, 0)),
        out_shape=jax.ShapeDtypeStruct((B, S, W), F32),
        scratch_shapes=[pltpu.VMEM((ts + 8, W), F32), pltpu.VMEM((8, W), F32), pltpu.VMEM((1, W), F32)],
        compiler_params=_cparams(("parallel", "arbitrary")),
        name="rg_lru",
    )(proj3, proj3, conv_w, row(conv_b), wa_bd, row(ba), wx_bd, row(bx), row(lam))


def _xattn_kernel(x_ref, a_ref, b_ref, wmix_ref, kv_ref, g_ref, wq_ref, qn_ref, kn_ref, wo_ref, o_ref):
    ka = a_ref.shape[2]
    x = x_ref[0] + _dot(a_ref[0], wmix_ref[0:ka, :]) + _dot(b_ref[0], wmix_ref[ka:, :])
    q = _dot(_rms(x, g_ref[...]), wq_ref[...])
    kv = kv_ref[0]
    hw = XA_HEADS * XA_DH
    outs = []
    for h in range(XA_HEADS):
        hl = slice(h * XA_DH, (h + 1) * XA_DH)
        qh = _rms(q[:, hl], qn_ref[...])
        kh = _rms(kv[:, hl], kn_ref[...])
        vh = kv[:, hw + h * XA_DH: hw + (h + 1) * XA_DH]
        s = _dot_nt(qh, kh) * (XA_DH ** -0.5)
        s = s - jnp.max(s, axis=1, keepdims=True)
        p = jnp.exp(s)
        p = p / jnp.sum(p, axis=1, keepdims=True)
        outs.append(_dot(p, vh))
    o_ref[0] = x + _dot(jnp.concatenate(outs, axis=1), wo_ref[...])


def cross_attention(x3, a3, b3, wmix_bf16, kv3, g, wq_bf16, qn, kn, wo_bf16, tq=256):
    B, S, D = x3.shape
    M = kv3.shape[1]
    ka, kb = a3.shape[2], b3.shape[2]
    hw = XA_HEADS * XA_DH
    full = lambda shape: pl.BlockSpec(shape, lambda b, i: (0,) * len(shape))
    return pl.pallas_call(
        _xattn_kernel,
        grid=(B, S // tq),
        in_specs=[pl.BlockSpec((1, tq, D), lambda b, i: (b, i, 0)),
                  pl.BlockSpec((1, tq, ka), lambda b, i: (b, i, 0)),
                  pl.BlockSpec((1, tq, kb), lambda b, i: (b, i, 0)),
                  full((ka + kb, D)),
                  pl.BlockSpec((1, M, 2 * hw), lambda b, i: (b, 0, 0)),
                  full((1, D)), full((D, hw)), full((1, XA_DH)), full((1, XA_DH)), full((hw, D))],
        out_specs=pl.BlockSpec((1, tq, D), lambda b, i: (b, i, 0)),
        out_shape=jax.ShapeDtypeStruct((B, S, D), F32),
        compiler_params=_cparams(("parallel", "parallel")),
        name="cross_attention",
    )(x3, a3, b3, wmix_bf16, kv3, g.reshape(1, D), wq_bf16, qn.reshape(1, -1), kn.reshape(1, -1), wo_bf16)


BF16_ROWS = 16


def _routing_stages(s1, s2, sub, finish, nstage):
    K = PEER_TOPK
    st = {"a1": s1, "a2": s2, "v1": [], "v2": [], "rank2": jnp.full(s2.shape, float(K), F32)}

    def extract(src, dst, rounds, want_rank):
        def run(nil):
            arr = st[src]
            gone = nil - jnp.inf
            for r in rounds:
                m = jnp.max(arr, axis=0, keepdims=True)
                st[dst].append(m)
                hit = arr == m
                if want_rank:
                    st["rank2"] = jnp.where(hit, float(r), st["rank2"])
                arr = jnp.where(hit, gone, arr)
            st[src] = arr
            st["last"] = m
        return run

    def candidates(nil):
        v1, v2 = st["v1"], st["v2"]
        v1m = jnp.concatenate(v1, axis=0)
        v2m = jnp.concatenate(v2, axis=0)
        groups = [(v1[0] + nil) + v2m]
        for a in range(1, 8):
            groups.append(jnp.where(sub < K // (a + 1), v1[a] + v2m[0:8], -jnp.inf))
        groups.append(v1m[8:16] + v2[0])
        st["cand"] = jnp.concatenate(groups, axis=0)
        st["top"] = []
        st["last"] = v2[0]

    def threshold(nil):
        top = st["top"]
        st["tau"] = top[K - 1]
        z = nil
        for tv in top:
            z = z + jnp.exp(tv - top[0])
        st["z"] = z
        st["last"] = z

    def counts(nil):
        v1, v2, tau = st["v1"], st["v2"], st["tau"]
        count = jnp.zeros(s1.shape, F32)
        for bb in range(K // 2):
            count = count + jnp.where(s1 + v2[bb] >= tau, 1.0, 0.0)
        best = nil
        for bb in range(K // 2, K):
            best = best + jnp.where(v1[0] + v2[bb] >= tau, 1.0, 0.0)
        count = count + jnp.where(s1 == v1[0], best, 0.0)
        finish(count, jnp.exp(s1 - v1[0]) / st["z"], st["rank2"].astype(BF16),
               jnp.exp(s2 - v2[0]).astype(BF16))
        st["last"] = best

    quarter = [range(q * 4, q * 4 + 4) for q in range(4)]
    work = ([extract("a1", "v1", r, False) for r in quarter]
            + [extract("a2", "v2", r, True) for r in quarter]
            + [candidates] + [extract("cand", "top", r, False) for r in quarter]
            + [threshold, counts])
    bounds = [round(j * len(work) / nstage) for j in range(nstage + 1)]

    def stage(j):
        def run(nil):
            for piece in work[bounds[j]:bounds[j + 1]]:
                piece(nil)
            return st["last"]
        return run

    return [stage(j) for j in range(nstage)]


def _peer_kernel(x_ref, xnext_ref, g_ref, wq_ref, sk_ref, u_ref, vt_ref, o_ref,
                 xn_ref, xnt_ref, c_ref, a_ref, r2_ref, e2_ref, acc_ref, s_ref, *, ib, isub):
    t = pl.program_id(0)
    e = pl.program_id(1)
    nk = PEER_NKEYS
    tq = x_ref.shape[0]
    slot = t % 2
    nslot = 1 - slot
    sub = lax.broadcasted_iota(jnp.int32, (8, 1), 0)
    nsub = ib // isub

    def prepare(src_ref, s):
        xn32 = _rms(src_ref[...], g_ref[...])
        xn_ref[...] = xn32.astype(BF16)
        xnt_ref[s] = xn32.T.astype(BF16)

    def route_scores(h):
        q = jnp.dot(xn_ref[...], wq_ref[h], preferred_element_type=F32)
        return _dot_nt(sk_ref[2 * h], q[:, :nk]), _dot_nt(sk_ref[2 * h + 1], q[:, nk:])

    def table_writer(h, s):
        def finish(count, amp, rank2, e2):
            c_ref[s, h] = count
            a_ref[s, h] = amp
            r2_ref[s, h] = rank2
            e2_ref[s, h] = e2
        return finish

    @pl.when(jnp.logical_and(t == 0, e == 0))
    def _():
        prepare(x_ref, 0)

        def one(h, c):
            s1, s2 = route_scores(h)
            for run in _routing_stages(s1, s2, sub, table_writer(h, 0), 1):
                run(jnp.zeros((1, tq), F32))
            return c

        lax.fori_loop(0, PEER_HEADS, one, 0)

    @pl.when(e == 0)
    def _():
        acc_ref[...] = jnp.zeros_like(acc_ref)
        prepare(xnext_ref, nslot)
        s_ref[0], s_ref[1] = route_scores(0)

    ngrp = nk // BF16_ROWS
    xnt = xnt_ref[slot]
    first = pl.multiple_of(e * ib, ib)
    step_c = [c_ref[slot, h, pl.ds(first, ib), :] for h in range(PEER_HEADS)]
    step_a = [a_ref[slot, h, pl.ds(first, ib), :] for h in range(PEER_HEADS)]

    def expert_matmul(sc):
        lo = sc * isub * nk
        u_sub = pltpu.bitcast(u_ref[lo // 2:(lo + isub * nk) // 2, :], BF16)
        return jnp.dot(u_sub, xnt, preferred_element_type=F32)

    stages = _routing_stages(s_ref[0], s_ref[1], sub, table_writer(e, nslot), nsub)
    hid_next = expert_matmul(0)
    ahead_s1, ahead_s2 = route_scores(jnp.minimum(e + 1, PEER_HEADS - 1))

    total = None
    after = jnp.zeros((1, tq), F32)
    for sc in range(nsub):
        zero = jnp.broadcast_to(stages[sc](after) * 0.0, (BF16_ROWS, tq)).astype(BF16)
        lo = sc * isub * nk
        hid_sub = hid_next
        if sc + 1 < nsub:
            hid_next = expert_matmul(sc + 1)
        pieces = []
        for ii in range(isub):
            il = sc * isub + ii
            hid = hid_sub[ii * nk:(ii + 1) * nk, :]
            act = (0.5 * hid * (1.0 + lax.erf(hid * np.float32(np.sqrt(0.5))))).astype(BF16)
            w = [zero] * ngrp
            for h in range(PEER_HEADS):
                cnt = jnp.broadcast_to(step_c[h][il:il + 1, :], (BF16_ROWS, tq)).astype(BF16)
                amp = jnp.broadcast_to(step_a[h][il:il + 1, :], (BF16_ROWS, tq)).astype(BF16)
                for gi in range(ngrp):
                    rows = slice(gi * BF16_ROWS, (gi + 1) * BF16_ROWS)
                    w[gi] = w[gi] + jnp.where(r2_ref[slot, h, rows, :] < cnt,
                                              e2_ref[slot, h, rows, :] * amp, zero)
            pieces += [w[gi] * act[gi * BF16_ROWS:(gi + 1) * BF16_ROWS, :] for gi in range(ngrp)]
        after = pieces[-1][0:1, :].astype(F32) * 0.0
        vt_sub = pltpu.bitcast(vt_ref[:, lo:lo + isub * nk], BF16)
        part = jnp.dot(vt_sub, jnp.concatenate(pieces, axis=0), preferred_element_type=F32)
        total = part if total is None else total + part
    acc_ref[...] += total
    s_ref[0] = ahead_s1
    s_ref[1] = ahead_s2

    @pl.when(e == pl.num_programs(1) - 1)
    def _():
        o_ref[...] = x_ref[...] + acc_ref[...].T


def _pair_rows_kernel(x_ref, o_ref, *, transpose):
    x = x_ref[0]
    if transpose:
        x = x.T
    o_ref[...] = pltpu.bitcast(x.astype(BF16), jnp.uint32)


def _pair_rows(w3, layer, transpose=False, tr=512):
    _, R, C = w3.shape
    in_spec = pl.BlockSpec((1, tr, C), lambda i: (layer, i, 0))
    if transpose:
        out_spec = pl.BlockSpec((C // 2, tr), lambda i: (0, i))
        out_shape = jax.ShapeDtypeStruct((C // 2, R), jnp.uint32)
    else:
        out_spec = pl.BlockSpec((tr // 2, C), lambda i: (i, 0))
        out_shape = jax.ShapeDtypeStruct((R // 2, C), jnp.uint32)
    return pl.pallas_call(
        functools.partial(_pair_rows_kernel, transpose=transpose),
        grid=(R // tr,), in_specs=[in_spec], out_specs=out_spec, out_shape=out_shape,
        compiler_params=_cparams(("parallel",)),
        name="pair_rows",
    )(w3)


def peer_ffn(x2d, g, wq3_bf16, sk_bf16, u_pairs, vt_pairs, tq=256, isub=2):
    T, D = x2d.shape
    H = PEER_HEADS
    ne = H
    ib = PEER_NKEYS // ne
    ec = ib * PEER_NKEYS
    nt = T // tq
    full = lambda shape: pl.BlockSpec(shape, lambda i, e: (0,) * len(shape))
    words = pltpu.VMEM((2, H, PEER_NKEYS, tq), F32)
    halves = pltpu.VMEM((2, H, PEER_NKEYS, tq), BF16)
    return pl.pallas_call(
        functools.partial(_peer_kernel, ib=ib, isub=isub),
        grid=(nt, ne),
        in_specs=[pl.BlockSpec((tq, D), lambda i, e: (i, 0)),
                  pl.BlockSpec((tq, D), lambda i, e: (jnp.minimum(i + 1, nt - 1), 0)),
                  full((1, D)), full((H, D, 2 * PEER_NKEYS)), full((2 * H, PEER_NKEYS, PEER_NKEYS)),
                  pl.BlockSpec((ec // 2, D), lambda i, e: (e, 0)),
                  pl.BlockSpec((D // 2, ec), lambda i, e: (0, e))],
        out_specs=pl.BlockSpec((tq, D), lambda i, e: (i, 0)),
        out_shape=jax.ShapeDtypeStruct((T, D), F32),
        scratch_shapes=[pltpu.VMEM((tq, D), BF16), pltpu.VMEM((2, D, tq), BF16),
                        words, words, halves, halves, pltpu.VMEM((D, tq), F32),
                        pltpu.VMEM((2, PEER_NKEYS, tq), F32)],
        compiler_params=_cparams(("arbitrary", "arbitrary")),
        name="peer_ffn",
    )(x2d, x2d, g.reshape(1, D), wq3_bf16, sk_bf16, u_pairs, vt_pairs)


def _block_diag(w):
    G, n, _ = w.shape
    eye = jnp.eye(G, dtype=w.dtype)
    return (eye[:, None, :, None] * w[:, :, None, :]).reshape(G * n, G * n)


def even_layer(x2d, B, S, norm, w_in, gate_up, gate_b, out_norm, w_out):
    D = x2d.shape[1]
    gg0 = EV_GG * LANES
    wp = jnp.concatenate([w_in[:, :1536], w_in[:, 1536 + GLA_RANK:], w_in[:, 1536:1536 + GLA_RANK],
                          jnp.zeros((D, EV_COLS - gg0 - GLA_RANK), w_in.dtype)], axis=1).astype(BF16)
    proj = norm_proj(x2d, norm, wp).reshape(B, S, EV_COLS)
    gup = jnp.concatenate([gate_up, jnp.zeros((LANES - GLA_RANK, gate_up.shape[1]), gate_up.dtype)], axis=0)
    a_out = gla_mixer(proj, gup.astype(BF16), gate_b, out_norm)
    b_out = sb_attention(proj)
    return a_out, b_out, w_out.astype(BF16)


def odd_layer(x2d, B, S, positions, norm, w_in, q_norm, k_norm, conv_w, conv_b, wa, ba, wx, bx, lam, w_out):
    proj = norm_proj(x2d, norm, w_in.astype(BF16)).reshape(B, S, -1)
    half = HEAD_DIM // 2
    inv_freq = ROPE_THETA ** (-jnp.arange(0, HEAD_DIM, 2, dtype=F32) / HEAD_DIM)
    inv_tile = jnp.tile(inv_freq, LANES // half).reshape(1, LANES)
    seg = np.arange(OD_W) // HEAD_DIM
    blockdiag = jnp.asarray(seg[:, None] == seg[None, :], dtype=BF16)
    q_t, k_rot, v_t, kmean = moba_prep(proj, positions.reshape(B, S, 1), inv_tile,
                                       jnp.tile(q_norm, OD_W // HEAD_DIM).reshape(1, OD_W),
                                       jnp.tile(k_norm, OD_W // HEAD_DIM).reshape(1, OD_W), blockdiag)
    c_out = moba_attention(q_t, k_rot, v_t, kmean.reshape(B, S // MOBA_BLOCK, OD_W))
    d_out = rg_lru_mixer(proj, conv_w, conv_b, _block_diag(wa).astype(BF16), ba,
                         _block_diag(wx).astype(BF16), bx, lam)
    return c_out, d_out, w_out.astype(BF16)


def kernel(x, mem, positions, ev_norm, ev_w_in, ev_gla_gate_up, ev_gla_gate_b, ev_gla_out_norm, ev_w_out, od_norm, od_w_in, od_q_norm, od_k_norm, od_conv_w, od_conv_b, od_gate_a_w, od_gate_a_b, od_gate_x_w, od_gate_x_b, od_lambda, od_w_out, xa_norm, xa_mem_norm, xa_wq, xa_wkv, xa_q_norm, xa_k_norm, xa_wo, ffn_norm, peer_wq, peer_subkeys, peer_u, peer_v):
    B, S, D = x.shape
    M = mem.shape[1]
    depth = xa_norm.shape[0]
    x2d = x.reshape(B * S, D)
    mem2d = mem.reshape(B * M, D)
    for l in range(depth):
        if l % 2 == 0:
            e = l // 2
            mix = even_layer(x2d, B, S, ev_norm[e], ev_w_in[e], ev_gla_gate_up[e], ev_gla_gate_b[e],
                             ev_gla_out_norm[e], ev_w_out[e])
        else:
            o = l // 2
            mix = odd_layer(x2d, B, S, positions, od_norm[o], od_w_in[o], od_q_norm[o], od_k_norm[o],
                            od_conv_w[o], od_conv_b[o], od_gate_a_w[o], od_gate_a_b[o],
                            od_gate_x_w[o], od_gate_x_b[o], od_lambda[o], od_w_out[o])
        kv = norm_proj(mem2d, xa_mem_norm[l], xa_wkv[l].astype(BF16))
        x2d = cross_attention(x2d.reshape(B, S, D), *mix, kv.reshape(B, M, -1), xa_norm[l],
                              xa_wq[l].astype(BF16), xa_q_norm[l], xa_k_norm[l],
                              xa_wo[l].astype(BF16)).reshape(B * S, D)
        H = PEER_HEADS
        wq3 = peer_wq[l].reshape(D, H, 2 * PEER_NKEYS).transpose(1, 0, 2).astype(BF16)
        sk = peer_subkeys[l].reshape(2 * H, PEER_NKEYS, -1).astype(BF16)
        x2d = peer_ffn(x2d, ffn_norm[l], wq3, sk, _pair_rows(peer_u, l),
                       _pair_rows(peer_v, l, transpose=True))
    return x2d.reshape(B, S, D)
```

```python
import functools

import jax
import jax.numpy as jnp
import numpy as np
from jax import lax
from jax.experimental import pallas as pl
from jax.experimental.pallas import tpu as pltpu

F32 = jnp.float32
BF16 = jnp.bfloat16

EPS = 1e-6
ROPE_THETA = 10000.0
LANES = 128
HEAD_DIM = 64
GLA_HEADS = 4
GLA_DV = 128
GLA_CHUNK = 64
GLA_TAU = 16.0
GLA_RANK = 16
MOBA_BLOCK = 256
MOBA_TOPK = 3
RG_C = 8.0
RG_CONV = 4
XA_HEADS = 4
XA_DH = 128
PEER_HEADS = 8
PEER_NKEYS = 128
PEER_TOPK = 16
NEG = -1e30
F32_LOG_TINY = -104.0
VMEM_LIMIT = 56 * 1024 * 1024


def _cparams(sem):
    return pltpu.CompilerParams(dimension_semantics=sem, vmem_limit_bytes=VMEM_LIMIT)


def _dot(a, b):
    return jnp.dot(a.astype(BF16), b.astype(BF16), preferred_element_type=F32)


def _dot_nt(a, b):
    return lax.dot_general(a.astype(BF16), b.astype(BF16), (((1,), (1,)), ((), ())),
                           preferred_element_type=F32)


def _dot_tn(a, b):
    return lax.dot_general(a.astype(BF16), b.astype(BF16), (((0,), (0,)), ((), ())),
                           preferred_element_type=F32)


def _split2(a):
    hi = a.astype(BF16)
    lo = (a - hi.astype(F32)).astype(BF16)
    return hi, lo


def _dot_exact_rhs(a, m_bf16):
    hi, lo = _split2(a)
    return (jnp.dot(hi, m_bf16, preferred_element_type=F32)
            + jnp.dot(lo, m_bf16, preferred_element_type=F32))


def _dot_exact_lhs(m_bf16, a):
    hi, lo = _split2(a)
    return (jnp.dot(m_bf16, hi, preferred_element_type=F32)
            + jnp.dot(m_bf16, lo, preferred_element_type=F32))


def _rms(x, g):
    return x * lax.rsqrt(jnp.mean(x * x, axis=-1, keepdims=True) + EPS) * g


def _softplus(z):
    return jnp.maximum(z, 0.0) + jnp.log1p(jnp.exp(-jnp.abs(z)))


def _head_mask(hh):
    lane = lax.broadcasted_iota(jnp.int32, (1, LANES), 1)
    return ((lane // HEAD_DIM) == hh).astype(F32)


def _norm_proj_kernel(x_ref, g_ref, w_ref, o_ref):
    xn = _rms(x_ref[...], g_ref[...])
    o_ref[...] = jnp.dot(xn.astype(BF16), w_ref[...], preferred_element_type=F32)


def norm_proj(x2d, g, w_bf16, tm=256):
    T, D = x2d.shape
    N = w_bf16.shape[1]
    return pl.pallas_call(
        _norm_proj_kernel,
        grid=(T // tm,),
        in_specs=[pl.BlockSpec((tm, D), lambda i: (i, 0)),
                  pl.BlockSpec((1, D), lambda i: (0, 0)),
                  pl.BlockSpec((D, N), lambda i: (0, 0))],
        out_specs=pl.BlockSpec((tm, N), lambda i: (i, 0)),
        out_shape=jax.ShapeDtypeStruct((T, N), F32),
        compiler_params=_cparams(("parallel",)),
        name="norm_proj",
    )(x2d, g.reshape(1, D), w_bf16)


EV_Q, EV_K, EV_V, EV_R, EV_SQ, EV_SK, EV_SV, EV_GG = 0, 2, 4, 8, 12, 16, 20, 24
EV_COLS = 25 * LANES


def _gla_kernel(q_ref, k_ref, v_ref, r_ref, gg_ref, gup_ref, gb_ref, onorm_ref, o_ref, state_ref,
                *, ts):
    C = GLA_CHUNK

    @pl.when(pl.program_id(1) == 0)
    def _():
        state_ref[...] = jnp.zeros_like(state_ref)

    row = lax.broadcasted_iota(jnp.int32, (C, C), 0)
    col = lax.broadcasted_iota(jnp.int32, (C, C), 1)
    causal = row >= col
    tri = causal.astype(BF16)
    masks = [_head_mask(0), _head_mask(1)]
    scale = HEAD_DIM ** -0.5
    onorm = onorm_ref[...]

    for ci in range(ts // C):
        sl = slice(ci * C, (ci + 1) * C)
        pre = _dot(gg_ref[0, sl, :], gup_ref[...]) + gb_ref[...]
        g = (jnp.minimum(pre, 0.0) - jnp.log1p(jnp.exp(-jnp.abs(pre)))) * (1.0 / GLA_TAU)
        b = _dot_exact_lhs(tri, g)
        bmid = b[C // 2 - 1:C // 2, :]
        blast = b[C - 1:C, :]
        q = q_ref[0, sl, :] * scale
        k = k_ref[0, sl, :]
        qd = q * jnp.exp(b - bmid)
        kd = k * jnp.exp(bmid - b)
        qe = q * jnp.exp(b)
        kdec = k * jnp.exp(blast - b)
        eb_last = jnp.exp(blast)
        for p in range(GLA_HEADS // 2):
            lanes = slice(p * LANES, (p + 1) * LANES)
            st = state_ref[p]
            new_st = st * eb_last[:, lanes]
            for hh in range(2):
                h = 2 * p + hh
                m = masks[hh]
                hl = slice(h * GLA_DV, (h + 1) * GLA_DV)
                a = _dot_nt(qd[:, lanes] * m, kd[:, lanes])
                a = jnp.where(causal, a, 0.0)
                v_h = v_ref[0, sl, hl]
                o = _dot(a, v_h) + _dot_nt(qe[:, lanes] * m, st)
                o = _rms(o, onorm)
                r_h = r_ref[0, sl, hl]
                o_ref[0, sl, hl] = o * (r_h * jax.nn.sigmoid(r_h))
                new_st = new_st + _dot_tn(v_h, kdec[:, lanes] * m)
            state_ref[p] = new_st


def gla_mixer(proj3, gate_up_pad, gate_b, out_norm, ts=512):
    B, S, _ = proj3.shape
    nh = GLA_HEADS
    w = nh * GLA_DV

    def col(blk_w, tile):
        idx = tile * LANES // blk_w
        return pl.BlockSpec((1, ts, blk_w), lambda b, c: (b, c, idx))

    return pl.pallas_call(
        functools.partial(_gla_kernel, ts=ts),
        grid=(B, S // ts),
        in_specs=[col(nh * HEAD_DIM, EV_Q), col(nh * HEAD_DIM, EV_K), col(w, EV_V), col(w, EV_R),
                  col(LANES, EV_GG),
                  pl.BlockSpec((LANES, nh * HEAD_DIM), lambda b, c: (0, 0)),
                  pl.BlockSpec((1, nh * HEAD_DIM), lambda b, c: (0, 0)),
                  pl.BlockSpec((1, GLA_DV), lambda b, c: (0, 0))],
        out_specs=pl.BlockSpec((1, ts, w), lambda b, c: (b, c, 0)),
        out_shape=jax.ShapeDtypeStruct((B, S, w), F32),
        scratch_shapes=[pltpu.VMEM((nh // 2, GLA_DV, LANES), F32)],
        compiler_params=_cparams(("parallel", "arbitrary")),
        name="gla",
    )(proj3, proj3, proj3, proj3, proj3, gate_up_pad, gate_b.reshape(1, -1), out_norm.reshape(1, -1))


def _sb_kernel(q_ref, k_ref, v_ref, o_ref, acc_ref, carry_ref, *, tq):
    i = pl.program_id(2)
    q = q_ref[0] * (HEAD_DIM ** -0.5)
    masks = [_head_mask(0), _head_mask(1)]
    qh = [(q * m).astype(BF16) for m in masks]
    row = lax.broadcasted_iota(jnp.int32, (tq, tq), 0)
    col = lax.broadcasted_iota(jnp.int32, (tq, tq), 1)
    upper = (row > col).astype(BF16)
    past = col < row
    acc_ref[...] = jnp.zeros_like(acc_ref)
    carry_ref[...] = jnp.zeros_like(carry_ref)

    def tile(j, diag):
        off = pl.multiple_of(j * tq, tq)
        k = k_ref[0, pl.ds(off, tq), :].astype(BF16)
        v = v_ref[0, pl.ds(off, tq), :].astype(BF16)
        for hh in range(2):
            z = lax.dot_general(qh[hh], k, (((1,), (1,)), ((), ())), preferred_element_type=F32)
            sp = _softplus(z)
            log_rem = jnp.where(past, -sp, 0.0) if diag else -sp
            carry = carry_ref[hh]
            after = _dot_exact_rhs(log_rem, upper) + carry
            w = jnp.exp((z - sp) + after)
            if diag:
                w = jnp.where(past, w, 0.0)
            acc_ref[hh] += jnp.dot(w.astype(BF16), v, preferred_element_type=F32)
            carry_ref[hh] = carry + jnp.sum(log_rem, axis=1, keepdims=True)

    def first_two_tiles():
        off = pl.multiple_of((i - 1) * tq, tq)
        k = k_ref[0, pl.ds(off, 2 * tq), :].astype(BF16)
        v = v_ref[0, pl.ds(off, 2 * tq), :].astype(BF16)
        for hh in range(2):
            z = lax.dot_general(qh[hh], k, (((1,), (1,)), ((), ())), preferred_element_type=F32)
            sp = _softplus(z)
            z_l, z_r, sp_l, sp_r = z[:, :tq], z[:, tq:], sp[:, :tq], sp[:, tq:]
            rem_r = jnp.where(past, -sp_r, 0.0)
            after_r = _dot_exact_rhs(rem_r, upper)
            tot_r = jnp.sum(rem_r, axis=1, keepdims=True)
            after_l = _dot_exact_rhs(-sp_l, upper) + tot_r
            w_r = jnp.where(past, jnp.exp((z_r - sp_r) + after_r), 0.0)
            w_l = jnp.exp((z_l - sp_l) + after_l)
            w = jnp.concatenate([w_l, w_r], axis=1).astype(BF16)
            acc_ref[hh] = jnp.dot(w, v, preferred_element_type=F32)
            carry_ref[hh] = tot_r - jnp.sum(sp_l, axis=1, keepdims=True)

    @pl.when(i == 0)
    def _():
        tile(i, True)

    @pl.when(i > 0)
    def _():
        first_two_tiles()

    def live(c):
        j, worst = c
        return jnp.logical_and(j >= 0, worst > F32_LOG_TINY)

    def body(c):
        j, _ = c
        tile(j, False)
        return j - 1, jnp.max(carry_ref[...])

    lax.while_loop(live, body, (i - 2, jnp.max(carry_ref[...])))
    o_ref[0] = acc_ref[0] * masks[0] + acc_ref[1] * masks[1]


def sb_attention(proj3, tq=256):
    B, S, _ = proj3.shape
    npairs = 4
    return pl.pallas_call(
        functools.partial(_sb_kernel, tq=tq),
        grid=(B, npairs, S // tq),
        in_specs=[pl.BlockSpec((1, tq, LANES), lambda b, p, i: (b, i, EV_SQ + p)),
                  pl.BlockSpec((1, S, LANES), lambda b, p, i: (b, 0, EV_SK + p)),
                  pl.BlockSpec((1, S, LANES), lambda b, p, i: (b, 0, EV_SV + p))],
        out_specs=pl.BlockSpec((1, tq, LANES), lambda b, p, i: (b, i, p)),
        out_shape=jax.ShapeDtypeStruct((B, S, npairs * LANES), F32),
        scratch_shapes=[pltpu.VMEM((2, tq, LANES), F32), pltpu.VMEM((2, tq, 1), F32)],
        compiler_params=_cparams(("parallel", "parallel", "arbitrary")),
        name="sb_attention",
    )(proj3, proj3, proj3)


OD_Q, OD_K, OD_V, OD_RX, OD_RG = 0, 4, 8, 12, 16
OD_W = 4 * LANES


def _moba_prep_kernel(q_ref, k_ref, v_ref, pos_ref, invf_ref, qn_ref, kn_ref, bd_ref,
                      qt_ref, ko_ref, vt_ref, km_ref):
    ang = pos_ref[0].astype(F32) * invf_ref[...]
    cos1, sin1 = jnp.cos(ang), jnp.sin(ang)
    lane = lax.broadcasted_iota(jnp.int32, (1, LANES), 1)
    first_half = (lane % HEAD_DIM) < (HEAD_DIM // 2)
    bd = bd_ref[...]

    def norm_rope(x, gain):
        ms = _dot_exact_rhs(x * x, bd) * (1.0 / HEAD_DIM)
        xn = x * lax.rsqrt(ms + EPS) * gain
        outs = []
        for t in range(OD_W // LANES):
            xb = xn[:, t * LANES:(t + 1) * LANES]
            up = pltpu.roll(xb, LANES - HEAD_DIM // 2, axis=1)
            dn = pltpu.roll(xb, HEAD_DIM // 2, axis=1)
            outs.append(xb * cos1 + jnp.where(first_half, -up, dn) * sin1)
        return jnp.concatenate(outs, axis=1)

    qt_ref[0] = norm_rope(q_ref[0], qn_ref[...]).T.astype(BF16)
    kr = norm_rope(k_ref[0], kn_ref[...])
    ko_ref[0] = kr.astype(BF16)
    km_ref[0, 0] = jnp.mean(kr, axis=0, keepdims=True)
    vt_ref[0] = v_ref[0].T.astype(BF16)


def moba_prep(proj3, pos3, inv_freq_tile, qn_tile, kn_tile, blockdiag):
    B, S, _ = proj3.shape
    tb = MOBA_BLOCK
    nb = S // tb
    full = lambda shape: pl.BlockSpec(shape, lambda b, i: (0,) * len(shape))
    return pl.pallas_call(
        _moba_prep_kernel,
        grid=(B, nb),
        in_specs=[pl.BlockSpec((1, tb, OD_W), lambda b, i: (b, i, OD_Q // 4)),
                  pl.BlockSpec((1, tb, OD_W), lambda b, i: (b, i, OD_K // 4)),
                  pl.BlockSpec((1, tb, OD_W), lambda b, i: (b, i, OD_V // 4)),
                  pl.BlockSpec((1, tb, 1), lambda b, i: (b, i, 0)),
                  full((1, LANES)), full((1, OD_W)), full((1, OD_W)), full((OD_W, OD_W))],
        out_specs=[pl.BlockSpec((1, OD_W, tb), lambda b, i: (b, 0, i)),
                   pl.BlockSpec((1, tb, OD_W), lambda b, i: (b, i, 0)),
                   pl.BlockSpec((1, OD_W, tb), lambda b, i: (b, 0, i)),
                   pl.BlockSpec((1, 1, 1, OD_W), lambda b, i: (b, i, 0, 0))],
        out_shape=[jax.ShapeDtypeStruct((B, OD_W, S), BF16),
                   jax.ShapeDtypeStruct((B, S, OD_W), BF16),
                   jax.ShapeDtypeStruct((B, OD_W, S), BF16),
                   jax.ShapeDtypeStruct((B, nb, 1, OD_W), F32)],
        compiler_params=_cparams(("parallel", "parallel")),
        name="moba_prep",
    )(proj3, proj3, proj3, pos3, inv_freq_tile, qn_tile, kn_tile, blockdiag)


def _moba_kernel(qt_ref, k_ref, vt_ref, km_ref, o_ref, acc_ref, m_ref, sel_ref):
    tb = MOBA_BLOCK
    nbp = km_ref.shape[1]
    i = pl.program_id(2)
    qt = qt_ref[0]
    km = km_ref[0].astype(BF16)
    chan = lax.broadcasted_iota(jnp.int32, (LANES, 1), 0)
    in_head = [(chan // HEAD_DIM) == hh for hh in range(2)]
    blk = lax.broadcasted_iota(jnp.int32, (nbp, tb), 0).astype(F32)
    own = i.astype(F32)
    key = lax.broadcasted_iota(jnp.int32, (tb, tb), 0)
    qry = lax.broadcasted_iota(jnp.int32, (tb, tb), 1)
    visible = key <= qry
    off_own = pl.multiple_of(i * tb, tb)
    k_own = k_ref[0, pl.ds(off_own, tb), :]
    vt_own = vt_ref[0, :, pl.ds(off_own, tb)]
    zero = jnp.zeros_like(qt)
    one = jnp.ones_like(qt)
    qs = []
    for hh in range(2):
        qm = jnp.where(in_head[hh], qt, zero)
        gate = jnp.dot(km, qm, preferred_element_type=F32)
        g = jnp.where(blk < own, gate, -jnp.inf)
        sel = jnp.zeros((nbp, tb), F32)
        for r in range(MOBA_TOPK):
            mx = jnp.max(g, axis=0, keepdims=True)
            idx = jnp.min(jnp.where(g == mx, blk, float(nbp)), axis=0, keepdims=True)
            hit = blk == idx
            keep = jnp.where(i > r, 1.0, 0.0)
            sel = sel + jnp.where(hit, keep, 0.0)
            g = jnp.where(hit, -jnp.inf, g)
        sel_ref[hh] = sel
        qb = qm * (HEAD_DIM ** -0.5)
        qs.append(qb)
        s = jnp.dot(k_own, qb, preferred_element_type=F32)
        s = jnp.where(visible, s, NEG)
        m0 = jnp.max(s, axis=0, keepdims=True)
        p = jnp.exp(s - m0)
        m_ref[hh] = m0
        acc_ref[hh] = jnp.dot(jnp.where(in_head[hh], vt_own, one), p.astype(BF16),
                              preferred_element_type=F32)

    def past_blocks(n, nblk):
        off = pl.multiple_of(n * tb, tb)
        k = k_ref[0, pl.ds(off, nblk * tb), :]
        vt = vt_ref[0, :, pl.ds(off, nblk * tb)]
        scores = [jnp.dot(k, qs[hh], preferred_element_type=F32) for hh in range(2)]
        probs, alphas = [], []
        for hh in range(2):
            parts = []
            for j in range(nblk):
                chosen = sel_ref[hh, pl.ds(n + j, 1), :]
                parts.append(jnp.where(chosen > 0.0, scores[hh][j * tb:(j + 1) * tb, :], NEG))
            m_old = m_ref[hh]
            m_new = m_old
            for part in parts:
                m_new = jnp.maximum(m_new, jnp.max(part, axis=0, keepdims=True))
            alphas.append(jnp.exp(m_old - m_new))
            probs.append(jnp.concatenate([jnp.exp(part - m_new).astype(BF16) for part in parts], axis=0))
            m_ref[hh] = m_new
        ones_v = jnp.ones_like(vt)
        pv = [jnp.dot(jnp.where(in_head[hh], vt, ones_v), probs[hh], preferred_element_type=F32)
              for hh in range(2)]
        for hh in range(2):
            acc_ref[hh] = acc_ref[hh] * alphas[hh] + pv[hh]

    def eight_blocks(t, c):
        past_blocks(8 * t, 8)
        return c

    lax.fori_loop(0, i // 8, eight_blocks, 0)
    rem = i % 8

    @pl.when(rem >= 4)
    def _():
        past_blocks(i - rem, 4)

    @pl.when(rem % 4 >= 2)
    def _():
        past_blocks(i - rem % 4, 2)

    @pl.when(rem % 2 == 1)
    def _():
        past_blocks(i - 1, 1)

    a0, a1 = acc_ref[0], acc_ref[1]
    den0 = a0[HEAD_DIM:HEAD_DIM + 1, :]
    den1 = a1[0:1, :]
    o_ref[0] = jnp.where(in_head[0], a0 / den0, a1 / den1).T


def moba_attention(q_t, k_rot, v_t, kmean):
    B, S, _ = k_rot.shape
    tb = MOBA_BLOCK
    nb = S // tb
    npairs = OD_W // LANES
    nbp = -(-nb // BF16_ROWS) * BF16_ROWS
    kmean = jnp.pad(kmean, ((0, 0), (0, nbp - nb), (0, 0)))
    return pl.pallas_call(
        _moba_kernel,
        grid=(B, npairs, nb),
        in_specs=[pl.BlockSpec((1, LANES, tb), lambda b, p, i: (b, p, i)),
                  pl.BlockSpec((1, S, LANES), lambda b, p, i: (b, 0, p)),
                  pl.BlockSpec((1, LANES, S), lambda b, p, i: (b, p, 0)),
                  pl.BlockSpec((1, nbp, LANES), lambda b, p, i: (b, 0, p))],
        out_specs=pl.BlockSpec((1, tb, LANES), lambda b, p, i: (b, i, p)),
        out_shape=jax.ShapeDtypeStruct((B, S, OD_W), F32),
        scratch_shapes=[pltpu.VMEM((2, LANES, tb), F32), pltpu.VMEM((2, 1, tb), F32),
                        pltpu.VMEM((2, nbp, tb), F32)],
        compiler_params=_cparams(("parallel", "parallel", "arbitrary")),
        name="moba_attention",
    )(q_t, k_rot, v_t, kmean)


def _rglru_kernel(x_ref, gate_ref, cw_ref, cb_ref, wa_ref, ba_ref, wx_ref, bx_ref, lam_ref, o_ref,
                  buf_ref, xprev_ref, hprev_ref, *, ts):
    @pl.when(pl.program_id(1) == 0)
    def _():
        xprev_ref[...] = jnp.zeros_like(xprev_ref)
        hprev_ref[...] = jnp.zeros_like(hprev_ref)

    x = x_ref[0]
    buf_ref[0:8, :] = xprev_ref[...]
    buf_ref[8:8 + ts, :] = x
    xprev_ref[...] = x[ts - 8:ts, :]
    xc = cb_ref[...]
    for kk in range(RG_CONV):
        start = 8 - (RG_CONV - 1) + kk
        xc = xc + cw_ref[kk:kk + 1, :] * buf_ref[start:start + ts, :]
    r = jax.nn.sigmoid(_dot(xc, wa_ref[...]) + ba_ref[...])
    ig = jax.nn.sigmoid(_dot(xc, wx_ref[...]) + bx_ref[...])
    log_a = (-RG_C) * r * _softplus(-lam_ref[...])
    a = jnp.exp(log_a)
    u = jnp.sqrt(-jnp.tanh(log_a) * (a * a + 1.0)) * (ig * xc)
    t_idx = lax.broadcasted_iota(jnp.int32, (ts, 1), 0)
    d = 1
    while d < ts:
        valid = t_idx >= d
        a_sh = pltpu.roll(a, d, axis=0)
        u_sh = pltpu.roll(u, d, axis=0)
        u = jnp.where(valid, a * u_sh, 0.0) + u
        a = jnp.where(valid, a * a_sh, a)
        d *= 2
    h = a * hprev_ref[...] + u
    hprev_ref[...] = h[ts - 1:ts, :]
    o_ref[0] = h * jax.nn.gelu(gate_ref[0], approximate=True)


def rg_lru_mixer(proj3, conv_w, conv_b, wa_bd, ba, wx_bd, bx, lam, ts=256):
    B, S, _ = proj3.shape
    W = conv_w.shape[1]
    full = lambda shape: pl.BlockSpec(shape, lambda b, t: (0,) * len(shape))
    row = lambda v: v.reshape(1, W)
    return pl.pallas_call(
        functools.partial(_rglru_kernel, ts=ts),
        grid=(B, S // ts),
        in_specs=[pl.BlockSpec((1, ts, W), lambda b, t: (b, t, OD_RX // 4)),
                  pl.BlockSpec((1, ts, W), lambda b, t: (b, t, OD_RG // 4)),
                  full((RG_CONV, W)), full((1, W)), full((W, W)), full((1, W)), full((W, W)),
                  full((1, W)), full((1, W))],
        out_specs=pl.BlockSpec((1, ts, W), lambda b, t: (b, t, 0)),
        out_shape=jax.ShapeDtypeStruct((B, S, W), F32),
        scratch_shapes=[pltpu.VMEM((ts + 8, W), F32), pltpu.VMEM((8, W), F32), pltpu.VMEM((1, W), F32)],
        compiler_params=_cparams(("parallel", "arbitrary")),
        name="rg_lru",
    )(proj3, proj3, conv_w, row(conv_b), wa_bd, row(ba), wx_bd, row(bx), row(lam))


def _xattn_kernel(x_ref, a_ref, b_ref, wmix_ref, kv_ref, g_ref, wq_ref, qn_ref, kn_ref, wo_ref, o_ref):
    ka = a_ref.shape[2]
    x = x_ref[0] + _dot(a_ref[0], wmix_ref[0:ka, :]) + _dot(b_ref[0], wmix_ref[ka:, :])
    q = _dot(_rms(x, g_ref[...]), wq_ref[...])
    kv = kv_ref[0]
    hw = XA_HEADS * XA_DH
    outs = []
    for h in range(XA_HEADS):
        hl = slice(h * XA_DH, (h + 1) * XA_DH)
        qh = _rms(q[:, hl], qn_ref[...])
        kh = _rms(kv[:, hl], kn_ref[...])
        vh = kv[:, hw + h * XA_DH: hw + (h + 1) * XA_DH]
        s = _dot_nt(qh, kh) * (XA_DH ** -0.5)
        s = s - jnp.max(s, axis=1, keepdims=True)
        p = jnp.exp(s)
        p = p / jnp.sum(p, axis=1, keepdims=True)
        outs.append(_dot(p, vh))
    o_ref[0] = x + _dot(jnp.concatenate(outs, axis=1), wo_ref[...])


def cross_attention(x3, a3, b3, wmix_bf16, kv3, g, wq_bf16, qn, kn, wo_bf16, tq=256):
    B, S, D = x3.shape
    M = kv3.shape[1]
    ka, kb = a3.shape[2], b3.shape[2]
    hw = XA_HEADS * XA_DH
    full = lambda shape: pl.BlockSpec(shape, lambda b, i: (0,) * len(shape))
    return pl.pallas_call(
        _xattn_kernel,
        grid=(B, S // tq),
        in_specs=[pl.BlockSpec((1, tq, D), lambda b, i: (b, i, 0)),
                  pl.BlockSpec((1, tq, ka), lambda b, i: (b, i, 0)),
                  pl.BlockSpec((1, tq, kb), lambda b, i: (b, i, 0)),
                  full((ka + kb, D)),
                  pl.BlockSpec((1, M, 2 * hw), lambda b, i: (b, 0, 0)),
                  full((1, D)), full((D, hw)), full((1, XA_DH)), full((1, XA_DH)), full((hw, D))],
        out_specs=pl.BlockSpec((1, tq, D), lambda b, i: (b, i, 0)),
        out_shape=jax.ShapeDtypeStruct((B, S, D), F32),
        compiler_params=_cparams(("parallel", "parallel")),
        name="cross_attention",
    )(x3, a3, b3, wmix_bf16, kv3, g.reshape(1, D), wq_bf16, qn.reshape(1, -1), kn.reshape(1, -1), wo_bf16)


BF16_ROWS = 16


def _routing_stages(s1, s2, sub, finish, nstage):
    K = PEER_TOPK
    st = {"a1": s1, "a2": s2, "v1": [], "v2": [], "rank2": jnp.full(s2.shape, float(K), F32)}

    def extract(src, dst, rounds, want_rank):
        def run(nil):
            arr = st[src]
            gone = nil - jnp.inf
            for r in rounds:
                m = jnp.max(arr, axis=0, keepdims=True)
                st[dst].append(m)
                hit = arr == m
                if want_rank:
                    st["rank2"] = jnp.where(hit, float(r), st["rank2"])
                arr = jnp.where(hit, gone, arr)
            st[src] = arr
            st["last"] = m
        return run

    def candidates(nil):
        v1, v2 = st["v1"], st["v2"]
        v1m = jnp.concatenate(v1, axis=0)
        v2m = jnp.concatenate(v2, axis=0)
        groups = [(v1[0] + nil) + v2m]
        for a in range(1, 8):
            groups.append(jnp.where(sub < K // (a + 1), v1[a] + v2m[0:8], -jnp.inf))
        groups.append(v1m[8:16] + v2[0])
        st["cand"] = jnp.concatenate(groups, axis=0)
        st["top"] = []
        st["last"] = v2[0]

    def threshold(nil):
        top = st["top"]
        st["tau"] = top[K - 1]
        z = nil
        for tv in top:
            z = z + jnp.exp(tv - top[0])
        st["z"] = z
        st["last"] = z

    def counts(nil):
        v1, v2, tau = st["v1"], st["v2"], st["tau"]
        count = jnp.zeros(s1.shape, F32)
        for bb in range(K // 2):
            count = count + jnp.where(s1 + v2[bb] >= tau, 1.0, 0.0)
        best = nil
        for bb in range(K // 2, K):
            best = best + jnp.where(v1[0] + v2[bb] >= tau, 1.0, 0.0)
        count = count + jnp.where(s1 == v1[0], best, 0.0)
        finish(count, jnp.exp(s1 - v1[0]) / st["z"], st["rank2"].astype(BF16),
               jnp.exp(s2 - v2[0]).astype(BF16))
        st["last"] = best

    quarter = [range(q * 4, q * 4 + 4) for q in range(4)]
    work = ([extract("a1", "v1", r, False) for r in quarter]
            + [extract("a2", "v2", r, True) for r in quarter]
            + [candidates] + [extract("cand", "top", r, False) for r in quarter]
            + [threshold, counts])
    bounds = [round(j * len(work) / nstage) for j in range(nstage + 1)]

    def stage(j):
        def run(nil):
            for piece in work[bounds[j]:bounds[j + 1]]:
                piece(nil)
            return st["last"]
        return run

    return [stage(j) for j in range(nstage)]


def _peer_kernel(x_ref, xnext_ref, g_ref, wq_ref, sk_ref, u_ref, vt_ref, o_ref,
                 xn_ref, xnt_ref, c_ref, a_ref, r2_ref, e2_ref, acc_ref, s_ref, *, ib, isub):
    t = pl.program_id(0)
    e = pl.program_id(1)
    nk = PEER_NKEYS
    tq = x_ref.shape[0]
    slot = t % 2
    nslot = 1 - slot
    sub = lax.broadcasted_iota(jnp.int32, (8, 1), 0)
    nsub = ib // isub

    def prepare(src_ref, s):
        xn32 = _rms(src_ref[...], g_ref[...])
        xn_ref[...] = xn32.astype(BF16)
        xnt_ref[s] = xn32.T.astype(BF16)

    def route_scores(h):
        q = jnp.dot(xn_ref[...], wq_ref[h], preferred_element_type=F32)
        return _dot_nt(sk_ref[2 * h], q[:, :nk]), _dot_nt(sk_ref[2 * h + 1], q[:, nk:])

    def table_writer(h, s):
        def finish(count, amp, rank2, e2):
            c_ref[s, h] = count
            a_ref[s, h] = amp
            r2_ref[s, h] = rank2
            e2_ref[s, h] = e2
        return finish

    @pl.when(jnp.logical_and(t == 0, e == 0))
    def _():
        prepare(x_ref, 0)

        def one(h, c):
            s1, s2 = route_scores(h)
            for run in _routing_stages(s1, s2, sub, table_writer(h, 0), 1):
                run(jnp.zeros((1, tq), F32))
            return c

        lax.fori_loop(0, PEER_HEADS, one, 0)

    @pl.when(e == 0)
    def _():
        acc_ref[...] = jnp.zeros_like(acc_ref)
        prepare(xnext_ref, nslot)
        s_ref[0], s_ref[1] = route_scores(0)

    ngrp = nk // BF16_ROWS
    xnt = xnt_ref[slot]
    first = pl.multiple_of(e * ib, ib)
    step_c = [c_ref[slot, h, pl.ds(first, ib), :] for h in range(PEER_HEADS)]
    step_a = [a_ref[slot, h, pl.ds(first, ib), :] for h in range(PEER_HEADS)]

    def expert_matmul(sc):
        lo = sc * isub * nk
        u_sub = pltpu.bitcast(u_ref[lo // 2:(lo + isub * nk) // 2, :], BF16)
        return jnp.dot(u_sub, xnt, preferred_element_type=F32)

    stages = _routing_stages(s_ref[0], s_ref[1], sub, table_writer(e, nslot), nsub)
    hid_next = expert_matmul(0)
    ahead_s1, ahead_s2 = route_scores(jnp.minimum(e + 1, PEER_HEADS - 1))

    total = None
    after = jnp.zeros((1, tq), F32)
    for sc in range(nsub):
        zero = jnp.broadcast_to(stages[sc](after) * 0.0, (BF16_ROWS, tq)).astype(BF16)
        lo = sc * isub * nk
        hid_sub = hid_next
        if sc + 1 < nsub:
            hid_next = expert_matmul(sc + 1)
        pieces = []
        for ii in range(isub):
            il = sc * isub + ii
            hid = hid_sub[ii * nk:(ii + 1) * nk, :]
            act = (0.5 * hid * (1.0 + lax.erf(hid * np.float32(np.sqrt(0.5))))).astype(BF16)
            w = [zero] * ngrp
            for h in range(PEER_HEADS):
                cnt = jnp.broadcast_to(step_c[h][il:il + 1, :], (BF16_ROWS, tq)).astype(BF16)
                amp = jnp.broadcast_to(step_a[h][il:il + 1, :], (BF16_ROWS, tq)).astype(BF16)
                for gi in range(ngrp):
                    rows = slice(gi * BF16_ROWS, (gi + 1) * BF16_ROWS)
                    w[gi] = w[gi] + jnp.where(r2_ref[slot, h, rows, :] < cnt,
                                              e2_ref[slot, h, rows, :] * amp, zero)
            pieces += [w[gi] * act[gi * BF16_ROWS:(gi + 1) * BF16_ROWS, :] for gi in range(ngrp)]
        after = pieces[-1][0:1, :].astype(F32) * 0.0
        vt_sub = pltpu.bitcast(vt_ref[:, lo:lo + isub * nk], BF16)
        part = jnp.dot(vt_sub, jnp.concatenate(pieces, axis=0), preferred_element_type=F32)
        total = part if total is None else total + part
    acc_ref[...] += total
    s_ref[0] = ahead_s1
    s_ref[1] = ahead_s2

    @pl.when(e == pl.num_programs(1) - 1)
    def _():
        o_ref[...] = x_ref[...] + acc_ref[...].T


def _pair_rows_kernel(x_ref, o_ref, *, transpose):
    x = x_ref[0]
    if transpose:
        x = x.T
    o_ref[...] = pltpu.bitcast(x.astype(BF16), jnp.uint32)


def _pair_rows(w3, layer, transpose=False, tr=512):
    _, R, C = w3.shape
    in_spec = pl.BlockSpec((1, tr, C), lambda i: (layer, i, 0))
    if transpose:
        out_spec = pl.BlockSpec((C // 2, tr), lambda i: (0, i))
        out_shape = jax.ShapeDtypeStruct((C // 2, R), jnp.uint32)
    else:
        out_spec = pl.BlockSpec((tr // 2, C), lambda i: (i, 0))
        out_shape = jax.ShapeDtypeStruct((R // 2, C), jnp.uint32)
    return pl.pallas_call(
        functools.partial(_pair_rows_kernel, transpose=transpose),
        grid=(R // tr,), in_specs=[in_spec], out_specs=out_spec, out_shape=out_shape,
        compiler_params=_cparams(("parallel",)),
        name="pair_rows",
    )(w3)


def peer_ffn(x2d, g, wq3_bf16, sk_bf16, u_pairs, vt_pairs, tq=256, isub=2):
    T, D = x2d.shape
    H = PEER_HEADS
    ne = H
    ib = PEER_NKEYS // ne
    ec = ib * PEER_NKEYS
    nt = T // tq
    full = lambda shape: pl.BlockSpec(shape, lambda i, e: (0,) * len(shape))
    words = pltpu.VMEM((2, H, PEER_NKEYS, tq), F32)
    halves = pltpu.VMEM((2, H, PEER_NKEYS, tq), BF16)
    return pl.pallas_call(
        functools.partial(_peer_kernel, ib=ib, isub=isub),
        grid=(nt, ne),
        in_specs=[pl.BlockSpec((tq, D), lambda i, e: (i, 0)),
                  pl.BlockSpec((tq, D), lambda i, e: (jnp.minimum(i + 1, nt - 1), 0)),
                  full((1, D)), full((H, D, 2 * PEER_NKEYS)), full((2 * H, PEER_NKEYS, PEER_NKEYS)),
                  pl.BlockSpec((ec // 2, D), lambda i, e: (e, 0)),
                  pl.BlockSpec((D // 2, ec), lambda i, e: (0, e))],
        out_specs=pl.BlockSpec((tq, D), lambda i, e: (i, 0)),
        out_shape=jax.ShapeDtypeStruct((T, D), F32),
        scratch_shapes=[pltpu.VMEM((tq, D), BF16), pltpu.VMEM((2, D, tq), BF16),
                        words, words, halves, halves, pltpu.VMEM((D, tq), F32),
                        pltpu.VMEM((2, PEER_NKEYS, tq), F32)],
        compiler_params=_cparams(("arbitrary", "arbitrary")),
        name="peer_ffn",
    )(x2d, x2d, g.reshape(1, D), wq3_bf16, sk_bf16, u_pairs, vt_pairs)


def _block_diag(w):
    G, n, _ = w.shape
    eye = jnp.eye(G, dtype=w.dtype)
    return (eye[:, None, :, None] * w[:, :, None, :]).reshape(G * n, G * n)


def even_layer(x2d, B, S, norm, w_in, gate_up, gate_b, out_norm, w_out):
    D = x2d.shape[1]
    gg0 = EV_GG * LANES
    wp = jnp.concatenate([w_in[:, :1536], w_in[:, 1536 + GLA_RANK:], w_in[:, 1536:1536 + GLA_RANK],
                          jnp.zeros((D, EV_COLS - gg0 - GLA_RANK), w_in.dtype)], axis=1).astype(BF16)
    proj = norm_proj(x2d, norm, wp).reshape(B, S, EV_COLS)
    gup = jnp.concatenate([gate_up, jnp.zeros((LANES - GLA_RANK, gate_up.shape[1]), gate_up.dtype)], axis=0)
    a_out = gla_mixer(proj, gup.astype(BF16), gate_b, out_norm)
    b_out = sb_attention(proj)
    return a_out, b_out, w_out.astype(BF16)


def odd_layer(x2d, B, S, positions, norm, w_in, q_norm, k_norm, conv_w, conv_b, wa, ba, wx, bx, lam, w_out):
    proj = norm_proj(x2d, norm, w_in.astype(BF16)).reshape(B, S, -1)
    half = HEAD_DIM // 2
    inv_freq = ROPE_THETA ** (-jnp.arange(0, HEAD_DIM, 2, dtype=F32) / HEAD_DIM)
    inv_tile = jnp.tile(inv_freq, LANES // half).reshape(1, LANES)
    seg = np.arange(OD_W) // HEAD_DIM
    blockdiag = jnp.asarray(seg[:, None] == seg[None, :], dtype=BF16)
    q_t, k_rot, v_t, kmean = moba_prep(proj, positions.reshape(B, S, 1), inv_tile,
                                       jnp.tile(q_norm, OD_W // HEAD_DIM).reshape(1, OD_W),
                                       jnp.tile(k_norm, OD_W // HEAD_DIM).reshape(1, OD_W), blockdiag)
    c_out = moba_attention(q_t, k_rot, v_t, kmean.reshape(B, S // MOBA_BLOCK, OD_W))
    d_out = rg_lru_mixer(proj, conv_w, conv_b, _block_diag(wa).astype(BF16), ba,
                         _block_diag(wx).astype(BF16), bx, lam)
    return c_out, d_out, w_out.astype(BF16)


def kernel(x, mem, positions, ev_norm, ev_w_in, ev_gla_gate_up, ev_gla_gate_b, ev_gla_out_norm, ev_w_out, od_norm, od_w_in, od_q_norm, od_k_norm, od_conv_w, od_conv_b, od_gate_a_w, od_gate_a_b, od_gate_x_w, od_gate_x_b, od_lambda, od_w_out, xa_norm, xa_mem_norm, xa_wq, xa_wkv, xa_q_norm, xa_k_norm, xa_wo, ffn_norm, peer_wq, peer_subkeys, peer_u, peer_v):
    B, S, D = x.shape
    M = mem.shape[1]
    depth = xa_norm.shape[0]
    x2d = x.reshape(B * S, D)
    mem2d = mem.reshape(B * M, D)
    for l in range(depth):
        if l % 2 == 0:
            e = l // 2
            mix = even_layer(x2d, B, S, ev_norm[e], ev_w_in[e], ev_gla_gate_up[e], ev_gla_gate_b[e],
                             ev_gla_out_norm[e], ev_w_out[e])
        else:
            o = l // 2
            mix = odd_layer(x2d, B, S, positions, od_norm[o], od_w_in[o], od_q_norm[o], od_k_norm[o],
                            od_conv_w[o], od_conv_b[o], od_gate_a_w[o], od_gate_a_b[o],
                            od_gate_x_w[o], od_gate_x_b[o], od_lambda[o], od_w_out[o])
        kv = norm_proj(mem2d, xa_mem_norm[l], xa_wkv[l].astype(BF16))
        x2d = cross_attention(x2d.reshape(B, S, D), *mix, kv.reshape(B, M, -1), xa_norm[l],
                              xa_wq[l].astype(BF16), xa_q_norm[l], xa_k_norm[l],
                              xa_wo[l].astype(BF16)).reshape(B * S, D)
        H = PEER_HEADS
        wq3 = peer_wq[l].reshape(D, H, 2 * PEER_NKEYS).transpose(1, 0, 2).astype(BF16)
        sk = peer_subkeys[l].reshape(2 * H, PEER_NKEYS, -1).astype(BF16)
        x2d = peer_ffn(x2d, ffn_norm[l], wq3, sk, _pair_rows(peer_u, l),
                       _pair_rows(peer_v, l, transpose=True))
    return x2d.reshape(B, S, D)
```

```python
import functools

import jax
import jax.numpy as jnp
import numpy as np
from jax import lax
from jax.experimental import pallas as pl
from jax.experimental.pallas import tpu as pltpu

F32 = jnp.float32
BF16 = jnp.bfloat16

EPS = 1e-6
ROPE_THETA = 10000.0
LANES = 128
HEAD_DIM = 64
GLA_HEADS = 4
GLA_DV = 128
GLA_CHUNK = 64
GLA_TAU = 16.0
GLA_RANK = 16
MOBA_BLOCK = 256
MOBA_TOPK = 3
RG_C = 8.0
RG_CONV = 4
XA_HEADS = 4
XA_DH = 128
PEER_HEADS = 8
PEER_NKEYS = 128
PEER_TOPK = 16
NEG = -1e30
F32_LOG_TINY = -104.0
VMEM_LIMIT = 56 * 1024 * 1024


def _cparams(sem):
    return pltpu.CompilerParams(dimension_semantics=sem, vmem_limit_bytes=VMEM_LIMIT)


def _dot(a, b):
    return jnp.dot(a.astype(BF16), b.astype(BF16), preferred_element_type=F32)


def _dot_nt(a, b):
    return lax.dot_general(a.astype(BF16), b.astype(BF16), (((1,), (1,)), ((), ())),
                           preferred_element_type=F32)


def _dot_tn(a, b):
    return lax.dot_general(a.astype(BF16), b.astype(BF16), (((0,), (0,)), ((), ())),
                           preferred_element_type=F32)


def _split2(a):
    hi = a.astype(BF16)
    lo = (a - hi.astype(F32)).astype(BF16)
    return hi, lo


def _dot_exact_rhs(a, m_bf16):
    hi, lo = _split2(a)
    return (jnp.dot(hi, m_bf16, preferred_element_type=F32)
            + jnp.dot(lo, m_bf16, preferred_element_type=F32))


def _dot_exact_lhs(m_bf16, a):
    hi, lo = _split2(a)
    return (jnp.dot(m_bf16, hi, preferred_element_type=F32)
            + jnp.dot(m_bf16, lo, preferred_element_type=F32))


def _rms(x, g):
    return x * lax.rsqrt(jnp.mean(x * x, axis=-1, keepdims=True) + EPS) * g


def _softplus(z):
    return jnp.maximum(z, 0.0) + jnp.log1p(jnp.exp(-jnp.abs(z)))


def _head_mask(hh):
    lane = lax.broadcasted_iota(jnp.int32, (1, LANES), 1)
    return ((lane // HEAD_DIM) == hh).astype(F32)


def _norm_proj_kernel(x_ref, g_ref, w_ref, o_ref):
    xn = _rms(x_ref[...], g_ref[...])
    o_ref[...] = jnp.dot(xn.astype(BF16), w_ref[...], preferred_element_type=F32)


def norm_proj(x2d, g, w_bf16, tm=512):
    T, D = x2d.shape
    N = w_bf16.shape[1]
    return pl.pallas_call(
        _norm_proj_kernel,
        grid=(T // tm,),
        in_specs=[pl.BlockSpec((tm, D), lambda i: (i, 0)),
                  pl.BlockSpec((1, D), lambda i: (0, 0)),
                  pl.BlockSpec((D, N), lambda i: (0, 0))],
        out_specs=pl.BlockSpec((tm, N), lambda i: (i, 0)),
        out_shape=jax.ShapeDtypeStruct((T, N), F32),
        compiler_params=_cparams(("parallel",)),
        name="norm_proj",
    )(x2d, g.reshape(1, D), w_bf16)


EV_Q, EV_K, EV_V, EV_R, EV_SQ, EV_SK, EV_SV, EV_GG = 0, 2, 4, 8, 12, 16, 20, 24
EV_COLS = 25 * LANES


def _gla_kernel(q_ref, k_ref, v_ref, r_ref, gg_ref, gup_ref, gb_ref, onorm_ref, o_ref, state_ref,
                *, ts):
    C = GLA_CHUNK

    @pl.when(pl.program_id(1) == 0)
    def _():
        state_ref[...] = jnp.zeros_like(state_ref)

    row = lax.broadcasted_iota(jnp.int32, (C, C), 0)
    col = lax.broadcasted_iota(jnp.int32, (C, C), 1)
    causal = row >= col
    tri = causal.astype(BF16)
    masks = [_head_mask(0), _head_mask(1)]
    scale = HEAD_DIM ** -0.5
    onorm = onorm_ref[...]

    for ci in range(ts // C):
        sl = slice(ci * C, (ci + 1) * C)
        pre = _dot(gg_ref[0, sl, :], gup_ref[...]) + gb_ref[...]
        g = (jnp.minimum(pre, 0.0) - jnp.log1p(jnp.exp(-jnp.abs(pre)))) * (1.0 / GLA_TAU)
        b = _dot_exact_lhs(tri, g)
        bmid = b[C // 2 - 1:C // 2, :]
        blast = b[C - 1:C, :]
        q = q_ref[0, sl, :] * scale
        k = k_ref[0, sl, :]
        qd = q * jnp.exp(b - bmid)
        kd = k * jnp.exp(bmid - b)
        qe = q * jnp.exp(b)
        kdec = k * jnp.exp(blast - b)
        eb_last = jnp.exp(blast)
        for p in range(GLA_HEADS // 2):
            lanes = slice(p * LANES, (p + 1) * LANES)
            st = state_ref[p]
            new_st = st * eb_last[:, lanes]
            for hh in range(2):
                h = 2 * p + hh
                m = masks[hh]
                hl = slice(h * GLA_DV, (h + 1) * GLA_DV)
                a = _dot_nt(qd[:, lanes] * m, kd[:, lanes])
                a = jnp.where(causal, a, 0.0)
                v_h = v_ref[0, sl, hl]
                o = _dot(a, v_h) + _dot_nt(qe[:, lanes] * m, st)
                o = _rms(o, onorm)
                r_h = r_ref[0, sl, hl]
                o_ref[0, sl, hl] = o * (r_h * jax.nn.sigmoid(r_h))
                new_st = new_st + _dot_tn(v_h, kdec[:, lanes] * m)
            state_ref[p] = new_st


def gla_mixer(proj3, gate_up_pad, gate_b, out_norm, ts=512):
    B, S, _ = proj3.shape
    nh = GLA_HEADS
    w = nh * GLA_DV

    def col(blk_w, tile):
        idx = tile * LANES // blk_w
        return pl.BlockSpec((1, ts, blk_w), lambda b, c: (b, c, idx))

    return pl.pallas_call(
        functools.partial(_gla_kernel, ts=ts),
        grid=(B, S // ts),
        in_specs=[col(nh * HEAD_DIM, EV_Q), col(nh * HEAD_DIM, EV_K), col(w, EV_V), col(w, EV_R),
                  col(LANES, EV_GG),
                  pl.BlockSpec((LANES, nh * HEAD_DIM), lambda b, c: (0, 0)),
                  pl.BlockSpec((1, nh * HEAD_DIM), lambda b, c: (0, 0)),
                  pl.BlockSpec((1, GLA_DV), lambda b, c: (0, 0))],
        out_specs=pl.BlockSpec((1, ts, w), lambda b, c: (b, c, 0)),
        out_shape=jax.ShapeDtypeStruct((B, S, w), F32),
        scratch_shapes=[pltpu.VMEM((nh // 2, GLA_DV, LANES), F32)],
        compiler_params=_cparams(("parallel", "arbitrary")),
        name="gla",
    )(proj3, proj3, proj3, proj3, proj3, gate_up_pad, gate_b.reshape(1, -1), out_norm.reshape(1, -1))


def _sb_kernel(q_ref, k_ref, v_ref, o_ref, acc_ref, carry_ref, *, tq):
    i = pl.program_id(2)
    q = q_ref[0] * (HEAD_DIM ** -0.5)
    masks = [_head_mask(0), _head_mask(1)]
    qh = [(q * m).astype(BF16) for m in masks]
    row = lax.broadcasted_iota(jnp.int32, (tq, tq), 0)
    col = lax.broadcasted_iota(jnp.int32, (tq, tq), 1)
    upper = (row > col).astype(BF16)
    past = col < row
    acc_ref[...] = jnp.zeros_like(acc_ref)
    carry_ref[...] = jnp.zeros_like(carry_ref)

    def tile(j, diag):
        off = pl.multiple_of(j * tq, tq)
        k = k_ref[0, pl.ds(off, tq), :].astype(BF16)
        v = v_ref[0, pl.ds(off, tq), :].astype(BF16)
        for hh in range(2):
            z = lax.dot_general(qh[hh], k, (((1,), (1,)), ((), ())), preferred_element_type=F32)
            sp = _softplus(z)
            log_rem = jnp.where(past, -sp, 0.0) if diag else -sp
            carry = carry_ref[hh]
            after = _dot_exact_rhs(log_rem, upper) + carry
            w = jnp.exp((z - sp) + after)
            if diag:
                w = jnp.where(past, w, 0.0)
            acc_ref[hh] += jnp.dot(w.astype(BF16), v, preferred_element_type=F32)
            carry_ref[hh] = carry + jnp.sum(log_rem, axis=1, keepdims=True)

    def first_two_tiles():
        off = pl.multiple_of((i - 1) * tq, tq)
        k = k_ref[0, pl.ds(off, 2 * tq), :].astype(BF16)
        v = v_ref[0, pl.ds(off, 2 * tq), :].astype(BF16)
        for hh in range(2):
            z = lax.dot_general(qh[hh], k, (((1,), (1,)), ((), ())), preferred_element_type=F32)
            sp = _softplus(z)
            z_l, z_r, sp_l, sp_r = z[:, :tq], z[:, tq:], sp[:, :tq], sp[:, tq:]
            rem_r = jnp.where(past, -sp_r, 0.0)
            after_r = _dot_exact_rhs(rem_r, upper)
            tot_r = jnp.sum(rem_r, axis=1, keepdims=True)
            after_l = _dot_exact_rhs(-sp_l, upper) + tot_r
            w_r = jnp.where(past, jnp.exp((z_r - sp_r) + after_r), 0.0)
            w_l = jnp.exp((z_l - sp_l) + after_l)
            w = jnp.concatenate([w_l, w_r], axis=1).astype(BF16)
            acc_ref[hh] = jnp.dot(w, v, preferred_element_type=F32)
            carry_ref[hh] = tot_r - jnp.sum(sp_l, axis=1, keepdims=True)

    @pl.when(i == 0)
    def _():
        tile(i, True)

    @pl.when(i > 0)
    def _():
        first_two_tiles()

    def live(c):
        j, worst = c
        return jnp.logical_and(j >= 0, worst > F32_LOG_TINY)

    def body(c):
        j, _ = c
        tile(j, False)
        return j - 1, jnp.max(carry_ref[...])

    lax.while_loop(live, body, (i - 2, jnp.max(carry_ref[...])))
    o_ref[0] = acc_ref[0] * masks[0] + acc_ref[1] * masks[1]


def sb_attention(proj3, tq=256):
    B, S, _ = proj3.shape
    npairs = 4
    return pl.pallas_call(
        functools.partial(_sb_kernel, tq=tq),
        grid=(B, npairs, S // tq),
        in_specs=[pl.BlockSpec((1, tq, LANES), lambda b, p, i: (b, i, EV_SQ + p)),
                  pl.BlockSpec((1, S, LANES), lambda b, p, i: (b, 0, EV_SK + p)),
                  pl.BlockSpec((1, S, LANES), lambda b, p, i: (b, 0, EV_SV + p))],
        out_specs=pl.BlockSpec((1, tq, LANES), lambda b, p, i: (b, i, p)),
        out_shape=jax.ShapeDtypeStruct((B, S, npairs * LANES), F32),
        scratch_shapes=[pltpu.VMEM((2, tq, LANES), F32), pltpu.VMEM((2, tq, 1), F32)],
        compiler_params=_cparams(("parallel", "parallel", "arbitrary")),
        name="sb_attention",
    )(proj3, proj3, proj3)


OD_Q, OD_K, OD_V, OD_RX, OD_RG = 0, 4, 8, 12, 16
OD_W = 4 * LANES


def _moba_prep_kernel(q_ref, k_ref, v_ref, pos_ref, invf_ref, qn_ref, kn_ref, bd_ref,
                      qt_ref, ko_ref, vt_ref, km_ref):
    ang = pos_ref[0].astype(F32) * invf_ref[...]
    cos1, sin1 = jnp.cos(ang), jnp.sin(ang)
    lane = lax.broadcasted_iota(jnp.int32, (1, LANES), 1)
    first_half = (lane % HEAD_DIM) < (HEAD_DIM // 2)
    bd = bd_ref[...]

    def norm_rope(x, gain):
        ms = _dot_exact_rhs(x * x, bd) * (1.0 / HEAD_DIM)
        xn = x * lax.rsqrt(ms + EPS) * gain
        outs = []
        for t in range(OD_W // LANES):
            xb = xn[:, t * LANES:(t + 1) * LANES]
            up = pltpu.roll(xb, LANES - HEAD_DIM // 2, axis=1)
            dn = pltpu.roll(xb, HEAD_DIM // 2, axis=1)
            outs.append(xb * cos1 + jnp.where(first_half, -up, dn) * sin1)
        return jnp.concatenate(outs, axis=1)

    qt_ref[0] = norm_rope(q_ref[0], qn_ref[...]).T.astype(BF16)
    kr = norm_rope(k_ref[0], kn_ref[...])
    ko_ref[0] = kr.astype(BF16)
    km_ref[0, 0] = jnp.mean(kr, axis=0, keepdims=True)
    vt_ref[0] = v_ref[0].T.astype(BF16)


def moba_prep(proj3, pos3, inv_freq_tile, qn_tile, kn_tile, blockdiag):
    B, S, _ = proj3.shape
    tb = MOBA_BLOCK
    nb = S // tb
    full = lambda shape: pl.BlockSpec(shape, lambda b, i: (0,) * len(shape))
    return pl.pallas_call(
        _moba_prep_kernel,
        grid=(B, nb),
        in_specs=[pl.BlockSpec((1, tb, OD_W), lambda b, i: (b, i, OD_Q // 4)),
                  pl.BlockSpec((1, tb, OD_W), lambda b, i: (b, i, OD_K // 4)),
                  pl.BlockSpec((1, tb, OD_W), lambda b, i: (b, i, OD_V // 4)),
                  pl.BlockSpec((1, tb, 1), lambda b, i: (b, i, 0)),
                  full((1, LANES)), full((1, OD_W)), full((1, OD_W)), full((OD_W, OD_W))],
        out_specs=[pl.BlockSpec((1, OD_W, tb), lambda b, i: (b, 0, i)),
                   pl.BlockSpec((1, tb, OD_W), lambda b, i: (b, i, 0)),
                   pl.BlockSpec((1, OD_W, tb), lambda b, i: (b, 0, i)),
                   pl.BlockSpec((1, 1, 1, OD_W), lambda b, i: (b, i, 0, 0))],
        out_shape=[jax.ShapeDtypeStruct((B, OD_W, S), BF16),
                   jax.ShapeDtypeStruct((B, S, OD_W), BF16),
                   jax.ShapeDtypeStruct((B, OD_W, S), BF16),
                   jax.ShapeDtypeStruct((B, nb, 1, OD_W), F32)],
        compiler_params=_cparams(("parallel", "parallel")),
        name="moba_prep",
    )(proj3, proj3, proj3, pos3, inv_freq_tile, qn_tile, kn_tile, blockdiag)


def _moba_kernel(qt_ref, k_ref, vt_ref, km_ref, o_ref, acc_ref, m_ref, sel_ref):
    tb = MOBA_BLOCK
    nbp = km_ref.shape[1]
    i = pl.program_id(2)
    qt = qt_ref[0]
    km = km_ref[0].astype(BF16)
    chan = lax.broadcasted_iota(jnp.int32, (LANES, 1), 0)
    in_head = [(chan // HEAD_DIM) == hh for hh in range(2)]
    blk = lax.broadcasted_iota(jnp.int32, (nbp, tb), 0).astype(F32)
    own = i.astype(F32)
    key = lax.broadcasted_iota(jnp.int32, (tb, tb), 0)
    qry = lax.broadcasted_iota(jnp.int32, (tb, tb), 1)
    visible = key <= qry
    off_own = pl.multiple_of(i * tb, tb)
    k_own = k_ref[0, pl.ds(off_own, tb), :]
    vt_own = vt_ref[0, :, pl.ds(off_own, tb)]
    zero = jnp.zeros_like(qt)
    one = jnp.ones_like(qt)
    qs = []
    for hh in range(2):
        qm = jnp.where(in_head[hh], qt, zero)
        gate = jnp.dot(km, qm, preferred_element_type=F32)
        g = jnp.where(blk < own, gate, -jnp.inf)
        sel = jnp.zeros((nbp, tb), F32)
        for r in range(MOBA_TOPK):
            mx = jnp.max(g, axis=0, keepdims=True)
            idx = jnp.min(jnp.where(g == mx, blk, float(nbp)), axis=0, keepdims=True)
            hit = blk == idx
            keep = jnp.where(i > r, 1.0, 0.0)
            sel = sel + jnp.where(hit, keep, 0.0)
            g = jnp.where(hit, -jnp.inf, g)
        sel_ref[hh] = sel
        qb = qm * (HEAD_DIM ** -0.5)
        qs.append(qb)
        s = jnp.dot(k_own, qb, preferred_element_type=F32)
        s = jnp.where(visible, s, NEG)
        m0 = jnp.max(s, axis=0, keepdims=True)
        p = jnp.exp(s - m0)
        m_ref[hh] = m0
        acc_ref[hh] = jnp.dot(jnp.where(in_head[hh], vt_own, one), p.astype(BF16),
                              preferred_element_type=F32)

    def past_blocks(n, nblk):
        off = pl.multiple_of(n * tb, tb)
        k = k_ref[0, pl.ds(off, nblk * tb), :]
        vt = vt_ref[0, :, pl.ds(off, nblk * tb)]
        scores = [jnp.dot(k, qs[hh], preferred_element_type=F32) for hh in range(2)]
        probs, alphas = [], []
        for hh in range(2):
            parts = []
            for j in range(nblk):
                chosen = sel_ref[hh, pl.ds(n + j, 1), :]
                parts.append(jnp.where(chosen > 0.0, scores[hh][j * tb:(j + 1) * tb, :], NEG))
            m_old = m_ref[hh]
            m_new = m_old
            for part in parts:
                m_new = jnp.maximum(m_new, jnp.max(part, axis=0, keepdims=True))
            alphas.append(jnp.exp(m_old - m_new))
            probs.append(jnp.concatenate([jnp.exp(part - m_new).astype(BF16) for part in parts], axis=0))
            m_ref[hh] = m_new
        ones_v = jnp.ones_like(vt)
        pv = [jnp.dot(jnp.where(in_head[hh], vt, ones_v), probs[hh], preferred_element_type=F32)
              for hh in range(2)]
        for hh in range(2):
            acc_ref[hh] = acc_ref[hh] * alphas[hh] + pv[hh]

    def eight_blocks(t, c):
        past_blocks(8 * t, 8)
        return c

    lax.fori_loop(0, i // 8, eight_blocks, 0)
    rem = i % 8

    @pl.when(rem >= 4)
    def _():
        past_blocks(i - rem, 4)

    @pl.when(rem % 4 >= 2)
    def _():
        past_blocks(i - rem % 4, 2)

    @pl.when(rem % 2 == 1)
    def _():
        past_blocks(i - 1, 1)

    a0, a1 = acc_ref[0], acc_ref[1]
    den0 = a0[HEAD_DIM:HEAD_DIM + 1, :]
    den1 = a1[0:1, :]
    o_ref[0] = jnp.where(in_head[0], a0 / den0, a1 / den1).T


def moba_attention(q_t, k_rot, v_t, kmean):
    B, S, _ = k_rot.shape
    tb = MOBA_BLOCK
    nb = S // tb
    npairs = OD_W // LANES
    nbp = -(-nb // BF16_ROWS) * BF16_ROWS
    kmean = jnp.pad(kmean, ((0, 0), (0, nbp - nb), (0, 0)))
    return pl.pallas_call(
        _moba_kernel,
        grid=(B, npairs, nb),
        in_specs=[pl.BlockSpec((1, LANES, tb), lambda b, p, i: (b, p, i)),
                  pl.BlockSpec((1, S, LANES), lambda b, p, i: (b, 0, p)),
                  pl.BlockSpec((1, LANES, S), lambda b, p, i: (b, p, 0)),
                  pl.BlockSpec((1, nbp, LANES), lambda b, p, i: (b, 0, p))],
        out_specs=pl.BlockSpec((1, tb, LANES), lambda b, p, i: (b, i, p)),
        out_shape=jax.ShapeDtypeStruct((B, S, OD_W), F32),
        scratch_shapes=[pltpu.VMEM((2, LANES, tb), F32), pltpu.VMEM((2, 1, tb), F32),
                        pltpu.VMEM((2, nbp, tb), F32)],
        compiler_params=_cparams(("parallel", "parallel", "arbitrary")),
        name="moba_attention",
    )(q_t, k_rot, v_t, kmean)


def _rglru_kernel(x_ref, gate_ref, cw_ref, cb_ref, wa_ref, ba_ref, wx_ref, bx_ref, lam_ref, o_ref,
                  buf_ref, xprev_ref, hprev_ref, *, ts):
    @pl.when(pl.program_id(1) == 0)
    def _():
        xprev_ref[...] = jnp.zeros_like(xprev_ref)
        hprev_ref[...] = jnp.zeros_like(hprev_ref)

    x = x_ref[0]
    buf_ref[0:8, :] = xprev_ref[...]
    buf_ref[8:8 + ts, :] = x
    xprev_ref[...] = x[ts - 8:ts, :]
    xc = cb_ref[...]
    for kk in range(RG_CONV):
        start = 8 - (RG_CONV - 1) + kk
        xc = xc + cw_ref[kk:kk + 1, :] * buf_ref[start:start + ts, :]
    r = jax.nn.sigmoid(_dot(xc, wa_ref[...]) + ba_ref[...])
    ig = jax.nn.sigmoid(_dot(xc, wx_ref[...]) + bx_ref[...])
    log_a = (-RG_C) * r * _softplus(-lam_ref[...])
    a = jnp.exp(log_a)
    u = jnp.sqrt(-jnp.tanh(log_a) * (a * a + 1.0)) * (ig * xc)
    t_idx = lax.broadcasted_iota(jnp.int32, (ts, 1), 0)
    d = 1
    while d < ts:
        valid = t_idx >= d
        a_sh = pltpu.roll(a, d, axis=0)
        u_sh = pltpu.roll(u, d, axis=0)
        u = jnp.where(valid, a * u_sh, 0.0) + u
        a = jnp.where(valid, a * a_sh, a)
        d *= 2
    h = a * hprev_ref[...] + u
    hprev_ref[...] = h[ts - 1:ts, :]
    o_ref[0] = h * jax.nn.gelu(gate_ref[0], approximate=True)


def rg_lru_mixer(proj3, conv_w, conv_b, wa_bd, ba, wx_bd, bx, lam, ts=256):
    B, S, _ = proj3.shape
    W = conv_w.shape[1]
    full = lambda shape: pl.BlockSpec(shape, lambda b, t: (0,) * len(shape))
    row = lambda v: v.reshape(1, W)
    return pl.pallas_call(
        functools.partial(_rglru_kernel, ts=ts),
        grid=(B, S // ts),
        in_specs=[pl.BlockSpec((1, ts, W), lambda b, t: (b, t, OD_RX // 4)),
                  pl.BlockSpec((1, ts, W), lambda b, t: (b, t, OD_RG // 4)),
                  full((RG_CONV, W)), full((1, W)), full((W, W)), full((1, W)), full((W, W)),
                  full((1, W)), full((1, W))],
        out_specs=pl.BlockSpec((1, ts, W), lambda b, t: (b, t, 0)),
        out_shape=jax.ShapeDtypeStruct((B, S, W), F32),
        scratch_shapes=[pltpu.VMEM((ts + 8, W), F32), pltpu.VMEM((8, W), F32), pltpu.VMEM((1, W), F32)],
        compiler_params=_cparams(("parallel", "arbitrary")),
        name="rg_lru",
    )(proj3, proj3, conv_w, row(conv_b), wa_bd, row(ba), wx_bd, row(bx), row(lam))


def _xattn_kernel(x_ref, a_ref, b_ref, wmix_ref, kv_ref, g_ref, wq_ref, qn_ref, kn_ref, wo_ref, o_ref):
    ka = a_ref.shape[2]
    x = x_ref[0] + _dot(a_ref[0], wmix_ref[0:ka, :]) + _dot(b_ref[0], wmix_ref[ka:, :])
    q = _dot(_rms(x, g_ref[...]), wq_ref[...])
    kv = kv_ref[0]
    hw = XA_HEADS * XA_DH
    outs = []
    for h in range(XA_HEADS):
        hl = slice(h * XA_DH, (h + 1) * XA_DH)
        qh = _rms(q[:, hl], qn_ref[...])
        kh = _rms(kv[:, hl], kn_ref[...])
        vh = kv[:, hw + h * XA_DH: hw + (h + 1) * XA_DH]
        s = _dot_nt(qh, kh) * (XA_DH ** -0.5)
        s = s - jnp.max(s, axis=1, keepdims=True)
        p = jnp.exp(s)
        p = p / jnp.sum(p, axis=1, keepdims=True)
        outs.append(_dot(p, vh))
    o_ref[0] = x + _dot(jnp.concatenate(outs, axis=1), wo_ref[...])


def cross_attention(x3, a3, b3, wmix_bf16, kv3, g, wq_bf16, qn, kn, wo_bf16, tq=512):
    B, S, D = x3.shape
    M = kv3.shape[1]
    ka, kb = a3.shape[2], b3.shape[2]
    hw = XA_HEADS * XA_DH
    full = lambda shape: pl.BlockSpec(shape, lambda b, i: (0,) * len(shape))
    return pl.pallas_call(
        _xattn_kernel,
        grid=(B, S // tq),
        in_specs=[pl.BlockSpec((1, tq, D), lambda b, i: (b, i, 0)),
                  pl.BlockSpec((1, tq, ka), lambda b, i: (b, i, 0)),
                  pl.BlockSpec((1, tq, kb), lambda b, i: (b, i, 0)),
                  full((ka + kb, D)),
                  pl.BlockSpec((1, M, 2 * hw), lambda b, i: (b, 0, 0)),
                  full((1, D)), full((D, hw)), full((1, XA_DH)), full((1, XA_DH)), full((hw, D))],
        out_specs=pl.BlockSpec((1, tq, D), lambda b, i: (b, i, 0)),
        out_shape=jax.ShapeDtypeStruct((B, S, D), F32),
        compiler_params=_cparams(("parallel", "parallel")),
        name="cross_attention",
    )(x3, a3, b3, wmix_bf16, kv3, g.reshape(1, D), wq_bf16, qn.reshape(1, -1), kn.reshape(1, -1), wo_bf16)


BF16_ROWS = 16


def _routing_stages(s1, s2, sub, finish, nstage):
    K = PEER_TOPK
    st = {"a1": s1, "a2": s2, "v1": [], "v2": [], "rank2": jnp.full(s2.shape, float(K), F32)}

    def extract(src, dst, rounds, want_rank):
        def run(nil):
            arr = st[src]
            gone = nil - jnp.inf
            for r in rounds:
                m = jnp.max(arr, axis=0, keepdims=True)
                st[dst].append(m)
                hit = arr == m
                if want_rank:
                    st["rank2"] = jnp.where(hit, float(r), st["rank2"])
                arr = jnp.where(hit, gone, arr)
            st[src] = arr
            st["last"] = m
        return run

    def candidates(nil):
        v1, v2 = st["v1"], st["v2"]
        v1m = jnp.concatenate(v1, axis=0)
        v2m = jnp.concatenate(v2, axis=0)
        groups = [(v1[0] + nil) + v2m]
        for a in range(1, 8):
            groups.append(jnp.where(sub < K // (a + 1), v1[a] + v2m[0:8], -jnp.inf))
        groups.append(v1m[8:16] + v2[0])
        st["cand"] = jnp.concatenate(groups, axis=0)
        st["top"] = []
        st["last"] = v2[0]

    def threshold(nil):
        top = st["top"]
        st["tau"] = top[K - 1]
        z = nil
        for tv in top:
            z = z + jnp.exp(tv - top[0])
        st["z"] = z
        st["last"] = z

    def counts(nil):
        v1, v2, tau = st["v1"], st["v2"], st["tau"]
        count = jnp.zeros(s1.shape, F32)
        for bb in range(K // 2):
            count = count + jnp.where(s1 + v2[bb] >= tau, 1.0, 0.0)
        best = nil
        for bb in range(K // 2, K):
            best = best + jnp.where(v1[0] + v2[bb] >= tau, 1.0, 0.0)
        count = count + jnp.where(s1 == v1[0], best, 0.0)
        finish(count, jnp.exp(s1 - v1[0]) / st["z"], st["rank2"].astype(BF16),
               jnp.exp(s2 - v2[0]).astype(BF16))
        st["last"] = best

    quarter = [range(q * 4, q * 4 + 4) for q in range(4)]
    work = ([extract("a1", "v1", r, False) for r in quarter]
            + [extract("a2", "v2", r, True) for r in quarter]
            + [candidates] + [extract("cand", "top", r, False) for r in quarter]
            + [threshold, counts])
    bounds = [round(j * len(work) / nstage) for j in range(nstage + 1)]

    def stage(j):
        def run(nil):
            for piece in work[bounds[j]:bounds[j + 1]]:
                piece(nil)
            return st["last"]
        return run

    return [stage(j) for j in range(nstage)]


def _peer_kernel(x_ref, xnext_ref, g_ref, wq_ref, sk_ref, u_ref, vt_ref, o_ref,
                 xn_ref, xnt_ref, c_ref, a_ref, r2_ref, e2_ref, acc_ref, s_ref, *, ib, isub):
    t = pl.program_id(0)
    e = pl.program_id(1)
    nk = PEER_NKEYS
    tq = x_ref.shape[0]
    slot = t % 2
    nslot = 1 - slot
    sub = lax.broadcasted_iota(jnp.int32, (8, 1), 0)
    nsub = ib // isub

    def prepare(src_ref, s):
        xn32 = _rms(src_ref[...], g_ref[...])
        xn_ref[...] = xn32.astype(BF16)
        xnt_ref[s] = xn32.T.astype(BF16)

    def route_scores(h):
        q = jnp.dot(xn_ref[...], wq_ref[h], preferred_element_type=F32)
        return _dot_nt(sk_ref[2 * h], q[:, :nk]), _dot_nt(sk_ref[2 * h + 1], q[:, nk:])

    def table_writer(h, s):
        def finish(count, amp, rank2, e2):
            c_ref[s, h] = count
            a_ref[s, h] = amp
            r2_ref[s, h] = rank2
            e2_ref[s, h] = e2
        return finish

    @pl.when(jnp.logical_and(t == 0, e == 0))
    def _():
        prepare(x_ref, 0)

        def one(h, c):
            s1, s2 = route_scores(h)
            for run in _routing_stages(s1, s2, sub, table_writer(h, 0), 1):
                run(jnp.zeros((1, tq), F32))
            return c

        lax.fori_loop(0, PEER_HEADS, one, 0)

    @pl.when(e == 0)
    def _():
        acc_ref[...] = jnp.zeros_like(acc_ref)
        prepare(xnext_ref, nslot)
        s_ref[0], s_ref[1] = route_scores(0)

    ngrp = nk // BF16_ROWS
    xnt = xnt_ref[slot]
    first = pl.multiple_of(e * ib, ib)
    step_c = [c_ref[slot, h, pl.ds(first, ib), :] for h in range(PEER_HEADS)]
    step_a = [a_ref[slot, h, pl.ds(first, ib), :] for h in range(PEER_HEADS)]

    def expert_matmul(sc):
        lo = sc * isub * nk
        u_sub = pltpu.bitcast(u_ref[lo // 2:(lo + isub * nk) // 2, :], BF16)
        return jnp.dot(u_sub, xnt, preferred_element_type=F32)

    stages = _routing_stages(s_ref[0], s_ref[1], sub, table_writer(e, nslot), nsub)
    hid_next = expert_matmul(0)
    ahead_s1, ahead_s2 = route_scores(jnp.minimum(e + 1, PEER_HEADS - 1))

    total = None
    after = jnp.zeros((1, tq), F32)
    for sc in range(nsub):
        zero = jnp.broadcast_to(stages[sc](after) * 0.0, (BF16_ROWS, tq)).astype(BF16)
        lo = sc * isub * nk
        hid_sub = hid_next
        if sc + 1 < nsub:
            hid_next = expert_matmul(sc + 1)
        pieces = []
        for ii in range(isub):
            il = sc * isub + ii
            hid = hid_sub[ii * nk:(ii + 1) * nk, :]
            act = (0.5 * hid * (1.0 + lax.erf(hid * np.float32(np.sqrt(0.5))))).astype(BF16)
            w = [zero] * ngrp
            for h in range(PEER_HEADS):
                cnt = jnp.broadcast_to(step_c[h][il:il + 1, :], (BF16_ROWS, tq)).astype(BF16)
                amp = jnp.broadcast_to(step_a[h][il:il + 1, :], (BF16_ROWS, tq)).astype(BF16)
                for gi in range(ngrp):
                    rows = slice(gi * BF16_ROWS, (gi + 1) * BF16_ROWS)
                    w[gi] = w[gi] + jnp.where(r2_ref[slot, h, rows, :] < cnt,
                                              e2_ref[slot, h, rows, :] * amp, zero)
            pieces += [w[gi] * act[gi * BF16_ROWS:(gi + 1) * BF16_ROWS, :] for gi in range(ngrp)]
        after = pieces[-1][0:1, :].astype(F32) * 0.0
        vt_sub = pltpu.bitcast(vt_ref[:, lo:lo + isub * nk], BF16)
        part = jnp.dot(vt_sub, jnp.concatenate(pieces, axis=0), preferred_element_type=F32)
        total = part if total is None else total + part
    acc_ref[...] += total
    s_ref[0] = ahead_s1
    s_ref[1] = ahead_s2

    @pl.when(e == pl.num_programs(1) - 1)
    def _():
        o_ref[...] = x_ref[...] + acc_ref[...].T


def _pair_rows_kernel(x_ref, o_ref, *, transpose):
    x = x_ref[0]
    if transpose:
        x = x.T
    o_ref[...] = pltpu.bitcast(x.astype(BF16), jnp.uint32)


def _pair_rows(w3, layer, transpose=False, tr=512):
    _, R, C = w3.shape
    in_spec = pl.BlockSpec((1, tr, C), lambda i: (layer, i, 0))
    if transpose:
        out_spec = pl.BlockSpec((C // 2, tr), lambda i: (0, i))
        out_shape = jax.ShapeDtypeStruct((C // 2, R), jnp.uint32)
    else:
        out_spec = pl.BlockSpec((tr // 2, C), lambda i: (i, 0))
        out_shape = jax.ShapeDtypeStruct((R // 2, C), jnp.uint32)
    return pl.pallas_call(
        functools.partial(_pair_rows_kernel, transpose=transpose),
        grid=(R // tr,), in_specs=[in_spec], out_specs=out_spec, out_shape=out_shape,
        compiler_params=_cparams(("parallel",)),
        name="pair_rows",
    )(w3)


def peer_ffn(x2d, g, wq3_bf16, sk_bf16, u_pairs, vt_pairs, tq=256, isub=2):
    T, D = x2d.shape
    H = PEER_HEADS
    ne = H
    ib = PEER_NKEYS // ne
    ec = ib * PEER_NKEYS
    nt = T // tq
    full = lambda shape: pl.BlockSpec(shape, lambda i, e: (0,) * len(shape))
    words = pltpu.VMEM((2, H, PEER_NKEYS, tq), F32)
    halves = pltpu.VMEM((2, H, PEER_NKEYS, tq), BF16)
    return pl.pallas_call(
        functools.partial(_peer_kernel, ib=ib, isub=isub),
        grid=(nt, ne),
        in_specs=[pl.BlockSpec((tq, D), lambda i, e: (i, 0)),
                  pl.BlockSpec((tq, D), lambda i, e: (jnp.minimum(i + 1, nt - 1), 0)),
                  full((1, D)), full((H, D, 2 * PEER_NKEYS)), full((2 * H, PEER_NKEYS, PEER_NKEYS)),
                  pl.BlockSpec((ec // 2, D), lambda i, e: (e, 0)),
                  pl.BlockSpec((D // 2, ec), lambda i, e: (0, e))],
        out_specs=pl.BlockSpec((tq, D), lambda i, e: (i, 0)),
        out_shape=jax.ShapeDtypeStruct((T, D), F32),
        scratch_shapes=[pltpu.VMEM((tq, D), BF16), pltpu.VMEM((2, D, tq), BF16),
                        words, words, halves, halves, pltpu.VMEM((D, tq), F32),
                        pltpu.VMEM((2, PEER_NKEYS, tq), F32)],
        compiler_params=_cparams(("arbitrary", "arbitrary")),
        name="peer_ffn",
    )(x2d, x2d, g.reshape(1, D), wq3_bf16, sk_bf16, u_pairs, vt_pairs)


def _block_diag(w):
    G, n, _ = w.shape
    eye = jnp.eye(G, dtype=w.dtype)
    return (eye[:, None, :, None] * w[:, :, None, :]).reshape(G * n, G * n)


def even_layer(x2d, B, S, norm, w_in, gate_up, gate_b, out_norm, w_out):
    D = x2d.shape[1]
    gg0 = EV_GG * LANES
    wp = jnp.concatenate([w_in[:, :1536], w_in[:, 1536 + GLA_RANK:], w_in[:, 1536:1536 + GLA_RANK],
                          jnp.zeros((D, EV_COLS - gg0 - GLA_RANK), w_in.dtype)], axis=1).astype(BF16)
    proj = norm_proj(x2d, norm, wp).reshape(B, S, EV_COLS)
    gup = jnp.concatenate([gate_up, jnp.zeros((LANES - GLA_RANK, gate_up.shape[1]), gate_up.dtype)], axis=0)
    a_out = gla_mixer(proj, gup.astype(BF16), gate_b, out_norm)
    b_out = sb_attention(proj)
    return a_out, b_out, w_out.astype(BF16)


def odd_layer(x2d, B, S, positions, norm, w_in, q_norm, k_norm, conv_w, conv_b, wa, ba, wx, bx, lam, w_out):
    proj = norm_proj(x2d, norm, w_in.astype(BF16)).reshape(B, S, -1)
    half = HEAD_DIM // 2
    inv_freq = ROPE_THETA ** (-jnp.arange(0, HEAD_DIM, 2, dtype=F32) / HEAD_DIM)
    inv_tile = jnp.tile(inv_freq, LANES // half).reshape(1, LANES)
    seg = np.arange(OD_W) // HEAD_DIM
    blockdiag = jnp.asarray(seg[:, None] == seg[None, :], dtype=BF16)
    q_t, k_rot, v_t, kmean = moba_prep(proj, positions.reshape(B, S, 1), inv_tile,
                                       jnp.tile(q_norm, OD_W // HEAD_DIM).reshape(1, OD_W),
                                       jnp.tile(k_norm, OD_W // HEAD_DIM).reshape(1, OD_W), blockdiag)
    c_out = moba_attention(q_t, k_rot, v_t, kmean.reshape(B, S // MOBA_BLOCK, OD_W))
    d_out = rg_lru_mixer(proj, conv_w, conv_b, _block_diag(wa).astype(BF16), ba,
                         _block_diag(wx).astype(BF16), bx, lam)
    return c_out, d_out, w_out.astype(BF16)


def kernel(x, mem, positions, ev_norm, ev_w_in, ev_gla_gate_up, ev_gla_gate_b, ev_gla_out_norm, ev_w_out, od_norm, od_w_in, od_q_norm, od_k_norm, od_conv_w, od_conv_b, od_gate_a_w, od_gate_a_b, od_gate_x_w, od_gate_x_b, od_lambda, od_w_out, xa_norm, xa_mem_norm, xa_wq, xa_wkv, xa_q_norm, xa_k_norm, xa_wo, ffn_norm, peer_wq, peer_subkeys, peer_u, peer_v):
    B, S, D = x.shape
    M = mem.shape[1]
    depth = xa_norm.shape[0]
    x2d = x.reshape(B * S, D)
    mem2d = mem.reshape(B * M, D)
    for l in range(depth):
        if l % 2 == 0:
            e = l // 2
            mix = even_layer(x2d, B, S, ev_norm[e], ev_w_in[e], ev_gla_gate_up[e], ev_gla_gate_b[e],
                             ev_gla_out_norm[e], ev_w_out[e])
        else:
            o = l // 2
            mix = odd_layer(x2d, B, S, positions, od_norm[o], od_w_in[o], od_q_norm[o], od_k_norm[o],
                            od_conv_w[o], od_conv_b[o], od_gate_a_w[o], od_gate_a_b[o],
                            od_gate_x_w[o], od_gate_x_b[o], od_lambda[o], od_w_out[o])
        kv = norm_proj(mem2d, xa_mem_norm[l], xa_wkv[l].astype(BF16))
        x2d = cross_attention(x2d.reshape(B, S, D), *mix, kv.reshape(B, M, -1), xa_norm[l],
                              xa_wq[l].astype(BF16), xa_q_norm[l], xa_k_norm[l],
                              xa_wo[l].astype(BF16)).reshape(B * S, D)
        H = PEER_HEADS
        wq3 = peer_wq[l].reshape(D, H, 2 * PEER_NKEYS).transpose(1, 0, 2).astype(BF16)
        sk = peer_subkeys[l].reshape(2 * H, PEER_NKEYS, -1).astype(BF16)
        x2d = peer_ffn(x2d, ffn_norm[l], wq3, sk, _pair_rows(peer_u, l),
                       _pair_rows(peer_v, l, transpose=True))
    return x2d.reshape(B, S, D)
```
